```python
import jax
import jax.numpy as jnp
from jax import lax
import numpy as np

D_MODEL = 1024
BATCH = 4
SEQ = 4096
DEPTH = 2

GRID_W = 64
CTX_LEN = 256
N_BRANCH = 4
BRANCH_W = D_MODEL // 4
HEAD_DIM = 64
MLA_HEADS = 4
MLA_NOPE = 64
MLA_ROPE = 32
MLA_V = 64
MLA_Q_RANK = 256
MLA_KV_RANK = 128
SWA_Q_HEADS = 4
SWA_KV_HEADS = 2
WINDOW = 128
SWA_BLOCK = 128
AXA_Q_HEADS = 4
AXA_KV_HEADS = 2
LRU_WIDTH = BRANCH_W
LRU_BLOCKS = 4
LRU_BLOCK = LRU_WIDTH // LRU_BLOCKS
LRU_C = 8.0
CONV_W = 4
CONV_LEFT = 2
Q_BLOCK = 128
ROPE_THETA = 10000.0
RMS_EPS = 1e-6
NEG_INF = -1e30

IN_SIZES = (MLA_Q_RANK, MLA_KV_RANK, MLA_ROPE,
            SWA_Q_HEADS * HEAD_DIM, SWA_KV_HEADS * HEAD_DIM, SWA_KV_HEADS * HEAD_DIM,
            AXA_Q_HEADS * HEAD_DIM, AXA_KV_HEADS * HEAD_DIM, AXA_KV_HEADS * HEAD_DIM,
            LRU_WIDTH,
            N_BRANCH * BRANCH_W,
            N_BRANCH * D_MODEL)
IN_COLS = sum(IN_SIZES)

kernel_name = 'hybrid_parallel_mla_swa_axial_rglru_dit'


def _rms_norm(x, gain):
    x32 = x.astype(jnp.float32)
    y = x32 * lax.rsqrt(jnp.mean(x32 * x32, axis=-1, keepdims=True) + RMS_EPS)
    return (y * gain.astype(jnp.float32)).astype(x.dtype)


def _split_in(z):
    idx, acc = [], 0
    for s in IN_SIZES[:-1]:
        acc += s
        idx.append(acc)
    return jnp.split(z, idx, axis=-1)


def _axial_rope(n, rot_dim):
    n_rows = n // GRID_W
    rows = jnp.repeat(jnp.arange(n_rows, dtype=jnp.float32), GRID_W)
    cols = jnp.tile(jnp.arange(GRID_W, dtype=jnp.float32), n_rows)
    quarter = rot_dim // 4
    freqs = ROPE_THETA ** (-jnp.arange(quarter, dtype=jnp.float32) / quarter)
    ang = jnp.concatenate([rows[:, None] * freqs, cols[:, None] * freqs], axis=-1)
    return jnp.cos(ang), jnp.sin(ang)


def _apply_rope(x, cos, sin):
    half = x.shape[-1] // 2
    cos = cos[None, :, None, :].astype(x.dtype)
    sin = sin[None, :, None, :].astype(x.dtype)
    x1, x2 = x[..., :half], x[..., half:]
    return jnp.concatenate([x1 * cos - x2 * sin, x2 * cos + x1 * sin], axis=-1)


def _rope_tail(x, cos, sin):
    r = 2 * cos.shape[-1]
    return jnp.concatenate([x[..., :-r], _apply_rope(x[..., -r:], cos, sin)], axis=-1)


def _short_conv(u, w, b):
    n = u.shape[1]
    up = jnp.pad(u, ((0, 0), (CONV_LEFT, CONV_W - 1 - CONV_LEFT), (0, 0)))
    y = b
    for j in range(CONV_W):
        y = y + up[:, j:j + n] * w[j]
    return y


def _softmax_with_sink(s, sink):
    m = jnp.maximum(jnp.max(s, axis=-1, keepdims=True), sink)
    p = jnp.exp(s - m)
    return p / (jnp.sum(p, axis=-1, keepdims=True) + jnp.exp(sink - m))


def _dense_attention(q, k, v):
    b, n, hkv, g, d = q.shape
    nb = n // Q_BLOCK
    scale = d ** -0.5
    qb = jnp.moveaxis(q.reshape(b, nb, Q_BLOCK, hkv, g, d), 1, 0)

    def block(qblk):
        s = jnp.einsum('bqhgd,bkhd->bhgqk', qblk, k).astype(jnp.float32) * scale
        p = jax.nn.softmax(s, axis=-1).astype(v.dtype)
        return jnp.einsum('bhgqk,bkhd->bqhgd', p, v)

    out = lax.map(block, qb)
    return jnp.moveaxis(out, 0, 1).reshape(b, n, -1)


def _window_attention(q, k, v, k_ctx, v_ctx, sink):
    b, n, hkv, g, d = q.shape
    w = SWA_BLOCK
    nb = n // w
    scale = d ** -0.5
    pad = ((0, 0), (w, w), (0, 0), (0, 0))
    kp = jnp.pad(k, pad).reshape(b, nb + 2, w, hkv, d)
    vp = jnp.pad(v, pad).reshape(b, nb + 2, w, hkv, v.shape[-1])
    kb = jnp.concatenate([kp[:, :-2], kp[:, 1:-1], kp[:, 2:]], axis=2)
    vb = jnp.concatenate([vp[:, :-2], vp[:, 1:-1], vp[:, 2:]], axis=2)
    qb = q.reshape(b, nb, w, hkv, g, d)
    s_loc = jnp.einsum('bnqhgd,bnkhd->bnhgqk', qb, kb).astype(jnp.float32) * scale
    s_ctx = jnp.einsum('bnqhgd,bkhd->bnhgqk', qb, k_ctx).astype(jnp.float32) * scale
    qi = jnp.arange(w)[:, None]
    ki = jnp.arange(3 * w)[None, :]
    key_pos = (jnp.arange(nb)[:, None, None] - 1) * w + ki[None]
    rel = ki - qi
    valid = ((rel >= w - WINDOW) & (rel <= w + WINDOW))[None] & (key_pos >= 0) & (key_pos < n)
    s_loc = jnp.where(valid[None, :, None, None], s_loc, NEG_INF)
    p = _softmax_with_sink(jnp.concatenate([s_loc, s_ctx], axis=-1),
                           sink[None, None, :, :, None, None])
    p_loc = p[..., :3 * w].astype(v.dtype)
    p_ctx = p[..., 3 * w:].astype(v.dtype)
    out = (jnp.einsum('bnhgqk,bnkhd->bnqhgd', p_loc, vb)
           + jnp.einsum('bnhgqk,bkhd->bnqhgd', p_ctx, v_ctx))
    return out.reshape(b, n, -1)


def _sink_attention(q, k, v, sink):
    b, n = q.shape[:2]
    scale = q.shape[-1] ** -0.5
    s = jnp.einsum('bqhgd,bkhd->bhgqk', q, k).astype(jnp.float32) * scale
    p = _softmax_with_sink(s, sink[None, :, :, None, None]).astype(v.dtype)
    return jnp.einsum('bhgqk,bkhd->bqhgd', p, v).reshape(b, n, -1)


def _scan_combine(left, right):
    a_l, b_l = left
    a_r, b_r = right
    return a_l * a_r, a_r * b_l + b_r


def _rglru_dir(u, w_r, b_r, w_i, b_i, lam, h0, reverse):
    u32 = u.astype(jnp.float32)
    shp = u32.shape
    ub = u32.reshape(shp[:-1] + (LRU_BLOCKS, LRU_BLOCK))
    r = jax.nn.sigmoid(jnp.einsum('bnki,kij->bnkj', ub, w_r.astype(jnp.float32)).reshape(shp)
                       + b_r.astype(jnp.float32))
    i = jax.nn.sigmoid(jnp.einsum('bnki,kij->bnkj', ub, w_i.astype(jnp.float32)).reshape(shp)
                       + b_i.astype(jnp.float32))
    log_a = -LRU_C * r * jax.nn.softplus(-lam.astype(jnp.float32))
    a = jnp.exp(log_a)
    bx = jnp.sqrt(-jnp.expm1(2.0 * log_a)) * (i * u32)
    a_cum, b_cum = lax.associative_scan(_scan_combine, (a, bx), axis=1, reverse=reverse)
    return a_cum * h0[:, None, :] + b_cum


def _rglru_branch(u_c, u_x, lp, want_ctx):
    bsz = u_x.shape[0]
    ys_x, ys_c = [], []
    for d, reverse in enumerate((False, True)):
        args = (lp['lru_w_r'][d], lp['lru_b_r'][d], lp['lru_w_i'][d], lp['lru_b_i'][d],
                lp['lru_lambda'][d])
        h_c = _rglru_dir(u_c, *args, jnp.zeros((bsz, LRU_WIDTH), jnp.float32), reverse)
        h_last = h_c[:, 0] if reverse else h_c[:, -1]
        ys_x.append(_rglru_dir(u_x, *args, h_last, reverse))
        ys_c.append(h_c)
    y_x = (ys_x[0] + ys_x[1]).astype(u_x.dtype)
    y_c = (ys_c[0] + ys_c[1]).astype(u_c.dtype) if want_ctx else None
    return y_x, y_c


def _project_stream(h, lp, rope, need_queries):
    b, n = h.shape[:2]
    (z_cq, z_ckv, z_kr, z_sq, z_sk, z_sv, z_aq, z_ak, z_av,
     z_lru, z_gate, z_merge) = _split_in(h @ lp['w_in'])
    out = {}
    ckv = _rms_norm(z_ckv, lp['mla_ckv_norm'])
    kv = (ckv @ lp['mla_w_ukv']).reshape(b, n, MLA_HEADS, MLA_NOPE + MLA_V)
    k_rope = jnp.broadcast_to(z_kr[:, :, None, :], (b, n, MLA_HEADS, MLA_ROPE))
    mla_k = _rms_norm(jnp.concatenate([kv[..., :MLA_NOPE], k_rope], axis=-1), lp['mla_k_norm'])
    out['mla_v'] = kv[..., MLA_NOPE:]
    swa_k = _rms_norm(z_sk.reshape(b, n, SWA_KV_HEADS, HEAD_DIM), lp['swa_k_norm'])
    out['swa_v'] = z_sv.reshape(b, n, SWA_KV_HEADS, HEAD_DIM)
    axa_k = _rms_norm(z_ak.reshape(b, n, AXA_KV_HEADS, HEAD_DIM), lp['axa_k_norm'])
    out['axa_v'] = z_av.reshape(b, n, AXA_KV_HEADS, HEAD_DIM)
    if rope is not None:
        mla_k = _rope_tail(mla_k, *rope['mla'])
        swa_k = _apply_rope(swa_k, *rope['head'])
        axa_k = _apply_rope(axa_k, *rope['head'])
    out['mla_k'] = mla_k
    out['swa_k'] = swa_k
    out['axa_k'] = axa_k
    out['lru_u'] = _short_conv(z_lru, lp['lru_conv_w'], lp['lru_conv_b'])
    if need_queries:
        cq = _rms_norm(z_cq, lp['mla_cq_norm'])
        mla_q = _rms_norm((cq @ lp['mla_w_uq']).reshape(b, n, MLA_HEADS, MLA_NOPE + MLA_ROPE),
                          lp['mla_q_norm'])
        swa_q = _rms_norm(z_sq.reshape(b, n, SWA_Q_HEADS, HEAD_DIM), lp['swa_q_norm'])
        axa_q = _rms_norm(z_aq.reshape(b, n, AXA_Q_HEADS, HEAD_DIM), lp['axa_q_norm'])
        if rope is not None:
            mla_q = _rope_tail(mla_q, *rope['mla'])
            swa_q = _apply_rope(swa_q, *rope['head'])
            axa_q = _apply_rope(axa_q, *rope['head'])
        out['mla_q'] = mla_q[:, :, :, None, :]
        out['swa_q'] = swa_q.reshape(b, n, SWA_KV_HEADS, SWA_Q_HEADS // SWA_KV_HEADS, HEAD_DIM)
        out['axa_q'] = axa_q.reshape(b, n, AXA_KV_HEADS, AXA_Q_HEADS // AXA_KV_HEADS, HEAD_DIM)
        out['gate'] = z_gate
        out['merge'] = z_merge
    return out


def _merge_branches(branches, z_gate, z_merge, lp):
    b, n = z_gate.shape[:2]
    y = jnp.stack(branches, axis=2) * jax.nn.silu(z_gate.reshape(b, n, N_BRANCH, BRANCH_W))
    proj = jnp.einsum('bnkw,kwd->bnkd', y, lp['w_branch'])
    mix = jnp.sum(jax.nn.sigmoid(z_merge.reshape(b, n, N_BRANCH, D_MODEL)) * proj, axis=2)
    return mix @ lp['w_out']


def _layer(x, ctx, mod_x, mod_c, lp, rope, update_ctx):
    shift_x, scale_x, gate_x = jnp.split(mod_x[:, None, :], 3, axis=-1)
    shift_c, scale_c, gate_c = jnp.split(mod_c, 3, axis=-1)
    hx = _rms_norm(x, lp['norm_w']) * (1 + scale_x) + shift_x
    hc = _rms_norm(ctx, lp['norm_w']) * (1 + scale_c) + shift_c
    px = _project_stream(hx, lp, rope, True)
    pc = _project_stream(hc, lp, None, update_ctx)
    sink = lp['swa_sink'].astype(jnp.float32).reshape(SWA_KV_HEADS, SWA_Q_HEADS // SWA_KV_HEADS)
    ya = _dense_attention(px['mla_q'], jnp.concatenate([pc['mla_k'], px['mla_k']], axis=1),
                          jnp.concatenate([pc['mla_v'], px['mla_v']], axis=1))
    yb = _window_attention(px['swa_q'], px['swa_k'], px['swa_v'], pc['swa_k'], pc['swa_v'], sink)
    yc = _dense_attention(px['axa_q'], jnp.concatenate([pc['axa_k'], px['axa_k']], axis=1),
                          jnp.concatenate([pc['axa_v'], px['axa_v']], axis=1))
    yd_x, yd_c = _rglru_branch(pc['lru_u'], px['lru_u'], lp, update_ctx)
    x = x + gate_x * _merge_branches((ya, yb, yc, yd_x), px['gate'], px['merge'], lp)
    if update_ctx:
        ya_c = _dense_attention(pc['mla_q'], pc['mla_k'], pc['mla_v'])
        yb_c = _sink_attention(pc['swa_q'], pc['swa_k'], pc['swa_v'], sink)
        yc_c = _dense_attention(pc['axa_q'], pc['axa_k'], pc['axa_v'])
        ctx = ctx + gate_c * _merge_branches((ya_c, yb_c, yc_c, yd_c), pc['gate'], pc['merge'], lp)
    return x, ctx


def setup_inputs(seed: int = 0) -> dict:
    key = jax.random.key(seed)
    ks = jax.random.split(key, 32)
    D = D_MODEL

    def nrm(k, shape, scale):
        return jax.random.normal(k, shape, jnp.float32) * scale

    def gain(k, shape):
        return 1.0 + 0.05 * jax.random.normal(k, shape, jnp.float32)

    a_target = jax.random.uniform(ks[25], (DEPTH, 2, LRU_WIDTH), jnp.float32, 0.9, 0.999)
    s_lam = a_target ** (1.0 / LRU_C)
    return {
        'x': nrm(ks[0], (BATCH, SEQ, D), 1.0),
        'c': nrm(ks[1], (BATCH, D), 1.0),
        'ctx': nrm(ks[2], (BATCH, CTX_LEN, D), 1.0),
        'c_ctx': nrm(ks[3], (D,), 1.0),
        'w_mod': nrm(ks[4], (DEPTH, D, 3 * D), 0.5 * D ** -0.5),
        'b_mod': nrm(ks[5], (DEPTH, 3 * D), 0.02),
        'norm_w': gain(ks[6], (DEPTH, D)),
        'w_in': nrm(ks[7], (DEPTH, D, IN_COLS), D ** -0.5),
        'mla_cq_norm': gain(ks[8], (DEPTH, MLA_Q_RANK)),
        'mla_ckv_norm': gain(ks[9], (DEPTH, MLA_KV_RANK)),
        'mla_w_uq': nrm(ks[10], (DEPTH, MLA_Q_RANK, MLA_HEADS * (MLA_NOPE + MLA_ROPE)), MLA_Q_RANK ** -0.5),
        'mla_w_ukv': nrm(ks[11], (DEPTH, MLA_KV_RANK, MLA_HEADS * (MLA_NOPE + MLA_V)), MLA_KV_RANK ** -0.5),
        'mla_q_norm': gain(ks[12], (DEPTH, MLA_NOPE + MLA_ROPE)),
        'mla_k_norm': gain(ks[13], (DEPTH, MLA_NOPE + MLA_ROPE)),
        'swa_q_norm': gain(ks[14], (DEPTH, HEAD_DIM)),
        'swa_k_norm': gain(ks[15], (DEPTH, HEAD_DIM)),
        'swa_sink': nrm(ks[16], (DEPTH, SWA_Q_HEADS), 0.5),
        'axa_q_norm': gain(ks[17], (DEPTH, HEAD_DIM)),
        'axa_k_norm': gain(ks[18], (DEPTH, HEAD_DIM)),
        'lru_conv_w': nrm(ks[19], (DEPTH, CONV_W, LRU_WIDTH), CONV_W ** -0.5),
        'lru_conv_b': nrm(ks[20], (DEPTH, LRU_WIDTH), 0.02),
        'lru_w_r': nrm(ks[21], (DEPTH, 2, LRU_BLOCKS, LRU_BLOCK, LRU_BLOCK), LRU_BLOCK ** -0.5),
        'lru_b_r': nrm(ks[22], (DEPTH, 2, LRU_WIDTH), 0.02),
        'lru_w_i': nrm(ks[23], (DEPTH, 2, LRU_BLOCKS, LRU_BLOCK, LRU_BLOCK), LRU_BLOCK ** -0.5),
        'lru_b_i': nrm(ks[24], (DEPTH, 2, LRU_WIDTH), 0.02),
        'lru_lambda': jnp.log(s_lam) - jnp.log1p(-s_lam),
        'w_branch': nrm(ks[26], (DEPTH, N_BRANCH, BRANCH_W, D), BRANCH_W ** -0.5),
        'w_out': nrm(ks[27], (DEPTH, D, D), D ** -0.5),
    }


def reference(x, c, ctx, c_ctx, w_mod, b_mod, norm_w, w_in, mla_cq_norm, mla_ckv_norm,
              mla_w_uq, mla_w_ukv, mla_q_norm, mla_k_norm, swa_q_norm, swa_k_norm, swa_sink,
              axa_q_norm, axa_k_norm, lru_conv_w, lru_conv_b, lru_w_r, lru_b_r, lru_w_i,
              lru_b_i, lru_lambda, w_branch, w_out):
    n = x.shape[1]
    rope = {'mla': _axial_rope(n, MLA_ROPE), 'head': _axial_rope(n, HEAD_DIM)}
    sc = jax.nn.silu(c)
    scc = jax.nn.silu(c_ctx)
    for l in range(DEPTH):
        lp = {
            'norm_w': norm_w[l], 'w_in': w_in[l],
            'mla_cq_norm': mla_cq_norm[l], 'mla_ckv_norm': mla_ckv_norm[l],
            'mla_w_uq': mla_w_uq[l], 'mla_w_ukv': mla_w_ukv[l],
            'mla_q_norm': mla_q_norm[l], 'mla_k_norm': mla_k_norm[l],
            'swa_q_norm': swa_q_norm[l], 'swa_k_norm': swa_k_norm[l], 'swa_sink': swa_sink[l],
            'axa_q_norm': axa_q_norm[l], 'axa_k_norm': axa_k_norm[l],
            'lru_conv_w': lru_conv_w[l], 'lru_conv_b': lru_conv_b[l],
            'lru_w_r': lru_w_r[l], 'lru_b_r': lru_b_r[l],
            'lru_w_i': lru_w_i[l], 'lru_b_i': lru_b_i[l], 'lru_lambda': lru_lambda[l],
            'w_branch': w_branch[l], 'w_out': w_out[l],
        }
        mod_x = sc @ w_mod[l] + b_mod[l]
        mod_c = scc @ w_mod[l] + b_mod[l]
        x, ctx = _layer(x, ctx, mod_x, mod_c, lp, rope, l < DEPTH - 1)
    return x
```

```python
import functools

import jax
import jax.numpy as jnp
from jax import lax
from jax.experimental import pallas as pl
from jax.experimental.pallas import tpu as pltpu

GRID_W = 64
N_BRANCH = 4
HEAD_DIM = 64
MLA_HEADS = 4
MLA_NOPE = 64
MLA_ROPE = 32
MLA_V = 64
MLA_QK = MLA_NOPE + MLA_ROPE
SWA_Q_HEADS = 4
SWA_KV_HEADS = 2
WINDOW = 128
AXA_Q_HEADS = 4
AXA_KV_HEADS = 2
LRU_BLOCKS = 4
LRU_C = 8.0
CONV_W = 4
CONV_LEFT = 2
ROPE_THETA = 10000.0
RMS_EPS = 1e-6
NEG_INF = -1e30

LANES = 128
SUBLANES = 8
TILE = 256
VMEM_LIMIT = 56 * 1024 * 1024

F32 = jnp.float32
BF16 = jnp.bfloat16


def _dot(a, b):
    return jnp.dot(a, b, preferred_element_type=F32)


def _dot_nt(a, b):
    return lax.dot_general(a, b, (((1,), (1,)), ((), ())), preferred_element_type=F32)


def _split(a):
    hi = a.astype(BF16)
    lo = (a - hi.astype(F32)).astype(BF16)
    return hi, lo


def _dot3(a, w_hi, w_lo):
    a_hi, a_lo = _split(a)
    return _dot(a_hi, w_hi) + _dot(a_lo, w_hi) + _dot(a_hi, w_lo)


def _sigmoid(v):
    return 1.0 / (1.0 + jnp.exp(-v))


def _cparams(sem):
    return pltpu.CompilerParams(dimension_semantics=sem, vmem_limit_bytes=VMEM_LIMIT)


def _const_spec(shape):
    nd = len(shape)
    return pl.BlockSpec(shape, lambda *_: (0,) * nd)


def _mod_kernel(c_ref, w_ref, b_ref, o_ref):
    v = c_ref[...]
    a = v * _sigmoid(v)
    w_hi, w_lo = _split(w_ref[0])
    o_ref[0] = _dot3(a, w_hi, w_lo) + b_ref[0]


def _modulation(cvec, w_mod, b_mod):
    depth, d, d3 = w_mod.shape
    r = cvec.shape[0]
    bn = 512
    return pl.pallas_call(
        _mod_kernel,
        grid=(depth, d3 // bn),
        in_specs=[pl.BlockSpec((r, d), lambda l, j: (0, 0)),
                  pl.BlockSpec((1, d, bn), lambda l, j: (l, 0, j)),
                  pl.BlockSpec((1, 1, bn), lambda l, j: (l, 0, j))],
        out_specs=pl.BlockSpec((1, r, bn), lambda l, j: (l, 0, j)),
        out_shape=jax.ShapeDtypeStruct((depth, r, d3), F32),
        compiler_params=_cparams(("arbitrary", "arbitrary")),
        name="modulation",
    )(cvec, w_mod, b_mod.reshape(depth, 1, d3))


def _modulated_norm(x, mod_ref, nw_ref):
    ms = jnp.mean(x * x, axis=-1, keepdims=True)
    y = x * lax.rsqrt(ms + RMS_EPS) * nw_ref[...]
    return y * (1.0 + mod_ref[0, 1:2, :]) + mod_ref[0, 0:1, :]


def _rms_lanes(v, gain, n):
    ms = jnp.sum(v * v, axis=-1, keepdims=True) * (1.0 / n)
    return v * lax.rsqrt(ms + RMS_EPS) * gain


def _rms_heads(v, gain, gmat):
    hi, lo = _split(v * v)
    ss = _dot(hi, gmat) + _dot(lo, gmat)
    return v * lax.rsqrt(ss * (1.0 / HEAD_DIM) + RMS_EPS) * gain


def _rope_slab(xs, cos, sin, first, shift):
    rot = jnp.where(first, -pltpu.roll(xs, LANES - shift, 1), pltpu.roll(xs, shift, 1))
    return xs * cos + rot * sin


def _proj_kernel(x_ref, mod_ref, nw_ref, wp_ref, wuq_ref, wuk_ref, wuv_ref,
                 g_cq, g_ckv, g_mq, g_mk, g_sq, g_sk, g_aq, g_ak, gmat_ref,
                 cosm_ref, sinm_ref, cosh_ref, sinh_ref,
                 mq_o, mk_o, mv_o, sq_o, sk_o, sv_o, aq_o, ak_o, av_o, lru_o):
    x = x_ref[0]
    h = _modulated_norm(x, mod_ref, nw_ref)
    z = _dot(h.astype(BF16), wp_ref[...])
    t = x.shape[0]
    lane = lax.broadcasted_iota(jnp.int32, (t, LANES), 1)
    first_h = (lane & (HEAD_DIM - 1)) < HEAD_DIM // 2
    first_m = (lane >= MLA_NOPE) & (lane < MLA_NOPE + MLA_ROPE // 2)
    cosm, sinm = cosm_ref[...], sinm_ref[...]
    cosh, sinh = cosh_ref[...], sinh_ref[...]
    gmat = gmat_ref[...]

    cq = _rms_lanes(z[:, 0:256], g_cq[...], 256.0)
    q = _dot(cq.astype(BF16), wuq_ref[...])
    ckv = _rms_lanes(z[:, 256:384], g_ckv[...], 128.0).astype(BF16)
    kn = _dot(ckv, wuk_ref[...])
    vv = _dot(ckv, wuv_ref[...])
    kr = z[:, 384:512]
    for hh in range(MLA_HEADS):
        sl = slice(hh * LANES, (hh + 1) * LANES)
        qs = _rms_lanes(q[:, sl], g_mq[...], float(MLA_QK))
        mq_o[0, hh] = _rope_slab(qs, cosm, sinm, first_m, MLA_ROPE // 2).astype(BF16)
        ks = _rms_lanes(kn[:, sl] + kr, g_mk[...], float(MLA_QK))
        mk_o[0, hh] = _rope_slab(ks, cosm, sinm, first_m, MLA_ROPE // 2).astype(BF16)
        mv_o[0, hh] = vv[:, hh * MLA_V:(hh + 1) * MLA_V].astype(BF16)

    def heads(zz, gain, g2, nheads, out):
        y = _rms_heads(zz, gain, g2)
        for s in range(nheads // 2):
            ys = _rope_slab(y[:, s * LANES:(s + 1) * LANES], cosh, sinh, first_h, HEAD_DIM // 2)
            out[0, 2 * s] = ys[:, :HEAD_DIM].astype(BF16)
            out[0, 2 * s + 1] = ys[:, HEAD_DIM:].astype(BF16)

    heads(z[:, 512:768], g_sq[...], gmat, SWA_Q_HEADS, sq_o)
    heads(z[:, 768:896], g_sk[...], gmat[:LANES, :LANES], SWA_KV_HEADS, sk_o)
    heads(z[:, 1024:1280], g_aq[...], gmat, AXA_Q_HEADS, aq_o)
    heads(z[:, 1280:1408], g_ak[...], gmat[:LANES, :LANES], AXA_KV_HEADS, ak_o)
    for hh in range(SWA_KV_HEADS):
        sv_o[0, hh] = z[:, 896 + hh * HEAD_DIM:896 + (hh + 1) * HEAD_DIM].astype(BF16)
        av_o[0, hh] = z[:, 1408 + hh * HEAD_DIM:1408 + (hh + 1) * HEAD_DIM].astype(BF16)
    lru_o[0] = z[:, 1536:1792]


def _project(xc, mod, lw, tabs, nb):
    b, s, d = xc.shape
    nt = s // TILE
    mod_rows = mod.shape[0]

    def tile_spec(width):
        return pl.BlockSpec((1, TILE, width), lambda t, i: (i, t, 0))

    def head_spec(nh, width):
        return pl.BlockSpec((1, nh, TILE, width), lambda t, i: (i, 0, t, 0))

    tab_spec = pl.BlockSpec((TILE, LANES), lambda t, i: (t, 0))
    consts = [lw["nw"], lw["wp"], lw["wuq"], lw["wuk"], lw["wuv"], lw["g_cq"], lw["g_ckv"],
              lw["g_mq"], lw["g_mk"], lw["g_sq"], lw["g_sk"], lw["g_aq"], lw["g_ak"], lw["gmat"]]
    in_specs = ([tile_spec(d),
                 pl.BlockSpec((1, 3, d), lambda t, i: (jnp.where(t == 0, mod_rows - 1, i), 0, 0))]
                + [_const_spec(c.shape) for c in consts] + [tab_spec] * 4)
    outs = [(MLA_HEADS, LANES), (MLA_HEADS, LANES), (MLA_HEADS, MLA_V),
            (SWA_Q_HEADS, HEAD_DIM), (SWA_KV_HEADS, HEAD_DIM), (SWA_KV_HEADS, HEAD_DIM),
            (AXA_Q_HEADS, HEAD_DIM), (AXA_KV_HEADS, HEAD_DIM), (AXA_KV_HEADS, HEAD_DIM)]
    out_specs = [head_spec(nh, w) for nh, w in outs] + [tile_spec(4 * HEAD_DIM)]
    out_shape = ([jax.ShapeDtypeStruct((b, nh, s, w), BF16) for nh, w in outs]
                 + [jax.ShapeDtypeStruct((b, s, 4 * HEAD_DIM), F32)])
    return pl.pallas_call(
        _proj_kernel,
        grid=(nt, b),
        in_specs=in_specs,
        out_specs=out_specs,
        out_shape=out_shape,
        compiler_params=_cparams(("arbitrary", "arbitrary")),
        name="project",
    )(xc, mod, *consts, *tabs)


def _dense_attn_kernel(q_ref, k_ref, v_ref, o_ref, *, n_heads, group, ctx_tile_first):
    s_len = k_ref.shape[2]
    n_kv = s_len // TILE
    if ctx_tile_first:
        n_steps = jnp.where(pl.program_id(1) == 0, 1, n_kv)
    else:
        n_steps = n_kv
    tq = q_ref.shape[2]
    dv = v_ref.shape[3]
    outs = []
    for hh in range(n_heads):
        q = q_ref[0, hh]
        kvh = hh // group

        def body(j, carry, q=q, kvh=kvh):
            m, l, acc = carry
            start = pl.multiple_of(j * TILE, TILE)
            k = k_ref[0, kvh, pl.ds(start, TILE), :]
            v = v_ref[0, kvh, pl.ds(start, TILE), :]
            s = _dot_nt(q, k)
            m_new = jnp.maximum(m, jnp.max(s, axis=-1, keepdims=True))
            alpha = jnp.exp(m - m_new)
            p = jnp.exp(s - m_new)
            l = alpha * l + jnp.sum(p, axis=-1, keepdims=True)
            acc = alpha * acc + _dot(p.astype(BF16), v)
            return m_new, l, acc

        init = (jnp.full((tq, 1), NEG_INF, F32), jnp.zeros((tq, 1), F32), jnp.zeros((tq, dv), F32))
        _, l, acc = lax.fori_loop(0, n_steps, body, init)
        outs.append(acc / l)
    o_ref[0] = jnp.concatenate(outs, axis=-1)


def _dense_attention(q, k, v, group, with_ctx_queries):
    b, nh, s, d = q.shape
    hkv, dv = v.shape[1], v.shape[3]
    nt = s // TILE
    off = 0 if with_ctx_queries else 1
    nq = nt - off
    kern = functools.partial(_dense_attn_kernel, n_heads=nh, group=group,
                             ctx_tile_first=with_ctx_queries)
    return pl.pallas_call(
        kern,
        grid=(b, nq),
        in_specs=[pl.BlockSpec((1, nh, TILE, d), lambda i, t: (i, 0, t + off, 0)),
                  pl.BlockSpec((1, hkv, s, d), lambda i, t: (i, 0, 0, 0)),
                  pl.BlockSpec((1, hkv, s, dv), lambda i, t: (i, 0, 0, 0))],
        out_specs=pl.BlockSpec((1, TILE, nh * dv), lambda i, t: (i, t, 0)),
        out_shape=jax.ShapeDtypeStruct((b, nq * TILE, nh * dv), F32),
        compiler_params=_cparams(("arbitrary", "arbitrary")),
        name="dense_attention",
    )(q, k, v)


def _window_attn_kernel(sink_ref, q_ref, k_ref, v_ref, o_ref, *, tile_off):
    s_len = k_ref.shape[2]
    n_tiles = s_len // TILE
    ti = pl.program_id(1) + tile_off
    is_lat = ti > 0
    t0 = ti * TILE
    half = TILE // 2
    left0 = pl.multiple_of(jnp.maximum(t0 - half, 0), half)
    cen0 = pl.multiple_of(t0, TILE)
    right0 = pl.multiple_of(jnp.minimum(t0 + TILE, s_len - half), half)
    r_c = lax.broadcasted_iota(jnp.int32, (TILE, TILE), 0)
    c_c = lax.broadcasted_iota(jnp.int32, (TILE, TILE), 1)
    r_s = lax.broadcasted_iota(jnp.int32, (TILE, half), 0)
    c_s = lax.broadcasted_iota(jnp.int32, (TILE, half), 1)
    ok_cen = (jnp.abs(r_c - c_c) <= WINDOW) & is_lat
    ok_left = ((r_s - c_s + half) <= WINDOW) & (ti > 1)
    ok_right = ((c_s - r_s + TILE) <= WINDOW) & is_lat & (ti < n_tiles - 1)
    outs = []
    for hh in range(SWA_Q_HEADS):
        kvh = hh // (SWA_Q_HEADS // SWA_KV_HEADS)
        q = q_ref[0, hh]
        sink = sink_ref[hh]

        def kv(start, size, kvh=kvh):
            return k_ref[0, kvh, pl.ds(start, size), :], v_ref[0, kvh, pl.ds(start, size), :]

        k_x, v_x = kv(0, TILE)
        k_l, v_l = kv(left0, half)
        k_c, v_c = kv(cen0, TILE)
        k_r, v_r = kv(right0, half)
        s_x = _dot_nt(q, k_x)
        s_l = jnp.where(ok_left, _dot_nt(q, k_l), NEG_INF)
        s_c = jnp.where(ok_cen, _dot_nt(q, k_c), NEG_INF)
        s_r = jnp.where(ok_right, _dot_nt(q, k_r), NEG_INF)
        m = jnp.maximum(jnp.maximum(jnp.max(s_x, -1, keepdims=True), jnp.max(s_c, -1, keepdims=True)),
                        jnp.maximum(jnp.max(s_l, -1, keepdims=True), jnp.max(s_r, -1, keepdims=True)))
        m = jnp.maximum(m, sink)
        p_x, p_l, p_c, p_r = (jnp.exp(s_x - m), jnp.exp(s_l - m), jnp.exp(s_c - m), jnp.exp(s_r - m))
        den = (jnp.sum(p_x, -1, keepdims=True) + jnp.sum(p_l, -1, keepdims=True)
               + jnp.sum(p_c, -1, keepdims=True) + jnp.sum(p_r, -1, keepdims=True) + jnp.exp(sink - m))
        num = (_dot(p_x.astype(BF16), v_x) + _dot(p_l.astype(BF16), v_l)
               + _dot(p_c.astype(BF16), v_c) + _dot(p_r.astype(BF16), v_r))
        outs.append(num / den)
    o_ref[0] = jnp.concatenate(outs, axis=-1)


def _window_attention(sink, q, k, v, with_ctx_queries):
    b, nh, s, d = q.shape
    hkv = k.shape[1]
    nt = s // TILE
    off = 0 if with_ctx_queries else 1
    nq = nt - off
    kern = functools.partial(_window_attn_kernel, tile_off=off)
    return pl.pallas_call(
        kern,
        grid=(b, nq),
        in_specs=[pl.BlockSpec(memory_space=pltpu.SMEM),
                  pl.BlockSpec((1, nh, TILE, d), lambda i, t: (i, 0, t + off, 0)),
                  pl.BlockSpec((1, hkv, s, d), lambda i, t: (i, 0, 0, 0)),
                  pl.BlockSpec((1, hkv, s, d), lambda i, t: (i, 0, 0, 0))],
        out_specs=pl.BlockSpec((1, TILE, nh * d), lambda i, t: (i, t, 0)),
        out_shape=jax.ShapeDtypeStruct((b, nq * TILE, nh * d), F32),
        compiler_params=_cparams(("arbitrary", "arbitrary")),
        name="window_attention",
    )(sink, q, k, v)


def _lru_conv(prev_ref, cur_ref, next_ref, prev_ok, next_ok, cw_ref, cb_ref):
    t = cur_ref.shape[1]
    prev = jnp.where(prev_ok, prev_ref[0], 0.0)
    nxt = jnp.where(next_ok, next_ref[0], 0.0)
    ext = jnp.concatenate([prev, cur_ref[0], nxt], axis=0)
    n = t + 2 * SUBLANES
    u = cb_ref[...] + cur_ref[0] * cw_ref[CONV_LEFT:CONV_LEFT + 1, :]
    for j in range(CONV_W):
        off = j - CONV_LEFT
        if off == 0:
            continue
        sh = pltpu.roll(ext, (-off) % n, 0)[SUBLANES:SUBLANES + t]
        u = u + sh * cw_ref[j:j + 1, :]
    return u


def _lru_scan_tile(u, d, wr_hi, wr_lo, wi_hi, wi_lo, br_ref, bi_ref, lam_ref, h0, reverse):
    t = u.shape[0]
    r = _sigmoid(_dot3(u, wr_hi[d], wr_lo[d]) + br_ref[d:d + 1, :])
    i = _sigmoid(_dot3(u, wi_hi[d], wi_lo[d]) + bi_ref[d:d + 1, :])
    lam = lam_ref[d:d + 1, :]
    softplus_neg = jnp.maximum(-lam, 0.0) + jnp.log1p(jnp.exp(-jnp.abs(lam)))
    log_a = -LRU_C * r * softplus_neg
    a = jnp.exp(log_a)
    bx = jnp.sqrt(1.0 - a * a) * (i * u)
    row = lax.broadcasted_iota(jnp.int32, u.shape, 0)
    step = 1
    while step < t:
        if reverse:
            ok = row < t - step
            a_sh = jnp.where(ok, pltpu.roll(a, t - step, 0), 1.0)
            b_sh = jnp.where(ok, pltpu.roll(bx, t - step, 0), 0.0)
        else:
            ok = row >= step
            a_sh = jnp.where(ok, pltpu.roll(a, step, 0), 1.0)
            b_sh = jnp.where(ok, pltpu.roll(bx, step, 0), 0.0)
        bx = a * b_sh + bx
        a = a * a_sh
        step *= 2
    h = a * h0 + bx
    h_end = h[0:1, :] if reverse else h[t - 1:t, :]
    return h, h_end


def _lru_kernel(fp_ref, fc_ref, fn_ref, rp_ref, rc_ref, rn_ref, cw_ref, cb_ref,
                wr_hi, wr_lo, wi_hi, wi_lo, br_ref, bi_ref, lam_ref,
                yf_ref, yb_ref, hf_s, hb_s):
    j = pl.program_id(1)
    nt = pl.num_programs(1)

    @pl.when(j == 0)
    def _():
        hf_s[...] = jnp.zeros_like(hf_s)
        hb_s[...] = jnp.zeros_like(hb_s)

    def bounds(ti):
        return ti >= 2, (ti >= 1) & (ti < nt - 1)

    tf = j
    tr = jnp.where(j == 0, 0, nt - j)
    p_ok, n_ok = bounds(tf)
    uf = _lru_conv(fp_ref, fc_ref, fn_ref, p_ok, n_ok, cw_ref, cb_ref)
    h, h_end = _lru_scan_tile(uf, 0, wr_hi, wr_lo, wi_hi, wi_lo, br_ref, bi_ref, lam_ref,
                              hf_s[...], False)
    yf_ref[0] = h
    hf_s[...] = h_end
    p_ok, n_ok = bounds(tr)
    ur = _lru_conv(rp_ref, rc_ref, rn_ref, p_ok, n_ok, cw_ref, cb_ref)
    h, h_end = _lru_scan_tile(ur, 1, wr_hi, wr_lo, wi_hi, wi_lo, br_ref, bi_ref, lam_ref,
                              hb_s[...], True)
    yb_ref[0] = h
    hb_s[...] = h_end


def _rglru(z, lw):
    b, s, c = z.shape
    nt = s // TILE
    r8 = TILE // SUBLANES
    n8 = s // SUBLANES

    def rev(j):
        return jnp.where(j == 0, 0, nt - j)

    def cur(f):
        return pl.BlockSpec((1, TILE, c), lambda i, j: (i, f(j), 0))

    def prev(f):
        return pl.BlockSpec((1, SUBLANES, c), lambda i, j: (i, jnp.maximum(f(j) * r8 - 1, 0), 0))

    def nxt(f):
        return pl.BlockSpec((1, SUBLANES, c), lambda i, j: (i, jnp.minimum((f(j) + 1) * r8, n8 - 1), 0))

    ident = lambda j: j
    consts = [lw["conv_w"], lw["conv_b"], lw["wr_hi"], lw["wr_lo"], lw["wi_hi"], lw["wi_lo"],
              lw["b_r"], lw["b_i"], lw["lam"]]
    return pl.pallas_call(
        _lru_kernel,
        grid=(b, nt),
        in_specs=[prev(ident), cur(ident), nxt(ident), prev(rev), cur(rev), nxt(rev)]
                 + [_const_spec(x.shape) for x in consts],
        out_specs=[cur(ident), cur(rev)],
        out_shape=[jax.ShapeDtypeStruct((b, s, c), F32)] * 2,
        scratch_shapes=[pltpu.VMEM((1, c), F32), pltpu.VMEM((1, c), F32)],
        compiler_params=_cparams(("arbitrary", "arbitrary")),
        name="rglru",
    )(z, z, z, z, z, z, *consts)


def _merge_kernel(x_ref, mod_ref, nw_ref, wg_ref, wm_ref, ya_ref, yb_ref, yc_ref, yf_ref, yr_ref,
                  wb_ref, wo_ref, o_ref):
    x = x_ref[0]
    hb = _modulated_norm(x, mod_ref, nw_ref).astype(BF16)
    zg = _dot(hb, wg_ref[...])
    d = x.shape[1]
    bw = ya_ref.shape[2]
    branches = (ya_ref[0], yb_ref[0], yc_ref[0], yf_ref[0] + yr_ref[0])
    mix = jnp.zeros_like(x)
    for k in range(N_BRANCH):
        g = zg[:, k * bw:(k + 1) * bw]
        y = branches[k] * (g * _sigmoid(g))
        proj = _dot(y.astype(BF16), wb_ref[k])
        zm = _dot(hb, wm_ref[:, k * d:(k + 1) * d])
        mix = mix + _sigmoid(zm) * proj
    o_ref[0] = x + mod_ref[0, 2:3, :] * _dot(mix.astype(BF16), wo_ref[...])


def _merge(xc, mod, lw, ya, yb, yc, yf, yr, with_ctx):
    b, s, d = xc.shape
    nt = s // TILE
    off = 0 if with_ctx else 1
    nq = nt - off
    mod_rows = mod.shape[0]
    bw = ya.shape[2]

    def stream(width):
        return pl.BlockSpec((1, TILE, width), lambda i, t: (i, t + off, 0))

    def local(width):
        return pl.BlockSpec((1, TILE, width), lambda i, t: (i, t, 0))

    consts_a = [lw["nw"], lw["wg"], lw["wm"]]
    consts_b = [lw["wb"], lw["wo"]]
    return pl.pallas_call(
        _merge_kernel,
        grid=(b, nq),
        in_specs=[stream(d),
                  pl.BlockSpec((1, 3, d), lambda i, t: (jnp.where(t + off == 0, mod_rows - 1, i), 0, 0))]
                 + [_const_spec(c.shape) for c in consts_a]
                 + [local(bw), local(bw), local(bw), stream(bw), stream(bw)]
                 + [_const_spec(c.shape) for c in consts_b],
        out_specs=local(d),
        out_shape=jax.ShapeDtypeStruct((b, nq * TILE, d), F32),
        compiler_params=_cparams(("arbitrary", "arbitrary")),
        name="merge",
    )(xc, mod, *consts_a, ya, yb, yc, yf, yr, *consts_b)


def _rope_tables(n, n_ctx):
    def axial(rot_dim):
        n_rows = n // GRID_W
        rows = jnp.repeat(jnp.arange(n_rows, dtype=F32), GRID_W)
        cols = jnp.tile(jnp.arange(GRID_W, dtype=F32), n_rows)
        quarter = rot_dim // 4
        freqs = ROPE_THETA ** (-jnp.arange(quarter, dtype=F32) / quarter)
        ang = jnp.concatenate([rows[:, None] * freqs, cols[:, None] * freqs], axis=-1)
        return jnp.cos(ang), jnp.sin(ang)

    def with_ctx(cos, sin):
        return (jnp.concatenate([jnp.ones((n_ctx, LANES), F32), cos], axis=0),
                jnp.concatenate([jnp.zeros((n_ctx, LANES), F32), sin], axis=0))

    cm, sm = axial(MLA_ROPE)
    one = jnp.ones((n, MLA_NOPE), F32)
    zero = jnp.zeros((n, MLA_NOPE), F32)
    pad1 = jnp.ones((n, LANES - MLA_QK), F32)
    pad0 = jnp.zeros((n, LANES - MLA_QK), F32)
    cosm, sinm = with_ctx(jnp.concatenate([one, cm, cm, pad1], axis=-1),
                          jnp.concatenate([zero, sm, sm, pad0], axis=-1))
    ch, sh = axial(HEAD_DIM)
    cosh, sinh = with_ctx(jnp.tile(ch, (1, 4)), jnp.tile(sh, (1, 4)))
    return cosm, sinm, cosh, sinh


def _block_diag(w):
    k, m, _ = w.shape
    out = jnp.zeros((k * m, k * m), w.dtype)
    for i in range(k):
        out = out.at[i * m:(i + 1) * m, i * m:(i + 1) * m].set(w[i])
    return out


def _layer_weights(l, p):
    d = p["w_in"].shape[1]
    w_in = p["w_in"][l]
    sizes = (256, 128, 32, 256, 128, 128, 256, 128, 128, 256, 1024, 4 * d)
    offs = [0]
    for sz in sizes:
        offs.append(offs[-1] + sz)
    cols = [w_in[:, offs[i]:offs[i + 1]] for i in range(len(sizes))]
    kr_slab = jnp.pad(cols[2], ((0, 0), (MLA_NOPE, LANES - MLA_QK)))
    wp = jnp.concatenate([cols[0], cols[1], kr_slab] + cols[3:10], axis=1).astype(BF16)

    def pad_heads(w, nh, width, real_from, real):
        k = w.shape[0]
        w = w.reshape(k, nh, width)[:, :, real_from:real_from + real]
        return jnp.pad(w, ((0, 0), (0, 0), (0, LANES - real))).reshape(k, nh * LANES)

    w_uq = p["mla_w_uq"][l]
    w_ukv = p["mla_w_ukv"][l]
    wuq = pad_heads(w_uq, MLA_HEADS, MLA_QK, 0, MLA_QK).astype(BF16)
    wuk = pad_heads(w_ukv, MLA_HEADS, MLA_NOPE + MLA_V, 0, MLA_NOPE).astype(BF16)
    wuv = w_ukv.reshape(-1, MLA_HEADS, MLA_NOPE + MLA_V)[:, :, MLA_NOPE:].reshape(-1, MLA_HEADS * MLA_V)
    wuv = wuv.astype(BF16)

    def slab_gain(g, scale):
        return (jnp.pad(g, (0, LANES - MLA_QK)) * scale).reshape(1, LANES)

    def head_gain(g, nh, scale):
        return (jnp.tile(g, nh) * scale).reshape(1, nh * HEAD_DIM)

    idx = jnp.arange(4 * HEAD_DIM) // HEAD_DIM
    gmat = (idx[:, None] == idx[None, :]).astype(BF16)

    def hi_lo(w):
        hi = w.astype(BF16)
        return hi, (w - hi.astype(F32)).astype(BF16)

    wr = jnp.stack([_block_diag(p["lru_w_r"][l, dd]) for dd in range(2)])
    wi = jnp.stack([_block_diag(p["lru_w_i"][l, dd]) for dd in range(2)])
    wr_hi, wr_lo = hi_lo(wr)
    wi_hi, wi_lo = hi_lo(wi)
    return {
        "nw": p["norm_w"][l].reshape(1, d),
        "wp": wp, "wuq": wuq, "wuk": wuk, "wuv": wuv,
        "g_cq": p["mla_cq_norm"][l].reshape(1, -1),
        "g_ckv": p["mla_ckv_norm"][l].reshape(1, -1),
        "g_mq": slab_gain(p["mla_q_norm"][l], MLA_QK ** -0.5),
        "g_mk": slab_gain(p["mla_k_norm"][l], 1.0),
        "g_sq": head_gain(p["swa_q_norm"][l], SWA_Q_HEADS, HEAD_DIM ** -0.5),
        "g_sk": head_gain(p["swa_k_norm"][l], SWA_KV_HEADS, 1.0),
        "g_aq": head_gain(p["axa_q_norm"][l], AXA_Q_HEADS, HEAD_DIM ** -0.5),
        "g_ak": head_gain(p["axa_k_norm"][l], AXA_KV_HEADS, 1.0),
        "gmat": gmat,
        "wg": cols[10].astype(BF16), "wm": cols[11].astype(BF16),
        "wb": p["w_branch"][l].astype(BF16), "wo": p["w_out"][l].astype(BF16),
        "conv_w": p["lru_conv_w"][l], "conv_b": p["lru_conv_b"][l].reshape(1, -1),
        "wr_hi": wr_hi, "wr_lo": wr_lo, "wi_hi": wi_hi, "wi_lo": wi_lo,
        "b_r": p["lru_b_r"][l], "b_i": p["lru_b_i"][l], "lam": p["lru_lambda"][l],
        "sink": p["swa_sink"][l],
    }


def kernel(x, c, ctx, c_ctx, w_mod, b_mod, norm_w, w_in, mla_cq_norm, mla_ckv_norm, mla_w_uq, mla_w_ukv, mla_q_norm, mla_k_norm, swa_q_norm, swa_k_norm, swa_sink, axa_q_norm, axa_k_norm, lru_conv_w, lru_conv_b, lru_w_r, lru_b_r, lru_w_i, lru_b_i, lru_lambda, w_branch, w_out):
    p = dict(norm_w=norm_w, w_in=w_in, mla_cq_norm=mla_cq_norm, mla_ckv_norm=mla_ckv_norm,
             mla_w_uq=mla_w_uq, mla_w_ukv=mla_w_ukv, mla_q_norm=mla_q_norm, mla_k_norm=mla_k_norm,
             swa_q_norm=swa_q_norm, swa_k_norm=swa_k_norm, swa_sink=swa_sink,
             axa_q_norm=axa_q_norm, axa_k_norm=axa_k_norm, lru_conv_w=lru_conv_w,
             lru_conv_b=lru_conv_b, lru_w_r=lru_w_r, lru_b_r=lru_b_r, lru_w_i=lru_w_i,
             lru_b_i=lru_b_i, lru_lambda=lru_lambda, w_branch=w_branch, w_out=w_out)
    b, n, d = x.shape
    n_ctx = ctx.shape[1]
    depth = w_mod.shape[0]
    assert n_ctx == TILE and n % TILE == 0 and n % GRID_W == 0

    rows = jnp.concatenate([c, c_ctx[None, :]], axis=0)
    n_rows = -(-rows.shape[0] // SUBLANES) * SUBLANES
    rows = jnp.pad(rows, ((0, n_rows - rows.shape[0]), (0, 0)))
    mod_all = _modulation(rows, w_mod, b_mod)[:, :b + 1].reshape(depth, b + 1, 3, d)

    tabs = _rope_tables(n, n_ctx)
    xc = jnp.concatenate([ctx, x], axis=1)
    for l in range(depth):
        lw = _layer_weights(l, p)
        mod = mod_all[l]
        upd = l < depth - 1
        mq, mk, mv, sq, sk, sv, aq, ak, av, zl = _project(xc, mod, lw, tabs, b)
        ya = _dense_attention(mq, mk, mv, 1, upd)
        yb = _window_attention(lw["sink"], sq, sk, sv, upd)
        yc = _dense_attention(aq, ak, av, AXA_Q_HEADS // AXA_KV_HEADS, upd)
        yf, yr = _rglru(zl, lw)
        xc = _merge(xc, mod, lw, ya, yb, yc, yf, yr, upd)
    return xc
```

```python
import functools

import jax
import jax.numpy as jnp
from jax import lax
from jax.experimental import pallas as pl
from jax.experimental.pallas import tpu as pltpu

GRID_W = 64
N_BRANCH = 4
HEAD_DIM = 64
MLA_HEADS = 4
MLA_NOPE = 64
MLA_ROPE = 32
MLA_V = 64
MLA_QK = MLA_NOPE + MLA_ROPE
SWA_Q_HEADS = 4
SWA_KV_HEADS = 2
WINDOW = 128
AXA_Q_HEADS = 4
AXA_KV_HEADS = 2
LRU_BLOCKS = 4
LRU_C = 8.0
CONV_W = 4
CONV_LEFT = 2
ROPE_THETA = 10000.0
RMS_EPS = 1e-6
NEG_INF = -1e30
LOG2E = 1.4426950408889634

LANES = 128
SUBLANES = 8
TILE = 256
VMEM_LIMIT = 56 * 1024 * 1024

F32 = jnp.float32
BF16 = jnp.bfloat16


def _dot(a, b):
    return jnp.dot(a, b, preferred_element_type=F32)


def _dot_nt(a, b):
    return lax.dot_general(a, b, (((1,), (1,)), ((), ())), preferred_element_type=F32)


def _split(a):
    hi = a.astype(BF16)
    lo = (a - hi.astype(F32)).astype(BF16)
    return hi, lo


def _dot3(a, w_hi, w_lo):
    a_hi, a_lo = _split(a)
    return _dot(a_hi, w_hi) + _dot(a_lo, w_hi) + _dot(a_hi, w_lo)


def _sigmoid(v):
    return 1.0 / (1.0 + jnp.exp(-v))


def _cparams(sem):
    return pltpu.CompilerParams(dimension_semantics=sem, vmem_limit_bytes=VMEM_LIMIT)


def _const_spec(shape):
    nd = len(shape)
    return pl.BlockSpec(shape, lambda *_: (0,) * nd)


def _mod_kernel(c_ref, w_ref, b_ref, o_ref):
    v = c_ref[...]
    a = v * _sigmoid(v)
    w_hi, w_lo = _split(w_ref[0])
    o_ref[0] = _dot3(a, w_hi, w_lo) + b_ref[0]


def _modulation(cvec, w_mod, b_mod):
    depth, d, d3 = w_mod.shape
    r = cvec.shape[0]
    bn = 512
    return pl.pallas_call(
        _mod_kernel,
        grid=(depth, d3 // bn),
        in_specs=[pl.BlockSpec((r, d), lambda l, j: (0, 0)),
                  pl.BlockSpec((1, d, bn), lambda l, j: (l, 0, j)),
                  pl.BlockSpec((1, 1, bn), lambda l, j: (l, 0, j))],
        out_specs=pl.BlockSpec((1, r, bn), lambda l, j: (l, 0, j)),
        out_shape=jax.ShapeDtypeStruct((depth, r, d3), F32),
        compiler_params=_cparams(("arbitrary", "arbitrary")),
        name="modulation",
    )(cvec, w_mod, b_mod.reshape(depth, 1, d3))


def _modulated_norm(x, mod_ref, nw_ref):
    ms = jnp.mean(x * x, axis=-1, keepdims=True)
    y = x * lax.rsqrt(ms + RMS_EPS) * nw_ref[...]
    return y * (1.0 + mod_ref[0, 1:2, :]) + mod_ref[0, 0:1, :]


def _rms_lanes(v, gain, n):
    ms = jnp.sum(v * v, axis=-1, keepdims=True) * (1.0 / n)
    return v * lax.rsqrt(ms + RMS_EPS) * gain


def _rms_heads(v, gain, gmat):
    hi, lo = _split(v * v)
    ss = _dot(hi, gmat) + _dot(lo, gmat)
    return v * lax.rsqrt(ss * (1.0 / HEAD_DIM) + RMS_EPS) * gain


def _rope_slab(xs, cos, sin, first, shift):
    rot = jnp.where(first, -pltpu.roll(xs, LANES - shift, 1), pltpu.roll(xs, shift, 1))
    return xs * cos + rot * sin


def _proj_kernel(x_ref, mod_ref, nw_ref, wp_ref, wuq_ref, wuk_ref, wuv_ref,
                 g_cq, g_ckv, g_mq, g_mk, g_sq, g_sk, g_aq, g_ak, gmat_ref,
                 cosm_ref, sinm_ref, cosh_ref, sinh_ref,
                 mq_o, mk_o, mv_o, sq_o, sk_o, sv_o, aq_o, ak_o, av_o, lru_o):
    x = x_ref[0]
    h = _modulated_norm(x, mod_ref, nw_ref)
    z = _dot(h.astype(BF16), wp_ref[...])
    t = x.shape[0]
    lane = lax.broadcasted_iota(jnp.int32, (t, LANES), 1)
    first_h = (lane & (HEAD_DIM - 1)) < HEAD_DIM // 2
    first_m = (lane >= MLA_NOPE) & (lane < MLA_NOPE + MLA_ROPE // 2)
    cosm, sinm = cosm_ref[...], sinm_ref[...]
    cosh, sinh = cosh_ref[...], sinh_ref[...]
    gmat = gmat_ref[...]

    cq = _rms_lanes(z[:, 0:256], g_cq[...], 256.0)
    q = _dot(cq.astype(BF16), wuq_ref[...])
    ckv = _rms_lanes(z[:, 256:384], g_ckv[...], 128.0).astype(BF16)
    kn = _dot(ckv, wuk_ref[...])
    vv = _dot(ckv, wuv_ref[...])
    kr = z[:, 384:512]
    for hh in range(MLA_HEADS):
        sl = slice(hh * LANES, (hh + 1) * LANES)
        qs = _rms_lanes(q[:, sl], g_mq[...], float(MLA_QK))
        mq_o[0, hh] = _rope_slab(qs, cosm, sinm, first_m, MLA_ROPE // 2).astype(BF16)
        ks = _rms_lanes(kn[:, sl] + kr, g_mk[...], float(MLA_QK))
        mk_o[0, hh] = _rope_slab(ks, cosm, sinm, first_m, MLA_ROPE // 2).astype(BF16)
        mv_o[0, hh] = vv[:, hh * MLA_V:(hh + 1) * MLA_V].astype(BF16)

    def heads(zz, gain, g2, nheads, out):
        y = _rms_heads(zz, gain, g2)
        for s in range(nheads // 2):
            ys = _rope_slab(y[:, s * LANES:(s + 1) * LANES], cosh, sinh, first_h, HEAD_DIM // 2)
            out[0, 2 * s] = ys[:, :HEAD_DIM].astype(BF16)
            out[0, 2 * s + 1] = ys[:, HEAD_DIM:].astype(BF16)

    heads(z[:, 512:768], g_sq[...], gmat, SWA_Q_HEADS, sq_o)
    heads(z[:, 768:896], g_sk[...], gmat[:LANES, :LANES], SWA_KV_HEADS, sk_o)
    heads(z[:, 1024:1280], g_aq[...], gmat, AXA_Q_HEADS, aq_o)
    heads(z[:, 1280:1408], g_ak[...], gmat[:LANES, :LANES], AXA_KV_HEADS, ak_o)
    for hh in range(SWA_KV_HEADS):
        sv_o[0, hh] = z[:, 896 + hh * HEAD_DIM:896 + (hh + 1) * HEAD_DIM].astype(BF16)
        av_o[0, hh] = z[:, 1408 + hh * HEAD_DIM:1408 + (hh + 1) * HEAD_DIM].astype(BF16)
    lru_o[0] = z[:, 1536:1792]


def _project(xc, mod, lw, tabs, nb):
    b, s, d = xc.shape
    nt = s // TILE
    mod_rows = mod.shape[0]

    def tile_spec(width):
        return pl.BlockSpec((1, TILE, width), lambda t, i: (i, t, 0))

    def head_spec(nh, width):
        return pl.BlockSpec((1, nh, TILE, width), lambda t, i: (i, 0, t, 0))

    tab_spec = pl.BlockSpec((TILE, LANES), lambda t, i: (t, 0))
    consts = [lw["nw"], lw["wp"], lw["wuq"], lw["wuk"], lw["wuv"], lw["g_cq"], lw["g_ckv"],
              lw["g_mq"], lw["g_mk"], lw["g_sq"], lw["g_sk"], lw["g_aq"], lw["g_ak"], lw["gmat"]]
    in_specs = ([tile_spec(d),
                 pl.BlockSpec((1, 3, d), lambda t, i: (jnp.where(t == 0, mod_rows - 1, i), 0, 0))]
                + [_const_spec(c.shape) for c in consts] + [tab_spec] * 4)
    outs = [(MLA_HEADS, LANES), (MLA_HEADS, LANES), (MLA_HEADS, MLA_V),
            (SWA_Q_HEADS, HEAD_DIM), (SWA_KV_HEADS, HEAD_DIM), (SWA_KV_HEADS, HEAD_DIM),
            (AXA_Q_HEADS, HEAD_DIM), (AXA_KV_HEADS, HEAD_DIM), (AXA_KV_HEADS, HEAD_DIM)]
    out_specs = [head_spec(nh, w) for nh, w in outs] + [tile_spec(4 * HEAD_DIM)]
    out_shape = ([jax.ShapeDtypeStruct((b, nh, s, w), BF16) for nh, w in outs]
                 + [jax.ShapeDtypeStruct((b, s, 4 * HEAD_DIM), F32)])
    return pl.pallas_call(
        _proj_kernel,
        grid=(nt, b),
        in_specs=in_specs,
        out_specs=out_specs,
        out_shape=out_shape,
        compiler_params=_cparams(("arbitrary", "arbitrary")),
        name="project",
    )(xc, mod, *consts, *tabs)


def _dense_attn_kernel(q_ref, k_ref, v_ref, o_ref, *, n_heads, group, ctx_tile_first):
    s_len = k_ref.shape[2]
    tq = q_ref.shape[2]

    def attend(n_keys):
        outs = []
        for hh in range(n_heads):
            kvh = hh // group
            k = k_ref[0, kvh, 0:n_keys, :]
            v = v_ref[0, kvh, 0:n_keys, :]
            s = _dot_nt(q_ref[0, hh], k)
            m = jnp.max(s, axis=-1, keepdims=True)
            p = jnp.exp2(s - m)
            l = jnp.sum(p, axis=-1, keepdims=True)
            outs.append(_dot(p.astype(BF16), v) / l)
        o_ref[0] = jnp.concatenate(outs, axis=-1)

    if ctx_tile_first:
        pl.when(pl.program_id(1) == 0)(lambda: attend(TILE))
        pl.when(pl.program_id(1) > 0)(lambda: attend(s_len))
    else:
        attend(s_len)


def _dense_attention(q, k, v, group, with_ctx_queries):
    b, nh, s, d = q.shape
    hkv, dv = v.shape[1], v.shape[3]
    nt = s // TILE
    off = 0 if with_ctx_queries else 1
    nq = nt - off
    kern = functools.partial(_dense_attn_kernel, n_heads=nh, group=group,
                             ctx_tile_first=with_ctx_queries)
    return pl.pallas_call(
        kern,
        grid=(b, nq),
        in_specs=[pl.BlockSpec((1, nh, TILE, d), lambda i, t: (i, 0, t + off, 0)),
                  pl.BlockSpec((1, hkv, s, d), lambda i, t: (i, 0, 0, 0)),
                  pl.BlockSpec((1, hkv, s, dv), lambda i, t: (i, 0, 0, 0))],
        out_specs=pl.BlockSpec((1, TILE, nh * dv), lambda i, t: (i, t, 0)),
        out_shape=jax.ShapeDtypeStruct((b, nq * TILE, nh * dv), F32),
        compiler_params=_cparams(("arbitrary", "arbitrary")),
        name="dense_attention",
    )(q, k, v)


def _window_attn_kernel(sink_ref, q_ref, k_ref, v_ref, o_ref, *, tile_off):
    s_len = k_ref.shape[2]
    n_tiles = s_len // TILE
    ti = pl.program_id(1) + tile_off
    is_lat = ti > 0
    t0 = ti * TILE
    half = TILE // 2
    left0 = pl.multiple_of(jnp.maximum(t0 - half, 0), half)
    cen0 = pl.multiple_of(t0, TILE)
    right0 = pl.multiple_of(jnp.minimum(t0 + TILE, s_len - half), half)
    r_c = lax.broadcasted_iota(jnp.int32, (TILE, TILE), 0)
    c_c = lax.broadcasted_iota(jnp.int32, (TILE, TILE), 1)
    r_s = lax.broadcasted_iota(jnp.int32, (TILE, half), 0)
    c_s = lax.broadcasted_iota(jnp.int32, (TILE, half), 1)
    ok_cen = (jnp.abs(r_c - c_c) <= WINDOW) & is_lat
    ok_left = ((r_s - c_s + half) <= WINDOW) & (ti > 1)
    ok_right = ((c_s - r_s + TILE) <= WINDOW) & is_lat & (ti < n_tiles - 1)
    outs = []
    for hh in range(SWA_Q_HEADS):
        kvh = hh // (SWA_Q_HEADS // SWA_KV_HEADS)
        q = q_ref[0, hh]
        sink = sink_ref[hh] * LOG2E

        def kv(start, size, kvh=kvh):
            return k_ref[0, kvh, pl.ds(start, size), :], v_ref[0, kvh, pl.ds(start, size), :]

        k_x, v_x = kv(0, TILE)
        k_l, v_l = kv(left0, half)
        k_c, v_c = kv(cen0, TILE)
        k_r, v_r = kv(right0, half)
        s_x = _dot_nt(q, k_x)
        s_l = jnp.where(ok_left, _dot_nt(q, k_l), NEG_INF)
        s_c = jnp.where(ok_cen, _dot_nt(q, k_c), NEG_INF)
        s_r = jnp.where(ok_right, _dot_nt(q, k_r), NEG_INF)
        m = jnp.maximum(jnp.maximum(jnp.max(s_x, -1, keepdims=True), jnp.max(s_c, -1, keepdims=True)),
                        jnp.maximum(jnp.max(s_l, -1, keepdims=True), jnp.max(s_r, -1, keepdims=True)))
        m = jnp.maximum(m, sink)
        p_x, p_l, p_c, p_r = (jnp.exp2(s_x - m), jnp.exp2(s_l - m), jnp.exp2(s_c - m), jnp.exp2(s_r - m))
        den = (jnp.sum(p_x, -1, keepdims=True) + jnp.sum(p_l, -1, keepdims=True)
               + jnp.sum(p_c, -1, keepdims=True) + jnp.sum(p_r, -1, keepdims=True) + jnp.exp2(sink - m))
        num = (_dot(p_x.astype(BF16), v_x) + _dot(p_l.astype(BF16), v_l)
               + _dot(p_c.astype(BF16), v_c) + _dot(p_r.astype(BF16), v_r))
        outs.append(num / den)
    o_ref[0] = jnp.concatenate(outs, axis=-1)


def _window_attention(sink, q, k, v, with_ctx_queries):
    b, nh, s, d = q.shape
    hkv = k.shape[1]
    nt = s // TILE
    off = 0 if with_ctx_queries else 1
    nq = nt - off
    kern = functools.partial(_window_attn_kernel, tile_off=off)
    return pl.pallas_call(
        kern,
        grid=(b, nq),
        in_specs=[pl.BlockSpec(memory_space=pltpu.SMEM),
                  pl.BlockSpec((1, nh, TILE, d), lambda i, t: (i, 0, t + off, 0)),
                  pl.BlockSpec((1, hkv, s, d), lambda i, t: (i, 0, 0, 0)),
                  pl.BlockSpec((1, hkv, s, d), lambda i, t: (i, 0, 0, 0))],
        out_specs=pl.BlockSpec((1, TILE, nh * d), lambda i, t: (i, t, 0)),
        out_shape=jax.ShapeDtypeStruct((b, nq * TILE, nh * d), F32),
        compiler_params=_cparams(("arbitrary", "arbitrary")),
        name="window_attention",
    )(sink, q, k, v)


def _lru_conv(prev_ref, cur_ref, next_ref, prev_ok, next_ok, cw_ref, cb_ref):
    t = cur_ref.shape[1]
    prev = jnp.where(prev_ok, prev_ref[0], 0.0)
    nxt = jnp.where(next_ok, next_ref[0], 0.0)
    ext = jnp.concatenate([prev, cur_ref[0], nxt], axis=0)
    n = t + 2 * SUBLANES
    u = cb_ref[...] + cur_ref[0] * cw_ref[CONV_LEFT:CONV_LEFT + 1, :]
    for j in range(CONV_W):
        off = j - CONV_LEFT
        if off == 0:
            continue
        sh = pltpu.roll(ext, (-off) % n, 0)[SUBLANES:SUBLANES + t]
        u = u + sh * cw_ref[j:j + 1, :]
    return u


def _lru_scan_tile(u, d, wr_hi, wr_lo, wi_hi, wi_lo, br_ref, bi_ref, lam_ref, h0, reverse):
    t = u.shape[0]
    r = _sigmoid(_dot3(u, wr_hi[d], wr_lo[d]) + br_ref[d:d + 1, :])
    i = _sigmoid(_dot3(u, wi_hi[d], wi_lo[d]) + bi_ref[d:d + 1, :])
    lam = lam_ref[d:d + 1, :]
    softplus_neg = jnp.maximum(-lam, 0.0) + jnp.log1p(jnp.exp(-jnp.abs(lam)))
    log_a = -LRU_C * r * softplus_neg
    a = jnp.exp(log_a)
    bx = jnp.sqrt(1.0 - a * a) * (i * u)
    row = lax.broadcasted_iota(jnp.int32, u.shape, 0)
    step = 1
    while step < t:
        if reverse:
            ok = row < t - step
            a_sh = jnp.where(ok, pltpu.roll(a, t - step, 0), 1.0)
            b_sh = jnp.where(ok, pltpu.roll(bx, t - step, 0), 0.0)
        else:
            ok = row >= step
            a_sh = jnp.where(ok, pltpu.roll(a, step, 0), 1.0)
            b_sh = jnp.where(ok, pltpu.roll(bx, step, 0), 0.0)
        bx = a * b_sh + bx
        a = a * a_sh
        step *= 2
    h = a * h0 + bx
    h_end = h[0:1, :] if reverse else h[t - 1:t, :]
    return h, h_end


def _lru_kernel(fp_ref, fc_ref, fn_ref, rp_ref, rc_ref, rn_ref, cw_ref, cb_ref,
                wr_hi, wr_lo, wi_hi, wi_lo, br_ref, bi_ref, lam_ref,
                yf_ref, yb_ref, hf_s, hb_s):
    j = pl.program_id(1)
    nt = pl.num_programs(1)

    @pl.when(j == 0)
    def _():
        hf_s[...] = jnp.zeros_like(hf_s)
        hb_s[...] = jnp.zeros_like(hb_s)

    def bounds(ti):
        return ti >= 2, (ti >= 1) & (ti < nt - 1)

    tf = j
    tr = jnp.where(j == 0, 0, nt - j)
    p_ok, n_ok = bounds(tf)
    uf = _lru_conv(fp_ref, fc_ref, fn_ref, p_ok, n_ok, cw_ref, cb_ref)
    h, h_end = _lru_scan_tile(uf, 0, wr_hi, wr_lo, wi_hi, wi_lo, br_ref, bi_ref, lam_ref,
                              hf_s[...], False)
    yf_ref[0] = h
    hf_s[...] = h_end
    p_ok, n_ok = bounds(tr)
    ur = _lru_conv(rp_ref, rc_ref, rn_ref, p_ok, n_ok, cw_ref, cb_ref)
    h, h_end = _lru_scan_tile(ur, 1, wr_hi, wr_lo, wi_hi, wi_lo, br_ref, bi_ref, lam_ref,
                              hb_s[...], True)
    yb_ref[0] = h
    hb_s[...] = h_end


def _rglru(z, lw):
    b, s, c = z.shape
    nt = s // TILE
    r8 = TILE // SUBLANES
    n8 = s // SUBLANES

    def rev(j):
        return jnp.where(j == 0, 0, nt - j)

    def cur(f):
        return pl.BlockSpec((1, TILE, c), lambda i, j: (i, f(j), 0))

    def prev(f):
        return pl.BlockSpec((1, SUBLANES, c), lambda i, j: (i, jnp.maximum(f(j) * r8 - 1, 0), 0))

    def nxt(f):
        return pl.BlockSpec((1, SUBLANES, c), lambda i, j: (i, jnp.minimum((f(j) + 1) * r8, n8 - 1), 0))

    ident = lambda j: j
    consts = [lw["conv_w"], lw["conv_b"], lw["wr_hi"], lw["wr_lo"], lw["wi_hi"], lw["wi_lo"],
              lw["b_r"], lw["b_i"], lw["lam"]]
    return pl.pallas_call(
        _lru_kernel,
        grid=(b, nt),
        in_specs=[prev(ident), cur(ident), nxt(ident), prev(rev), cur(rev), nxt(rev)]
                 + [_const_spec(x.shape) for x in consts],
        out_specs=[cur(ident), cur(rev)],
        out_shape=[jax.ShapeDtypeStruct((b, s, c), F32)] * 2,
        scratch_shapes=[pltpu.VMEM((1, c), F32), pltpu.VMEM((1, c), F32)],
        compiler_params=_cparams(("arbitrary", "arbitrary")),
        name="rglru",
    )(z, z, z, z, z, z, *consts)


def _merge_kernel(x_ref, mod_ref, nw_ref, wg_ref, wm_ref, ya_ref, yb_ref, yc_ref, yf_ref, yr_ref,
                  wb_ref, wo_ref, o_ref):
    x = x_ref[0]
    hb = _modulated_norm(x, mod_ref, nw_ref).astype(BF16)
    zg = _dot(hb, wg_ref[...])
    d = x.shape[1]
    bw = ya_ref.shape[2]
    branches = (ya_ref[0], yb_ref[0], yc_ref[0], yf_ref[0] + yr_ref[0])
    mix = jnp.zeros_like(x)
    for k in range(N_BRANCH):
        g = zg[:, k * bw:(k + 1) * bw]
        y = branches[k] * (g * _sigmoid(g))
        proj = _dot(y.astype(BF16), wb_ref[k])
        zm = _dot(hb, wm_ref[:, k * d:(k + 1) * d])
        mix = mix + _sigmoid(zm) * proj
    o_ref[0] = x + mod_ref[0, 2:3, :] * _dot(mix.astype(BF16), wo_ref[...])


def _merge(xc, mod, lw, ya, yb, yc, yf, yr, with_ctx):
    b, s, d = xc.shape
    nt = s // TILE
    off = 0 if with_ctx else 1
    nq = nt - off
    mod_rows = mod.shape[0]
    bw = ya.shape[2]

    def stream(width):
        return pl.BlockSpec((1, TILE, width), lambda i, t: (i, t + off, 0))

    def local(width):
        return pl.BlockSpec((1, TILE, width), lambda i, t: (i, t, 0))

    consts_a = [lw["nw"], lw["wg"], lw["wm"]]
    consts_b = [lw["wb"], lw["wo"]]
    return pl.pallas_call(
        _merge_kernel,
        grid=(b, nq),
        in_specs=[stream(d),
                  pl.BlockSpec((1, 3, d), lambda i, t: (jnp.where(t + off == 0, mod_rows - 1, i), 0, 0))]
                 + [_const_spec(c.shape) for c in consts_a]
                 + [local(bw), local(bw), local(bw), stream(bw), stream(bw)]
                 + [_const_spec(c.shape) for c in consts_b],
        out_specs=local(d),
        out_shape=jax.ShapeDtypeStruct((b, nq * TILE, d), F32),
        compiler_params=_cparams(("arbitrary", "arbitrary")),
        name="merge",
    )(xc, mod, *consts_a, ya, yb, yc, yf, yr, *consts_b)


def _rope_tables(n, n_ctx):
    def axial(rot_dim):
        n_rows = n // GRID_W
        rows = jnp.repeat(jnp.arange(n_rows, dtype=F32), GRID_W)
        cols = jnp.tile(jnp.arange(GRID_W, dtype=F32), n_rows)
        quarter = rot_dim // 4
        freqs = ROPE_THETA ** (-jnp.arange(quarter, dtype=F32) / quarter)
        ang = jnp.concatenate([rows[:, None] * freqs, cols[:, None] * freqs], axis=-1)
        return jnp.cos(ang), jnp.sin(ang)

    def with_ctx(cos, sin):
        return (jnp.concatenate([jnp.ones((n_ctx, LANES), F32), cos], axis=0),
                jnp.concatenate([jnp.zeros((n_ctx, LANES), F32), sin], axis=0))

    cm, sm = axial(MLA_ROPE)
    one = jnp.ones((n, MLA_NOPE), F32)
    zero = jnp.zeros((n, MLA_NOPE), F32)
    pad1 = jnp.ones((n, LANES - MLA_QK), F32)
    pad0 = jnp.zeros((n, LANES - MLA_QK), F32)
    cosm, sinm = with_ctx(jnp.concatenate([one, cm, cm, pad1], axis=-1),
                          jnp.concatenate([zero, sm, sm, pad0], axis=-1))
    ch, sh = axial(HEAD_DIM)
    cosh, sinh = with_ctx(jnp.tile(ch, (1, 4)), jnp.tile(sh, (1, 4)))
    return cosm, sinm, cosh, sinh


def _block_diag(w):
    k, m, _ = w.shape
    out = jnp.zeros((k * m, k * m), w.dtype)
    for i in range(k):
        out = out.at[i * m:(i + 1) * m, i * m:(i + 1) * m].set(w[i])
    return out


def _layer_weights(l, p):
    d = p["w_in"].shape[1]
    w_in = p["w_in"][l]
    sizes = (256, 128, 32, 256, 128, 128, 256, 128, 128, 256, 1024, 4 * d)
    offs = [0]
    for sz in sizes:
        offs.append(offs[-1] + sz)
    cols = [w_in[:, offs[i]:offs[i + 1]] for i in range(len(sizes))]
    kr_slab = jnp.pad(cols[2], ((0, 0), (MLA_NOPE, LANES - MLA_QK)))
    wp = jnp.concatenate([cols[0], cols[1], kr_slab] + cols[3:10], axis=1).astype(BF16)

    def pad_heads(w, nh, width, real_from, real):
        k = w.shape[0]
        w = w.reshape(k, nh, width)[:, :, real_from:real_from + real]
        return jnp.pad(w, ((0, 0), (0, 0), (0, LANES - real))).reshape(k, nh * LANES)

    w_uq = p["mla_w_uq"][l]
    w_ukv = p["mla_w_ukv"][l]
    wuq = pad_heads(w_uq, MLA_HEADS, MLA_QK, 0, MLA_QK).astype(BF16)
    wuk = pad_heads(w_ukv, MLA_HEADS, MLA_NOPE + MLA_V, 0, MLA_NOPE).astype(BF16)
    wuv = w_ukv.reshape(-1, MLA_HEADS, MLA_NOPE + MLA_V)[:, :, MLA_NOPE:].reshape(-1, MLA_HEADS * MLA_V)
    wuv = wuv.astype(BF16)

    def slab_gain(g, scale):
        return (jnp.pad(g, (0, LANES - MLA_QK)) * scale).reshape(1, LANES)

    def head_gain(g, nh, scale):
        return (jnp.tile(g, nh) * scale).reshape(1, nh * HEAD_DIM)

    idx = jnp.arange(4 * HEAD_DIM) // HEAD_DIM
    gmat = (idx[:, None] == idx[None, :]).astype(BF16)

    def hi_lo(w):
        hi = w.astype(BF16)
        return hi, (w - hi.astype(F32)).astype(BF16)

    wr = jnp.stack([_block_diag(p["lru_w_r"][l, dd]) for dd in range(2)])
    wi = jnp.stack([_block_diag(p["lru_w_i"][l, dd]) for dd in range(2)])
    wr_hi, wr_lo = hi_lo(wr)
    wi_hi, wi_lo = hi_lo(wi)
    return {
        "nw": p["norm_w"][l].reshape(1, d),
        "wp": wp, "wuq": wuq, "wuk": wuk, "wuv": wuv,
        "g_cq": p["mla_cq_norm"][l].reshape(1, -1),
        "g_ckv": p["mla_ckv_norm"][l].reshape(1, -1),
        "g_mq": slab_gain(p["mla_q_norm"][l], MLA_QK ** -0.5 * LOG2E),
        "g_mk": slab_gain(p["mla_k_norm"][l], 1.0),
        "g_sq": head_gain(p["swa_q_norm"][l], SWA_Q_HEADS, HEAD_DIM ** -0.5 * LOG2E),
        "g_sk": head_gain(p["swa_k_norm"][l], SWA_KV_HEADS, 1.0),
        "g_aq": head_gain(p["axa_q_norm"][l], AXA_Q_HEADS, HEAD_DIM ** -0.5 * LOG2E),
        "g_ak": head_gain(p["axa_k_norm"][l], AXA_KV_HEADS, 1.0),
        "gmat": gmat,
        "wg": cols[10].astype(BF16), "wm": cols[11].astype(BF16),
        "wb": p["w_branch"][l].astype(BF16), "wo": p["w_out"][l].astype(BF16),
        "conv_w": p["lru_conv_w"][l], "conv_b": p["lru_conv_b"][l].reshape(1, -1),
        "wr_hi": wr_hi, "wr_lo": wr_lo, "wi_hi": wi_hi, "wi_lo": wi_lo,
        "b_r": p["lru_b_r"][l], "b_i": p["lru_b_i"][l], "lam": p["lru_lambda"][l],
        "sink": p["swa_sink"][l],
    }


def kernel(x, c, ctx, c_ctx, w_mod, b_mod, norm_w, w_in, mla_cq_norm, mla_ckv_norm, mla_w_uq, mla_w_ukv, mla_q_norm, mla_k_norm, swa_q_norm, swa_k_norm, swa_sink, axa_q_norm, axa_k_norm, lru_conv_w, lru_conv_b, lru_w_r, lru_b_r, lru_w_i, lru_b_i, lru_lambda, w_branch, w_out):
    p = dict(norm_w=norm_w, w_in=w_in, mla_cq_norm=mla_cq_norm, mla_ckv_norm=mla_ckv_norm,
             mla_w_uq=mla_w_uq, mla_w_ukv=mla_w_ukv, mla_q_norm=mla_q_norm, mla_k_norm=mla_k_norm,
             swa_q_norm=swa_q_norm, swa_k_norm=swa_k_norm, swa_sink=swa_sink,
             axa_q_norm=axa_q_norm, axa_k_norm=axa_k_norm, lru_conv_w=lru_conv_w,
             lru_conv_b=lru_conv_b, lru_w_r=lru_w_r, lru_b_r=lru_b_r, lru_w_i=lru_w_i,
             lru_b_i=lru_b_i, lru_lambda=lru_lambda, w_branch=w_branch, w_out=w_out)
    b, n, d = x.shape
    n_ctx = ctx.shape[1]
    depth = w_mod.shape[0]
    assert n_ctx == TILE and n % TILE == 0 and n % GRID_W == 0

    rows = jnp.concatenate([c, c_ctx[None, :]], axis=0)
    n_rows = -(-rows.shape[0] // SUBLANES) * SUBLANES
    rows = jnp.pad(rows, ((0, n_rows - rows.shape[0]), (0, 0)))
    mod_all = _modulation(rows, w_mod, b_mod)[:, :b + 1].reshape(depth, b + 1, 3, d)

    tabs = _rope_tables(n, n_ctx)
    xc = jnp.concatenate([ctx, x], axis=1)
    for l in range(depth):
        lw = _layer_weights(l, p)
        mod = mod_all[l]
        upd = l < depth - 1
        mq, mk, mv, sq, sk, sv, aq, ak, av, zl = _project(xc, mod, lw, tabs, b)
        ya = _dense_attention(mq, mk, mv, 1, upd)
        yb = _window_attention(lw["sink"], sq, sk, sv, upd)
        yc = _dense_attention(aq, ak, av, AXA_Q_HEADS // AXA_KV_HEADS, upd)
        yf, yr = _rglru(zl, lw)
        xc = _merge(xc, mod, lw, ya, yb, yc, yf, yr, upd)
    return xc
```

```python
import functools

import jax
import jax.numpy as jnp
from jax import lax
from jax.experimental import pallas as pl
from jax.experimental.pallas import tpu as pltpu

GRID_W = 64
N_BRANCH = 4
HEAD_DIM = 64
MLA_HEADS = 4
MLA_NOPE = 64
MLA_ROPE = 32
MLA_V = 64
MLA_QK = MLA_NOPE + MLA_ROPE
SWA_Q_HEADS = 4
SWA_KV_HEADS = 2
WINDOW = 128
AXA_Q_HEADS = 4
AXA_KV_HEADS = 2
LRU_BLOCKS = 4
LRU_C = 8.0
CONV_W = 4
CONV_LEFT = 2
ROPE_THETA = 10000.0
RMS_EPS = 1e-6
NEG_INF = -1e30
LOG2E = 1.4426950408889634

LANES = 128
SUBLANES = 8
TILE = 256
VMEM_LIMIT = 56 * 1024 * 1024

F32 = jnp.float32
BF16 = jnp.bfloat16


def _dot(a, b):
    return jnp.dot(a, b, preferred_element_type=F32)


def _dot_nt(a, b):
    return lax.dot_general(a, b, (((1,), (1,)), ((), ())), preferred_element_type=F32)


def _split(a):
    hi = a.astype(BF16)
    lo = (a - hi.astype(F32)).astype(BF16)
    return hi, lo


def _dot3(a, w_hi, w_lo):
    a_hi, a_lo = _split(a)
    return _dot(a_hi, w_hi) + _dot(a_lo, w_hi) + _dot(a_hi, w_lo)


def _sigmoid(v):
    return 1.0 / (1.0 + jnp.exp(-v))


def _cparams(sem):
    return pltpu.CompilerParams(dimension_semantics=sem, vmem_limit_bytes=VMEM_LIMIT)


def _const_spec(shape):
    nd = len(shape)
    return pl.BlockSpec(shape, lambda *_: (0,) * nd)


def _mod_kernel(c_ref, w_ref, b_ref, o_ref):
    v = c_ref[...]
    a = v * _sigmoid(v)
    w_hi, w_lo = _split(w_ref[0])
    o_ref[0] = _dot3(a, w_hi, w_lo) + b_ref[0]


def _modulation(cvec, w_mod, b_mod):
    depth, d, d3 = w_mod.shape
    r = cvec.shape[0]
    bn = 512
    return pl.pallas_call(
        _mod_kernel,
        grid=(depth, d3 // bn),
        in_specs=[pl.BlockSpec((r, d), lambda l, j: (0, 0)),
                  pl.BlockSpec((1, d, bn), lambda l, j: (l, 0, j)),
                  pl.BlockSpec((1, 1, bn), lambda l, j: (l, 0, j))],
        out_specs=pl.BlockSpec((1, r, bn), lambda l, j: (l, 0, j)),
        out_shape=jax.ShapeDtypeStruct((depth, r, d3), F32),
        compiler_params=_cparams(("arbitrary", "arbitrary")),
        name="modulation",
    )(cvec, w_mod, b_mod.reshape(depth, 1, d3))


def _modulated_norm(x, mod_ref, nw_ref):
    ms = jnp.mean(x * x, axis=-1, keepdims=True)
    y = x * lax.rsqrt(ms + RMS_EPS) * nw_ref[...]
    return y * (1.0 + mod_ref[0, 1:2, :]) + mod_ref[0, 0:1, :]


def _lane_rinv(v, n):
    return lax.rsqrt(jnp.sum(v * v, axis=-1, keepdims=True) * (1.0 / n) + RMS_EPS)


def _head_rinv(v, gmat):
    hi, lo = _split(v * v)
    ss = _dot(hi, gmat) + _dot(lo, gmat)
    return lax.rsqrt(ss * (1.0 / HEAD_DIM) + RMS_EPS)


def _proj_kernel(x_ref, mod_ref, nw_ref, wp_ref, wuq_ref, wuqr_ref, wuk_ref, wuv_ref,
                 g_cq, g_ckv, g_mq, g_mk, g_sq, g_sk, g_aq, g_ak, gmat_ref,
                 cosm_ref, sinm_ref, cosh_ref, sinh_ref,
                 mq_o, mk_o, mv_o, sq_o, sk_o, sv_o, aq_o, ak_o, av_o, lru_o):
    x = x_ref[0]
    h = _modulated_norm(x, mod_ref, nw_ref)
    z = _dot(h.astype(BF16), wp_ref[...])
    t = x.shape[0]
    lo = lax.broadcasted_iota(jnp.int32, (t, LANES), 1) < HEAD_DIM
    cosm, sinm = cosm_ref[...], sinm_ref[...]
    cosh, sinh = cosh_ref[...], sinh_ref[...]
    gmat = gmat_ref[...]

    def pair_store(out, slab, swapped):
        out[0, 0] = jnp.where(lo, slab, 0.0).astype(BF16)
        out[0, 1] = jnp.where(lo, 0.0, swapped).astype(BF16)
        out[0, 2] = jnp.where(lo, swapped, 0.0).astype(BF16)
        out[0, 3] = jnp.where(lo, 0.0, slab).astype(BF16)

    cq = (z[:, 0:256] * _lane_rinv(z[:, 0:256], 256.0) * g_cq[...]).astype(BF16)
    q = _dot(cq, wuq_ref[...])
    qr = _dot(cq, wuqr_ref[...])
    ckv = (z[:, 256:384] * _lane_rinv(z[:, 256:384], 128.0) * g_ckv[...]).astype(BF16)
    kn = _dot(ckv, wuk_ref[...])
    vv = _dot(ckv, wuv_ref[...])
    kr = z[:, 384:512]
    krr = z[:, 512:640]
    gc_q = g_mq[...] * cosm
    gc_k = g_mk[...] * cosm
    for hh in range(MLA_HEADS):
        sl = slice(hh * LANES, (hh + 1) * LANES)
        r = _lane_rinv(q[:, sl], float(MLA_QK))
        mq_o[0, hh] = ((q[:, sl] * r) * gc_q + (qr[:, sl] * r) * sinm).astype(BF16)
        ks = kn[:, sl] + kr
        r = _lane_rinv(ks, float(MLA_QK))
        mk_o[0, hh] = ((ks * r) * gc_k + (krr * r) * sinm).astype(BF16)
    for i in range(MLA_HEADS // 2):
        vs = vv[:, i * LANES:(i + 1) * LANES]
        mv_o[0, 2 * i] = jnp.where(lo, vs, 0.0).astype(BF16)
        mv_o[0, 2 * i + 1] = jnp.where(lo, 0.0, vs).astype(BF16)

    def gqa(c0, g_q, g_k, q_o, k_o, v_o):
        zq, zqr = z[:, c0:c0 + 256], z[:, c0 + 256:c0 + 512]
        r = _head_rinv(zq, gmat)
        gc = g_q[...] * cosh
        for i in range(2):
            sl = slice(i * LANES, (i + 1) * LANES)
            q_o[0, i] = ((zq[:, sl] * r[:, sl]) * gc + (zqr[:, sl] * r[:, sl]) * sinh).astype(BF16)
        zk, zkr = z[:, c0 + 512:c0 + 640], z[:, c0 + 640:c0 + 768]
        r = _head_rinv(zk, gmat[:LANES, :LANES])
        ks = (zk * r) * (g_k[...] * cosh) + (zkr * r) * sinh
        pair_store(k_o, ks, pltpu.roll(ks, HEAD_DIM, 1))
        zv = z[:, c0 + 768:c0 + 896]
        pair_store(v_o, zv, pltpu.roll(zv, HEAD_DIM, 1))

    gqa(640, g_sq, g_sk, sq_o, sk_o, sv_o)
    gqa(1536, g_aq, g_ak, aq_o, ak_o, av_o)
    lru_o[0] = z[:, 2432:2688]


def _project(xc, mod, lw, tabs):
    b, s, d = xc.shape
    nt = s // TILE
    mod_rows = mod.shape[0]

    def tile_spec(width):
        return pl.BlockSpec((1, TILE, width), lambda t, i: (i, t, 0))

    def head_spec(nh):
        return pl.BlockSpec((1, nh, TILE, LANES), lambda t, i: (i, 0, t, 0))

    tab_spec = pl.BlockSpec((TILE, LANES), lambda t, i: (t, 0))
    consts = [lw["nw"], lw["wp"], lw["wuq"], lw["wuqr"], lw["wuk"], lw["wuv"], lw["g_cq"], lw["g_ckv"],
              lw["g_mq"], lw["g_mk"], lw["g_sq"], lw["g_sk"], lw["g_aq"], lw["g_ak"], lw["gmat"]]
    in_specs = ([tile_spec(d),
                 pl.BlockSpec((1, 3, d), lambda t, i: (jnp.where(t == 0, mod_rows - 1, i), 0, 0))]
                + [_const_spec(c.shape) for c in consts] + [tab_spec] * 4)
    n_slabs = [MLA_HEADS, MLA_HEADS, MLA_HEADS, 2, 4, 4, 2, 4, 4]
    out_specs = [head_spec(nh) for nh in n_slabs] + [tile_spec(4 * HEAD_DIM)]
    out_shape = ([jax.ShapeDtypeStruct((b, nh, s, LANES), BF16) for nh in n_slabs]
                 + [jax.ShapeDtypeStruct((b, s, 4 * HEAD_DIM), F32)])
    return pl.pallas_call(
        _proj_kernel,
        grid=(nt, b),
        in_specs=in_specs,
        out_specs=out_specs,
        out_shape=out_shape,
        compiler_params=_cparams(("arbitrary", "arbitrary")),
        name="project",
    )(xc, mod, *consts, *tabs)


def _dense_attn_kernel(q_ref, k_ref, v_ref, o_ref, *, q_share, ctx_tile_first):
    s_len = k_ref.shape[2]

    def attend(n_keys):
        slabs = []
        for i in range(2):
            acc = None
            for hh in (2 * i, 2 * i + 1):
                k = k_ref[0, hh, 0:n_keys, :]
                v = v_ref[0, hh, 0:n_keys, :]
                s = _dot_nt(q_ref[0, hh // q_share], k)
                m = jnp.max(s, axis=-1, keepdims=True)
                p = jnp.exp2(s - m)
                l = jnp.sum(p, axis=-1, keepdims=True)
                o = _dot(p.astype(BF16), v) / l
                acc = o if acc is None else acc + o
            slabs.append(acc)
        o_ref[0] = jnp.concatenate(slabs, axis=-1)

    if ctx_tile_first:
        pl.when(pl.program_id(1) == 0)(lambda: attend(TILE))
        pl.when(pl.program_id(1) > 0)(lambda: attend(s_len))
    else:
        attend(s_len)


def _dense_attention(q, k, v, with_ctx_queries):
    b, nq_slabs, s, _ = q.shape
    nh = k.shape[1]
    nt = s // TILE
    off = 0 if with_ctx_queries else 1
    nq = nt - off
    kern = functools.partial(_dense_attn_kernel, q_share=nh // nq_slabs,
                             ctx_tile_first=with_ctx_queries)
    return pl.pallas_call(
        kern,
        grid=(b, nq),
        in_specs=[pl.BlockSpec((1, nq_slabs, TILE, LANES), lambda i, t: (i, 0, t + off, 0)),
                  pl.BlockSpec((1, nh, s, LANES), lambda i, t: (i, 0, 0, 0)),
                  pl.BlockSpec((1, nh, s, LANES), lambda i, t: (i, 0, 0, 0))],
        out_specs=pl.BlockSpec((1, TILE, 2 * LANES), lambda i, t: (i, t, 0)),
        out_shape=jax.ShapeDtypeStruct((b, nq * TILE, 2 * LANES), F32),
        compiler_params=_cparams(("arbitrary", "arbitrary")),
        name="dense_attention",
    )(q, k, v)


def _window_attn_kernel(sink_ref, q_ref, k_ref, v_ref, o_ref, *, tile_off):
    s_len = k_ref.shape[2]
    n_tiles = s_len // TILE
    ti = pl.program_id(1) + tile_off
    is_lat = ti > 0
    t0 = ti * TILE
    half = TILE // 2
    left0 = pl.multiple_of(jnp.maximum(t0 - half, 0), half)
    cen0 = pl.multiple_of(t0, TILE)
    right0 = pl.multiple_of(jnp.minimum(t0 + TILE, s_len - half), half)
    r_c = lax.broadcasted_iota(jnp.int32, (TILE, TILE), 0)
    c_c = lax.broadcasted_iota(jnp.int32, (TILE, TILE), 1)
    r_s = lax.broadcasted_iota(jnp.int32, (TILE, half), 0)
    c_s = lax.broadcasted_iota(jnp.int32, (TILE, half), 1)
    ok_cen = (jnp.abs(r_c - c_c) <= WINDOW) & is_lat
    ok_left = ((r_s - c_s + half) <= WINDOW) & (ti > 1)
    ok_right = ((c_s - r_s + TILE) <= WINDOW) & is_lat & (ti < n_tiles - 1)
    slabs = []
    for i in range(2):
        acc = None
        for hh in (2 * i, 2 * i + 1):
            q = q_ref[0, i]
            sink = sink_ref[hh] * LOG2E

            def kv(start, size, hh=hh):
                return k_ref[0, hh, pl.ds(start, size), :], v_ref[0, hh, pl.ds(start, size), :]

            k_x, v_x = kv(0, TILE)
            k_l, v_l = kv(left0, half)
            k_c, v_c = kv(cen0, TILE)
            k_r, v_r = kv(right0, half)
            s_x = _dot_nt(q, k_x)
            s_l = jnp.where(ok_left, _dot_nt(q, k_l), NEG_INF)
            s_c = jnp.where(ok_cen, _dot_nt(q, k_c), NEG_INF)
            s_r = jnp.where(ok_right, _dot_nt(q, k_r), NEG_INF)
            m = jnp.maximum(jnp.maximum(jnp.max(s_x, -1, keepdims=True), jnp.max(s_c, -1, keepdims=True)),
                            jnp.maximum(jnp.max(s_l, -1, keepdims=True), jnp.max(s_r, -1, keepdims=True)))
            m = jnp.maximum(m, sink)
            p_x, p_l, p_c, p_r = (jnp.exp2(s_x - m), jnp.exp2(s_l - m), jnp.exp2(s_c - m), jnp.exp2(s_r - m))
            den = (jnp.sum(p_x, -1, keepdims=True) + jnp.sum(p_l, -1, keepdims=True)
                   + jnp.sum(p_c, -1, keepdims=True) + jnp.sum(p_r, -1, keepdims=True) + jnp.exp2(sink - m))
            num = (_dot(p_x.astype(BF16), v_x) + _dot(p_l.astype(BF16), v_l)
                   + _dot(p_c.astype(BF16), v_c) + _dot(p_r.astype(BF16), v_r))
            o = num / den
            acc = o if acc is None else acc + o
        slabs.append(acc)
    o_ref[0] = jnp.concatenate(slabs, axis=-1)


def _window_attention(sink, q, k, v, with_ctx_queries):
    b, nq_slabs, s, _ = q.shape
    nh = k.shape[1]
    nt = s // TILE
    off = 0 if with_ctx_queries else 1
    nq = nt - off
    kern = functools.partial(_window_attn_kernel, tile_off=off)
    return pl.pallas_call(
        kern,
        grid=(b, nq),
        in_specs=[pl.BlockSpec(memory_space=pltpu.SMEM),
                  pl.BlockSpec((1, nq_slabs, TILE, LANES), lambda i, t: (i, 0, t + off, 0)),
                  pl.BlockSpec((1, nh, s, LANES), lambda i, t: (i, 0, 0, 0)),
                  pl.BlockSpec((1, nh, s, LANES), lambda i, t: (i, 0, 0, 0))],
        out_specs=pl.BlockSpec((1, TILE, 2 * LANES), lambda i, t: (i, t, 0)),
        out_shape=jax.ShapeDtypeStruct((b, nq * TILE, 2 * LANES), F32),
        compiler_params=_cparams(("arbitrary", "arbitrary")),
        name="window_attention",
    )(sink, q, k, v)


def _lru_conv(prev_ref, cur_ref, next_ref, prev_ok, next_ok, cw_ref, cb_ref):
    t = cur_ref.shape[1]
    prev = jnp.where(prev_ok, prev_ref[0], 0.0)
    nxt = jnp.where(next_ok, next_ref[0], 0.0)
    ext = jnp.concatenate([prev, cur_ref[0], nxt], axis=0)
    n = t + 2 * SUBLANES
    u = cb_ref[...] + cur_ref[0] * cw_ref[CONV_LEFT:CONV_LEFT + 1, :]
    for j in range(CONV_W):
        off = j - CONV_LEFT
        if off == 0:
            continue
        sh = pltpu.roll(ext, (-off) % n, 0)[SUBLANES:SUBLANES + t]
        u = u + sh * cw_ref[j:j + 1, :]
    return u


def _lru_scan_tile(u, d, wr_hi, wr_lo, wi_hi, wi_lo, br_ref, bi_ref, lam_ref, h0, reverse):
    t = u.shape[0]
    r = _sigmoid(_dot3(u, wr_hi[d], wr_lo[d]) + br_ref[d:d + 1, :])
    i = _sigmoid(_dot3(u, wi_hi[d], wi_lo[d]) + bi_ref[d:d + 1, :])
    lam = lam_ref[d:d + 1, :]
    softplus_neg = jnp.maximum(-lam, 0.0) + jnp.log1p(jnp.exp(-jnp.abs(lam)))
    log_a = -LRU_C * r * softplus_neg
    a = jnp.exp(log_a)
    bx = jnp.sqrt(1.0 - a * a) * (i * u)
    row = lax.broadcasted_iota(jnp.int32, u.shape, 0)
    step = 1
    while step < t:
        if reverse:
            ok = row < t - step
            a_sh = jnp.where(ok, pltpu.roll(a, t - step, 0), 1.0)
            b_sh = jnp.where(ok, pltpu.roll(bx, t - step, 0), 0.0)
        else:
            ok = row >= step
            a_sh = jnp.where(ok, pltpu.roll(a, step, 0), 1.0)
            b_sh = jnp.where(ok, pltpu.roll(bx, step, 0), 0.0)
        bx = a * b_sh + bx
        a = a * a_sh
        step *= 2
    h = a * h0 + bx
    h_end = h[0:1, :] if reverse else h[t - 1:t, :]
    return h, h_end


def _lru_kernel(fp_ref, fc_ref, fn_ref, rp_ref, rc_ref, rn_ref, cw_ref, cb_ref,
                wr_hi, wr_lo, wi_hi, wi_lo, br_ref, bi_ref, lam_ref,
                yf_ref, yb_ref, hf_s, hb_s):
    j = pl.program_id(1)
    nt = pl.num_programs(1)

    @pl.when(j == 0)
    def _():
        hf_s[...] = jnp.zeros_like(hf_s)
        hb_s[...] = jnp.zeros_like(hb_s)

    def bounds(ti):
        return ti >= 2, (ti >= 1) & (ti < nt - 1)

    tf = j
    tr = jnp.where(j == 0, 0, nt - j)
    p_ok, n_ok = bounds(tf)
    uf = _lru_conv(fp_ref, fc_ref, fn_ref, p_ok, n_ok, cw_ref, cb_ref)
    h, h_end = _lru_scan_tile(uf, 0, wr_hi, wr_lo, wi_hi, wi_lo, br_ref, bi_ref, lam_ref,
                              hf_s[...], False)
    yf_ref[0] = h
    hf_s[...] = h_end
    p_ok, n_ok = bounds(tr)
    ur = _lru_conv(rp_ref, rc_ref, rn_ref, p_ok, n_ok, cw_ref, cb_ref)
    h, h_end = _lru_scan_tile(ur, 1, wr_hi, wr_lo, wi_hi, wi_lo, br_ref, bi_ref, lam_ref,
                              hb_s[...], True)
    yb_ref[0] = h
    hb_s[...] = h_end


def _rglru(z, lw):
    b, s, c = z.shape
    nt = s // TILE
    r8 = TILE // SUBLANES
    n8 = s // SUBLANES

    def rev(j):
        return jnp.where(j == 0, 0, nt - j)

    def cur(f):
        return pl.BlockSpec((1, TILE, c), lambda i, j: (i, f(j), 0))

    def prev(f):
        return pl.BlockSpec((1, SUBLANES, c), lambda i, j: (i, jnp.maximum(f(j) * r8 - 1, 0), 0))

    def nxt(f):
        return pl.BlockSpec((1, SUBLANES, c), lambda i, j: (i, jnp.minimum((f(j) + 1) * r8, n8 - 1), 0))

    ident = lambda j: j
    consts = [lw["conv_w"], lw["conv_b"], lw["wr_hi"], lw["wr_lo"], lw["wi_hi"], lw["wi_lo"],
              lw["b_r"], lw["b_i"], lw["lam"]]
    return pl.pallas_call(
        _lru_kernel,
        grid=(b, nt),
        in_specs=[prev(ident), cur(ident), nxt(ident), prev(rev), cur(rev), nxt(rev)]
                 + [_const_spec(x.shape) for x in consts],
        out_specs=[cur(ident), cur(rev)],
        out_shape=[jax.ShapeDtypeStruct((b, s, c), F32)] * 2,
        scratch_shapes=[pltpu.VMEM((1, c), F32), pltpu.VMEM((1, c), F32)],
        compiler_params=_cparams(("arbitrary", "arbitrary")),
        name="rglru",
    )(z, z, z, z, z, z, *consts)


def _merge_kernel(x_ref, mod_ref, nw_ref, wg_ref, wm_ref, ya_ref, yb_ref, yc_ref, yf_ref, yr_ref,
                  wb_ref, wo_ref, o_ref):
    x = x_ref[0]
    hb = _modulated_norm(x, mod_ref, nw_ref).astype(BF16)
    zg = _dot(hb, wg_ref[...])
    d = x.shape[1]
    bw = ya_ref.shape[2]
    branches = (ya_ref[0], yb_ref[0], yc_ref[0], yf_ref[0] + yr_ref[0])
    mix = jnp.zeros_like(x)
    for k in range(N_BRANCH):
        g = zg[:, k * bw:(k + 1) * bw]
        y = branches[k] * (g * _sigmoid(g))
        proj = _dot(y.astype(BF16), wb_ref[k])
        zm = _dot(hb, wm_ref[:, k * d:(k + 1) * d])
        mix = mix + _sigmoid(zm) * proj
    o_ref[0] = x + mod_ref[0, 2:3, :] * _dot(mix.astype(BF16), wo_ref[...])


def _merge(xc, mod, lw, ya, yb, yc, yf, yr, with_ctx):
    b, s, d = xc.shape
    nt = s // TILE
    off = 0 if with_ctx else 1
    nq = nt - off
    mod_rows = mod.shape[0]
    bw = ya.shape[2]

    def stream(width):
        return pl.BlockSpec((1, TILE, width), lambda i, t: (i, t + off, 0))

    def local(width):
        return pl.BlockSpec((1, TILE, width), lambda i, t: (i, t, 0))

    consts_a = [lw["nw"], lw["wg"], lw["wm"]]
    consts_b = [lw["wb"], lw["wo"]]
    return pl.pallas_call(
        _merge_kernel,
        grid=(b, nq),
        in_specs=[stream(d),
                  pl.BlockSpec((1, 3, d), lambda i, t: (jnp.where(t + off == 0, mod_rows - 1, i), 0, 0))]
                 + [_const_spec(c.shape) for c in consts_a]
                 + [local(bw), local(bw), local(bw), stream(bw), stream(bw)]
                 + [_const_spec(c.shape) for c in consts_b],
        out_specs=local(d),
        out_shape=jax.ShapeDtypeStruct((b, nq * TILE, d), F32),
        compiler_params=_cparams(("arbitrary", "arbitrary")),
        name="merge",
    )(xc, mod, *consts_a, ya, yb, yc, yf, yr, *consts_b)


def _rope_tables(n, n_ctx):
    def axial(rot_dim):
        n_rows = n // GRID_W
        rows = jnp.repeat(jnp.arange(n_rows, dtype=F32), GRID_W)
        cols = jnp.tile(jnp.arange(GRID_W, dtype=F32), n_rows)
        quarter = rot_dim // 4
        freqs = ROPE_THETA ** (-jnp.arange(quarter, dtype=F32) / quarter)
        ang = jnp.concatenate([rows[:, None] * freqs, cols[:, None] * freqs], axis=-1)
        return jnp.cos(ang), jnp.sin(ang)

    def with_ctx(cos, sin):
        return (jnp.concatenate([jnp.ones((n_ctx, LANES), F32), cos], axis=0),
                jnp.concatenate([jnp.zeros((n_ctx, LANES), F32), sin], axis=0))

    cm, sm = axial(MLA_ROPE)
    one = jnp.ones((n, MLA_NOPE), F32)
    zero = jnp.zeros((n, MLA_NOPE), F32)
    pad1 = jnp.ones((n, LANES - MLA_QK), F32)
    pad0 = jnp.zeros((n, LANES - MLA_QK), F32)
    cosm, sinm = with_ctx(jnp.concatenate([one, cm, cm, pad1], axis=-1),
                          jnp.concatenate([zero, sm, sm, pad0], axis=-1))
    ch, sh = axial(HEAD_DIM)
    cosh, sinh = with_ctx(jnp.tile(ch, (1, 4)), jnp.tile(sh, (1, 4)))
    return cosm, sinm, cosh, sinh


def _block_diag(w):
    k, m, _ = w.shape
    out = jnp.zeros((k * m, k * m), w.dtype)
    for i in range(k):
        out = out.at[i * m:(i + 1) * m, i * m:(i + 1) * m].set(w[i])
    return out


def _rot_fold(w, gain, n_heads, half):
    k = w.shape[0]
    wg = (w * gain).reshape(k, n_heads, 2, half)
    return jnp.concatenate([-wg[:, :, 1], wg[:, :, 0]], axis=2).reshape(k, n_heads * 2 * half)


def _layer_weights(l, p):
    d = p["w_in"].shape[1]
    w_in = p["w_in"][l]
    sizes = (256, 128, 32, 256, 128, 128, 256, 128, 128, 256, 1024, 4 * d)
    offs = [0]
    for sz in sizes:
        offs.append(offs[-1] + sz)
    cq, ckv, kr, sq, sk, sv, aq, ak, av, lru, gate, merge = (
        w_in[:, offs[i]:offs[i + 1]] for i in range(len(sizes)))

    q_scale = HEAD_DIM ** -0.5 * LOG2E
    mq_scale = MLA_QK ** -0.5 * LOG2E
    g_sq = jnp.tile(p["swa_q_norm"][l] * q_scale, 2)
    g_sk = jnp.tile(p["swa_k_norm"][l], 2)
    g_aq = jnp.tile(p["axa_q_norm"][l] * q_scale, 2)
    g_ak = jnp.tile(p["axa_k_norm"][l], 2)
    g_mq = p["mla_q_norm"][l] * mq_scale
    g_mk = p["mla_k_norm"][l]

    def rope_slab(w_rope, g_rope):
        rot = _rot_fold(w_rope.reshape(w_rope.shape[0], -1), jnp.tile(g_rope, w_rope.shape[1]),
                        w_rope.shape[1], MLA_ROPE // 2).reshape(w_rope.shape)
        return jnp.pad(rot, ((0, 0), (0, 0), (MLA_NOPE, LANES - MLA_QK)))

    kr_slab = jnp.pad(kr, ((0, 0), (MLA_NOPE, LANES - MLA_QK)))
    krr_slab = rope_slab(kr[:, None, :], g_mk[MLA_NOPE:])[:, 0]
    half = HEAD_DIM // 2
    wp = jnp.concatenate([
        cq, ckv, kr_slab, krr_slab,
        sq, _rot_fold(sq, jnp.tile(g_sq, 2), SWA_Q_HEADS, half),
        sk, _rot_fold(sk, g_sk, SWA_KV_HEADS, half), sv,
        aq, _rot_fold(aq, jnp.tile(g_aq, 2), AXA_Q_HEADS, half),
        ak, _rot_fold(ak, g_ak, AXA_KV_HEADS, half), av,
        lru], axis=1).astype(BF16)

    w_uq = p["mla_w_uq"][l].reshape(-1, MLA_HEADS, MLA_QK)
    w_ukv = p["mla_w_ukv"][l].reshape(-1, MLA_HEADS, MLA_NOPE + MLA_V)
    wuq = jnp.pad(w_uq, ((0, 0), (0, 0), (0, LANES - MLA_QK))).reshape(-1, MLA_HEADS * LANES).astype(BF16)
    wuqr = rope_slab(w_uq[:, :, MLA_NOPE:], g_mq[MLA_NOPE:]).reshape(-1, MLA_HEADS * LANES).astype(BF16)
    wuk = jnp.pad(w_ukv[:, :, :MLA_NOPE], ((0, 0), (0, 0), (0, LANES - MLA_NOPE)))
    wuk = wuk.reshape(-1, MLA_HEADS * LANES).astype(BF16)
    wuv = w_ukv[:, :, MLA_NOPE:].reshape(-1, MLA_HEADS * MLA_V).astype(BF16)

    def slab_gain(g):
        return jnp.pad(g, (0, LANES - MLA_QK)).reshape(1, LANES)

    idx = jnp.arange(4 * HEAD_DIM) // HEAD_DIM
    gmat = (idx[:, None] == idx[None, :]).astype(BF16)

    def hi_lo(w):
        hi = w.astype(BF16)
        return hi, (w - hi.astype(F32)).astype(BF16)

    wr = jnp.stack([_block_diag(p["lru_w_r"][l, dd]) for dd in range(2)])
    wi = jnp.stack([_block_diag(p["lru_w_i"][l, dd]) for dd in range(2)])
    wr_hi, wr_lo = hi_lo(wr)
    wi_hi, wi_lo = hi_lo(wi)
    return {
        "nw": p["norm_w"][l].reshape(1, d),
        "wp": wp, "wuq": wuq, "wuqr": wuqr, "wuk": wuk, "wuv": wuv,
        "g_cq": p["mla_cq_norm"][l].reshape(1, -1),
        "g_ckv": p["mla_ckv_norm"][l].reshape(1, -1),
        "g_mq": slab_gain(g_mq), "g_mk": slab_gain(g_mk),
        "g_sq": g_sq.reshape(1, LANES), "g_sk": g_sk.reshape(1, LANES),
        "g_aq": g_aq.reshape(1, LANES), "g_ak": g_ak.reshape(1, LANES),
        "gmat": gmat,
        "wg": gate.astype(BF16), "wm": merge.astype(BF16),
        "wb": p["w_branch"][l].astype(BF16), "wo": p["w_out"][l].astype(BF16),
        "conv_w": p["lru_conv_w"][l], "conv_b": p["lru_conv_b"][l].reshape(1, -1),
        "wr_hi": wr_hi, "wr_lo": wr_lo, "wi_hi": wi_hi, "wi_lo": wi_lo,
        "b_r": p["lru_b_r"][l], "b_i": p["lru_b_i"][l], "lam": p["lru_lambda"][l],
        "sink": p["swa_sink"][l],
    }


def kernel(x, c, ctx, c_ctx, w_mod, b_mod, norm_w, w_in, mla_cq_norm, mla_ckv_norm, mla_w_uq, mla_w_ukv, mla_q_norm, mla_k_norm, swa_q_norm, swa_k_norm, swa_sink, axa_q_norm, axa_k_norm, lru_conv_w, lru_conv_b, lru_w_r, lru_b_r, lru_w_i, lru_b_i, lru_lambda, w_branch, w_out):
    p = dict(norm_w=norm_w, w_in=w_in, mla_cq_norm=mla_cq_norm, mla_ckv_norm=mla_ckv_norm,
             mla_w_uq=mla_w_uq, mla_w_ukv=mla_w_ukv, mla_q_norm=mla_q_norm, mla_k_norm=mla_k_norm,
             swa_q_norm=swa_q_norm, swa_k_norm=swa_k_norm, swa_sink=swa_sink,
             axa_q_norm=axa_q_norm, axa_k_norm=axa_k_norm, lru_conv_w=lru_conv_w,
             lru_conv_b=lru_conv_b, lru_w_r=lru_w_r, lru_b_r=lru_b_r, lru_w_i=lru_w_i,
             lru_b_i=lru_b_i, lru_lambda=lru_lambda, w_branch=w_branch, w_out=w_out)
    b, n, d = x.shape
    n_ctx = ctx.shape[1]
    depth = w_mod.shape[0]
    assert n_ctx == TILE and n % TILE == 0 and n % GRID_W == 0

    rows = jnp.concatenate([c, c_ctx[None, :]], axis=0)
    n_rows = -(-rows.shape[0] // SUBLANES) * SUBLANES
    rows = jnp.pad(rows, ((0, n_rows - rows.shape[0]), (0, 0)))
    mod_all = _modulation(rows, w_mod, b_mod)[:, :b + 1].reshape(depth, b + 1, 3, d)

    tabs = _rope_tables(n, n_ctx)
    xc = jnp.concatenate([ctx, x], axis=1)
    for l in range(depth):
        lw = _layer_weights(l, p)
        mod = mod_all[l]
        upd = l < depth - 1
        mq, mk, mv, sq, sk, sv, aq, ak, av, zl = _project(xc, mod, lw, tabs)
        ya = _dense_attention(mq, mk, mv, upd)
        yb = _window_attention(lw["sink"], sq, sk, sv, upd)
        yc = _dense_attention(aq, ak, av, upd)
        yf, yr = _rglru(zl, lw)
        xc = _merge(xc, mod, lw, ya, yb, yc, yf, yr, upd)
    return xc
```

```python
import functools

import jax
import jax.numpy as jnp
from jax import lax
from jax.experimental import pallas as pl
from jax.experimental.pallas import tpu as pltpu

GRID_W = 64
N_BRANCH = 4
HEAD_DIM = 64
MLA_HEADS = 4
MLA_NOPE = 64
MLA_ROPE = 32
MLA_V = 64
MLA_QK = MLA_NOPE + MLA_ROPE
SWA_Q_HEADS = 4
SWA_KV_HEADS = 2
WINDOW = 128
AXA_Q_HEADS = 4
AXA_KV_HEADS = 2
LRU_BLOCKS = 4
LRU_C = 8.0
CONV_W = 4
CONV_LEFT = 2
ROPE_THETA = 10000.0
RMS_EPS = 1e-6
NEG_INF = -1e30
LOG2E = 1.4426950408889634

LANES = 128
SUBLANES = 8
TILE = 256
KEY_CHUNKS = 4
VMEM_LIMIT = 56 * 1024 * 1024

F32 = jnp.float32
BF16 = jnp.bfloat16


def _dot(a, b):
    return jnp.dot(a, b, preferred_element_type=F32)


def _dot_nt(a, b):
    return lax.dot_general(a, b, (((1,), (1,)), ((), ())), preferred_element_type=F32)


def _split(a):
    hi = a.astype(BF16)
    lo = (a - hi.astype(F32)).astype(BF16)
    return hi, lo


def _dot3(a, w_hi, w_lo):
    a_hi, a_lo = _split(a)
    return _dot(a_hi, w_hi) + _dot(a_lo, w_hi) + _dot(a_hi, w_lo)


def _sigmoid(v):
    return 1.0 / (1.0 + jnp.exp(-v))


def _cparams(sem):
    return pltpu.CompilerParams(dimension_semantics=sem, vmem_limit_bytes=VMEM_LIMIT)


def _const_spec(shape):
    nd = len(shape)
    return pl.BlockSpec(shape, lambda *_: (0,) * nd)


def _mod_kernel(c_ref, w_ref, b_ref, o_ref):
    v = c_ref[...]
    a = v * _sigmoid(v)
    w_hi, w_lo = _split(w_ref[0])
    o_ref[0] = _dot3(a, w_hi, w_lo) + b_ref[0]


def _modulation(cvec, w_mod, b_mod):
    depth, d, d3 = w_mod.shape
    r = cvec.shape[0]
    bn = 512
    return pl.pallas_call(
        _mod_kernel,
        grid=(depth, d3 // bn),
        in_specs=[pl.BlockSpec((r, d), lambda l, j: (0, 0)),
                  pl.BlockSpec((1, d, bn), lambda l, j: (l, 0, j)),
                  pl.BlockSpec((1, 1, bn), lambda l, j: (l, 0, j))],
        out_specs=pl.BlockSpec((1, r, bn), lambda l, j: (l, 0, j)),
        out_shape=jax.ShapeDtypeStruct((depth, r, d3), F32),
        compiler_params=_cparams(("arbitrary", "arbitrary")),
        name="modulation",
    )(cvec, w_mod, b_mod.reshape(depth, 1, d3))


def _modulated_norm(x, mod_ref, nw_ref):
    ms = jnp.mean(x * x, axis=-1, keepdims=True)
    y = x * lax.rsqrt(ms + RMS_EPS) * nw_ref[...]
    return y * (1.0 + mod_ref[0, 1:2, :]) + mod_ref[0, 0:1, :]


def _lane_rinv(v, n):
    return lax.rsqrt(jnp.sum(v * v, axis=-1, keepdims=True) * (1.0 / n) + RMS_EPS)


def _head_rinv(v, gmat):
    hi, lo = _split(v * v)
    ss = _dot(hi, gmat) + _dot(lo, gmat)
    return lax.rsqrt(ss * (1.0 / HEAD_DIM) + RMS_EPS)


def _proj_kernel(x_ref, mod_ref, nw_ref, wp_ref, wuq_ref, wuqr_ref, wuk_ref, wuv_ref,
                 g_cq, g_ckv, g_mq, g_mk, g_sq, g_sk, g_aq, g_ak, gmat_ref,
                 cosm_ref, sinm_ref, cosh_ref, sinh_ref,
                 mq_o, mk_o, mv_o, sq_o, sk_o, sv_o, aq_o, ak_o, av_o, lru_o):
    x = x_ref[0]
    h = _modulated_norm(x, mod_ref, nw_ref)
    z = _dot(h.astype(BF16), wp_ref[...])
    t = x.shape[0]
    lo = lax.broadcasted_iota(jnp.int32, (t, LANES), 1) < HEAD_DIM
    cosm, sinm = cosm_ref[...], sinm_ref[...]
    cosh, sinh = cosh_ref[...], sinh_ref[...]
    gmat = gmat_ref[...]

    def pair_store(out, slab, swapped, transpose=False):
        fix = (lambda a: a.T.astype(BF16)) if transpose else (lambda a: a.astype(BF16))
        out[0, 0] = fix(jnp.where(lo, slab, 0.0))
        out[0, 1] = fix(jnp.where(lo, 0.0, swapped))
        out[0, 2] = fix(jnp.where(lo, swapped, 0.0))
        out[0, 3] = fix(jnp.where(lo, 0.0, slab))

    cq = (z[:, 0:256] * _lane_rinv(z[:, 0:256], 256.0) * g_cq[...]).astype(BF16)
    q = _dot(cq, wuq_ref[...])
    qr = _dot(cq, wuqr_ref[...])
    ckv = (z[:, 256:384] * _lane_rinv(z[:, 256:384], 128.0) * g_ckv[...]).astype(BF16)
    kn = _dot(ckv, wuk_ref[...])
    vv = _dot(ckv, wuv_ref[...])
    kr = z[:, 384:512]
    krr = z[:, 512:640]
    gc_q = g_mq[...] * cosm
    gc_k = g_mk[...] * cosm
    for hh in range(MLA_HEADS):
        sl = slice(hh * LANES, (hh + 1) * LANES)
        r = _lane_rinv(q[:, sl], float(MLA_QK))
        mq_o[0, hh] = ((q[:, sl] * r) * gc_q + (qr[:, sl] * r) * sinm).astype(BF16)
        ks = kn[:, sl] + kr
        r = _lane_rinv(ks, float(MLA_QK))
        mk_o[0, hh] = ((ks * r) * gc_k + (krr * r) * sinm).astype(BF16)
    for i in range(MLA_HEADS // 2):
        vs = vv[:, i * LANES:(i + 1) * LANES]
        mv_o[0, 2 * i] = jnp.where(lo, vs, 0.0).T.astype(BF16)
        mv_o[0, 2 * i + 1] = jnp.where(lo, 0.0, vs).T.astype(BF16)

    def gqa(c0, g_q, g_k, q_o, k_o, v_o, v_transposed):
        zq, zqr = z[:, c0:c0 + 256], z[:, c0 + 256:c0 + 512]
        r = _head_rinv(zq, gmat)
        gc = g_q[...] * cosh
        for i in range(2):
            sl = slice(i * LANES, (i + 1) * LANES)
            q_o[0, i] = ((zq[:, sl] * r[:, sl]) * gc + (zqr[:, sl] * r[:, sl]) * sinh).astype(BF16)
        zk, zkr = z[:, c0 + 512:c0 + 640], z[:, c0 + 640:c0 + 768]
        r = _head_rinv(zk, gmat[:LANES, :LANES])
        ks = (zk * r) * (g_k[...] * cosh) + (zkr * r) * sinh
        pair_store(k_o, ks, pltpu.roll(ks, HEAD_DIM, 1))
        zv = z[:, c0 + 768:c0 + 896]
        pair_store(v_o, zv, pltpu.roll(zv, HEAD_DIM, 1), v_transposed)

    gqa(640, g_sq, g_sk, sq_o, sk_o, sv_o, False)
    gqa(1536, g_aq, g_ak, aq_o, ak_o, av_o, True)
    lru_o[0] = z[:, 2432:2688]


def _project(xc, mod, lw, tabs):
    b, s, d = xc.shape
    nt = s // TILE
    mod_rows = mod.shape[0]

    def tile_spec(width):
        return pl.BlockSpec((1, TILE, width), lambda t, i: (i, t, 0))

    def head_spec(nh):
        return pl.BlockSpec((1, nh, TILE, LANES), lambda t, i: (i, 0, t, 0))

    tab_spec = pl.BlockSpec((TILE, LANES), lambda t, i: (t, 0))
    consts = [lw["nw"], lw["wp"], lw["wuq"], lw["wuqr"], lw["wuk"], lw["wuv"], lw["g_cq"], lw["g_ckv"],
              lw["g_mq"], lw["g_mk"], lw["g_sq"], lw["g_sk"], lw["g_aq"], lw["g_ak"], lw["gmat"]]
    in_specs = ([tile_spec(d),
                 pl.BlockSpec((1, 3, d), lambda t, i: (jnp.where(t == 0, mod_rows - 1, i), 0, 0))]
                + [_const_spec(c.shape) for c in consts] + [tab_spec] * 4)
    def head_spec_t(nh):
        return pl.BlockSpec((1, nh, LANES, TILE), lambda t, i: (i, 0, 0, t))

    n_slabs = [MLA_HEADS, MLA_HEADS, MLA_HEADS, 2, 4, 4, 2, 4, 4]
    transposed = [False, False, True, False, False, False, False, False, True]
    out_specs = ([head_spec_t(nh) if tr else head_spec(nh) for nh, tr in zip(n_slabs, transposed)]
                 + [tile_spec(4 * HEAD_DIM)])
    out_shape = ([jax.ShapeDtypeStruct((b, nh, LANES, s) if tr else (b, nh, s, LANES), BF16)
                  for nh, tr in zip(n_slabs, transposed)]
                 + [jax.ShapeDtypeStruct((b, s, 4 * HEAD_DIM), F32)])
    return pl.pallas_call(
        _proj_kernel,
        grid=(nt, b),
        in_specs=in_specs,
        out_specs=out_specs,
        out_shape=out_shape,
        compiler_params=_cparams(("arbitrary", "arbitrary")),
        name="project",
    )(xc, mod, *consts, *tabs)


def _dense_attn_kernel(q_ref, k_ref, v_ref, o_ref, *, q_share, ctx_tile_first):
    s_len = k_ref.shape[2]

    def attend(n_keys):
        n_groups = n_keys // LANES
        n_chunks = min(KEY_CHUNKS, n_groups)
        edges = [(c * n_groups // n_chunks) * LANES for c in range(n_chunks + 1)]
        chunks = [slice(a, b) for a, b in zip(edges[:-1], edges[1:])]

        def scores(hh):
            q = q_ref[0, hh // q_share]
            return [_dot_nt(k_ref[0, hh, sl, :], q) for sl in chunks]

        def head_out(hh, st):
            m = functools.reduce(jnp.maximum, [jnp.max(sc, axis=0, keepdims=True) for sc in st])
            pt = [jnp.exp2(sc - m) for sc in st]
            l = sum(jnp.sum(pc, axis=0, keepdims=True) for pc in pt)
            ot = sum(_dot(v_ref[0, hh, :, sl], pc.astype(BF16)) for sl, pc in zip(chunks, pt))
            return ot / l

        n_heads = k_ref.shape[1]
        st_next = scores(0)
        outs = []
        for hh in range(n_heads):
            st = st_next
            if hh + 1 < n_heads:
                st_next = scores(hh + 1)
            outs.append(head_out(hh, st))
        o_ref[0] = jnp.concatenate([(outs[0] + outs[1]).T, (outs[2] + outs[3]).T], axis=-1)

    if ctx_tile_first:
        pl.when(pl.program_id(1) == 0)(lambda: attend(TILE))
        pl.when(pl.program_id(1) > 0)(lambda: attend(s_len))
    else:
        attend(s_len)


def _dense_attention(q, k, v, with_ctx_queries):
    b, nq_slabs, s, _ = q.shape
    nh = k.shape[1]
    nt = s // TILE
    off = 0 if with_ctx_queries else 1
    nq = nt - off
    kern = functools.partial(_dense_attn_kernel, q_share=nh // nq_slabs,
                             ctx_tile_first=with_ctx_queries)
    return pl.pallas_call(
        kern,
        grid=(b, nq),
        in_specs=[pl.BlockSpec((1, nq_slabs, TILE, LANES), lambda i, t: (i, 0, t + off, 0)),
                  pl.BlockSpec((1, nh, s, LANES), lambda i, t: (i, 0, 0, 0)),
                  pl.BlockSpec((1, nh, LANES, s), lambda i, t: (i, 0, 0, 0))],
        out_specs=pl.BlockSpec((1, TILE, 2 * LANES), lambda i, t: (i, t, 0)),
        out_shape=jax.ShapeDtypeStruct((b, nq * TILE, 2 * LANES), F32),
        compiler_params=_cparams(("arbitrary", "arbitrary")),
        name="dense_attention",
    )(q, k, v)


def _window_attn_kernel(sink_ref, q_ref, k_ref, v_ref, o_ref, *, tile_off):
    s_len = k_ref.shape[2]
    n_tiles = s_len // TILE
    ti = pl.program_id(1) + tile_off
    is_lat = ti > 0
    t0 = ti * TILE
    half = TILE // 2
    left0 = pl.multiple_of(jnp.maximum(t0 - half, 0), half)
    cen0 = pl.multiple_of(t0, TILE)
    right0 = pl.multiple_of(jnp.minimum(t0 + TILE, s_len - half), half)
    r_c = lax.broadcasted_iota(jnp.int32, (TILE, TILE), 0)
    c_c = lax.broadcasted_iota(jnp.int32, (TILE, TILE), 1)
    r_s = lax.broadcasted_iota(jnp.int32, (TILE, half), 0)
    c_s = lax.broadcasted_iota(jnp.int32, (TILE, half), 1)
    ok_cen = (jnp.abs(r_c - c_c) <= WINDOW) & is_lat
    ok_left = ((r_s - c_s + half) <= WINDOW) & (ti > 1)
    ok_right = ((c_s - r_s + TILE) <= WINDOW) & is_lat & (ti < n_tiles - 1)
    slabs = []
    for i in range(2):
        acc = None
        for hh in (2 * i, 2 * i + 1):
            q = q_ref[0, i]
            sink = sink_ref[hh] * LOG2E

            def kv(start, size, hh=hh):
                return k_ref[0, hh, pl.ds(start, size), :], v_ref[0, hh, pl.ds(start, size), :]

            k_x, v_x = kv(0, TILE)
            k_l, v_l = kv(left0, half)
            k_c, v_c = kv(cen0, TILE)
            k_r, v_r = kv(right0, half)
            s_x = _dot_nt(q, k_x)
            s_l = jnp.where(ok_left, _dot_nt(q, k_l), NEG_INF)
            s_c = jnp.where(ok_cen, _dot_nt(q, k_c), NEG_INF)
            s_r = jnp.where(ok_right, _dot_nt(q, k_r), NEG_INF)
            m = jnp.maximum(jnp.maximum(jnp.max(s_x, -1, keepdims=True), jnp.max(s_c, -1, keepdims=True)),
                            jnp.maximum(jnp.max(s_l, -1, keepdims=True), jnp.max(s_r, -1, keepdims=True)))
            m = jnp.maximum(m, sink)
            p_x, p_l, p_c, p_r = (jnp.exp2(s_x - m), jnp.exp2(s_l - m), jnp.exp2(s_c - m), jnp.exp2(s_r - m))
            den = (jnp.sum(p_x, -1, keepdims=True) + jnp.sum(p_l, -1, keepdims=True)
                   + jnp.sum(p_c, -1, keepdims=True) + jnp.sum(p_r, -1, keepdims=True) + jnp.exp2(sink - m))
            num = (_dot(p_x.astype(BF16), v_x) + _dot(p_l.astype(BF16), v_l)
                   + _dot(p_c.astype(BF16), v_c) + _dot(p_r.astype(BF16), v_r))
            o = num / den
            acc = o if acc is None else acc + o
        slabs.append(acc)
    o_ref[0] = jnp.concatenate(slabs, axis=-1)


def _window_attention(sink, q, k, v, with_ctx_queries):
    b, nq_slabs, s, _ = q.shape
    nh = k.shape[1]
    nt = s // TILE
    off = 0 if with_ctx_queries else 1
    nq = nt - off
    kern = functools.partial(_window_attn_kernel, tile_off=off)
    return pl.pallas_call(
        kern,
        grid=(b, nq),
        in_specs=[pl.BlockSpec(memory_space=pltpu.SMEM),
                  pl.BlockSpec((1, nq_slabs, TILE, LANES), lambda i, t: (i, 0, t + off, 0)),
                  pl.BlockSpec((1, nh, s, LANES), lambda i, t: (i, 0, 0, 0)),
                  pl.BlockSpec((1, nh, s, LANES), lambda i, t: (i, 0, 0, 0))],
        out_specs=pl.BlockSpec((1, TILE, 2 * LANES), lambda i, t: (i, t, 0)),
        out_shape=jax.ShapeDtypeStruct((b, nq * TILE, 2 * LANES), F32),
        compiler_params=_cparams(("arbitrary", "arbitrary")),
        name="window_attention",
    )(sink, q, k, v)


def _lru_conv(prev_ref, cur_ref, next_ref, prev_ok, next_ok, cw_ref, cb_ref):
    t = cur_ref.shape[1]
    prev = jnp.where(prev_ok, prev_ref[0], 0.0)
    nxt = jnp.where(next_ok, next_ref[0], 0.0)
    ext = jnp.concatenate([prev, cur_ref[0], nxt], axis=0)
    n = t + 2 * SUBLANES
    u = cb_ref[...] + cur_ref[0] * cw_ref[CONV_LEFT:CONV_LEFT + 1, :]
    for j in range(CONV_W):
        off = j - CONV_LEFT
        if off == 0:
            continue
        sh = pltpu.roll(ext, (-off) % n, 0)[SUBLANES:SUBLANES + t]
        u = u + sh * cw_ref[j:j + 1, :]
    return u


def _lru_scan_tile(u, d, wr_hi, wr_lo, wi_hi, wi_lo, br_ref, bi_ref, lam_ref, h0, reverse):
    t = u.shape[0]
    r = _sigmoid(_dot3(u, wr_hi[d], wr_lo[d]) + br_ref[d:d + 1, :])
    i = _sigmoid(_dot3(u, wi_hi[d], wi_lo[d]) + bi_ref[d:d + 1, :])
    lam = lam_ref[d:d + 1, :]
    softplus_neg = jnp.maximum(-lam, 0.0) + jnp.log1p(jnp.exp(-jnp.abs(lam)))
    log_a = -LRU_C * r * softplus_neg
    a = jnp.exp(log_a)
    bx = jnp.sqrt(1.0 - a * a) * (i * u)
    row = lax.broadcasted_iota(jnp.int32, u.shape, 0)
    step = 1
    while step < t:
        if reverse:
            ok = row < t - step
            a_sh = jnp.where(ok, pltpu.roll(a, t - step, 0), 1.0)
            b_sh = jnp.where(ok, pltpu.roll(bx, t - step, 0), 0.0)
        else:
            ok = row >= step
            a_sh = jnp.where(ok, pltpu.roll(a, step, 0), 1.0)
            b_sh = jnp.where(ok, pltpu.roll(bx, step, 0), 0.0)
        bx = a * b_sh + bx
        a = a * a_sh
        step *= 2
    h = a * h0 + bx
    h_end = h[0:1, :] if reverse else h[t - 1:t, :]
    return h, h_end


def _lru_kernel(fp_ref, fc_ref, fn_ref, rp_ref, rc_ref, rn_ref, cw_ref, cb_ref,
                wr_hi, wr_lo, wi_hi, wi_lo, br_ref, bi_ref, lam_ref,
                yf_ref, yb_ref, hf_s, hb_s):
    j = pl.program_id(1)
    nt = pl.num_programs(1)

    @pl.when(j == 0)
    def _():
        hf_s[...] = jnp.zeros_like(hf_s)
        hb_s[...] = jnp.zeros_like(hb_s)

    def bounds(ti):
        return ti >= 2, (ti >= 1) & (ti < nt - 1)

    tf = j
    tr = jnp.where(j == 0, 0, nt - j)
    p_ok, n_ok = bounds(tf)
    uf = _lru_conv(fp_ref, fc_ref, fn_ref, p_ok, n_ok, cw_ref, cb_ref)
    h, h_end = _lru_scan_tile(uf, 0, wr_hi, wr_lo, wi_hi, wi_lo, br_ref, bi_ref, lam_ref,
                              hf_s[...], False)
    yf_ref[0] = h
    hf_s[...] = h_end
    p_ok, n_ok = bounds(tr)
    ur = _lru_conv(rp_ref, rc_ref, rn_ref, p_ok, n_ok, cw_ref, cb_ref)
    h, h_end = _lru_scan_tile(ur, 1, wr_hi, wr_lo, wi_hi, wi_lo, br_ref, bi_ref, lam_ref,
                              hb_s[...], True)
    yb_ref[0] = h
    hb_s[...] = h_end


def _rglru(z, lw):
    b, s, c = z.shape
    nt = s // TILE
    r8 = TILE // SUBLANES
    n8 = s // SUBLANES

    def rev(j):
        return jnp.where(j == 0, 0, nt - j)

    def cur(f):
        return pl.BlockSpec((1, TILE, c), lambda i, j: (i, f(j), 0))

    def prev(f):
        return pl.BlockSpec((1, SUBLANES, c), lambda i, j: (i, jnp.maximum(f(j) * r8 - 1, 0), 0))

    def nxt(f):
        return pl.BlockSpec((1, SUBLANES, c), lambda i, j: (i, jnp.minimum((f(j) + 1) * r8, n8 - 1), 0))

    ident = lambda j: j
    consts = [lw["conv_w"], lw["conv_b"], lw["wr_hi"], lw["wr_lo"], lw["wi_hi"], lw["wi_lo"],
              lw["b_r"], lw["b_i"], lw["lam"]]
    return pl.pallas_call(
        _lru_kernel,
        grid=(b, nt),
        in_specs=[prev(ident), cur(ident), nxt(ident), prev(rev), cur(rev), nxt(rev)]
                 + [_const_spec(x.shape) for x in consts],
        out_specs=[cur(ident), cur(rev)],
        out_shape=[jax.ShapeDtypeStruct((b, s, c), F32)] * 2,
        scratch_shapes=[pltpu.VMEM((1, c), F32), pltpu.VMEM((1, c), F32)],
        compiler_params=_cparams(("arbitrary", "arbitrary")),
        name="rglru",
    )(z, z, z, z, z, z, *consts)


def _merge_kernel(x_ref, mod_ref, nw_ref, wg_ref, wm_ref, ya_ref, yb_ref, yc_ref, yf_ref, yr_ref,
                  wb_ref, wo_ref, o_ref):
    x = x_ref[0]
    hb = _modulated_norm(x, mod_ref, nw_ref).astype(BF16)
    zg = _dot(hb, wg_ref[...])
    d = x.shape[1]
    bw = ya_ref.shape[2]
    branches = (ya_ref[0], yb_ref[0], yc_ref[0], yf_ref[0] + yr_ref[0])
    mix = jnp.zeros_like(x)
    for k in range(N_BRANCH):
        g = zg[:, k * bw:(k + 1) * bw]
        y = branches[k] * (g * _sigmoid(g))
        proj = _dot(y.astype(BF16), wb_ref[k])
        zm = _dot(hb, wm_ref[:, k * d:(k + 1) * d])
        mix = mix + _sigmoid(zm) * proj
    o_ref[0] = x + mod_ref[0, 2:3, :] * _dot(mix.astype(BF16), wo_ref[...])


def _merge(xc, mod, lw, ya, yb, yc, yf, yr, with_ctx):
    b, s, d = xc.shape
    nt = s // TILE
    off = 0 if with_ctx else 1
    nq = nt - off
    mod_rows = mod.shape[0]
    bw = ya.shape[2]

    def stream(width):
        return pl.BlockSpec((1, TILE, width), lambda i, t: (i, t + off, 0))

    def local(width):
        return pl.BlockSpec((1, TILE, width), lambda i, t: (i, t, 0))

    consts_a = [lw["nw"], lw["wg"], lw["wm"]]
    consts_b = [lw["wb"], lw["wo"]]
    return pl.pallas_call(
        _merge_kernel,
        grid=(b, nq),
        in_specs=[stream(d),
                  pl.BlockSpec((1, 3, d), lambda i, t: (jnp.where(t + off == 0, mod_rows - 1, i), 0, 0))]
                 + [_const_spec(c.shape) for c in consts_a]
                 + [local(bw), local(bw), local(bw), stream(bw), stream(bw)]
                 + [_const_spec(c.shape) for c in consts_b],
        out_specs=local(d),
        out_shape=jax.ShapeDtypeStruct((b, nq * TILE, d), F32),
        compiler_params=_cparams(("arbitrary", "arbitrary")),
        name="merge",
    )(xc, mod, *consts_a, ya, yb, yc, yf, yr, *consts_b)


def _rope_tables(n, n_ctx):
    def axial(rot_dim):
        n_rows = n // GRID_W
        rows = jnp.repeat(jnp.arange(n_rows, dtype=F32), GRID_W)
        cols = jnp.tile(jnp.arange(GRID_W, dtype=F32), n_rows)
        quarter = rot_dim // 4
        freqs = ROPE_THETA ** (-jnp.arange(quarter, dtype=F32) / quarter)
        ang = jnp.concatenate([rows[:, None] * freqs, cols[:, None] * freqs], axis=-1)
        return jnp.cos(ang), jnp.sin(ang)

    def with_ctx(cos, sin):
        return (jnp.concatenate([jnp.ones((n_ctx, LANES), F32), cos], axis=0),
                jnp.concatenate([jnp.zeros((n_ctx, LANES), F32), sin], axis=0))

    cm, sm = axial(MLA_ROPE)
    one = jnp.ones((n, MLA_NOPE), F32)
    zero = jnp.zeros((n, MLA_NOPE), F32)
    pad1 = jnp.ones((n, LANES - MLA_QK), F32)
    pad0 = jnp.zeros((n, LANES - MLA_QK), F32)
    cosm, sinm = with_ctx(jnp.concatenate([one, cm, cm, pad1], axis=-1),
                          jnp.concatenate([zero, sm, sm, pad0], axis=-1))
    ch, sh = axial(HEAD_DIM)
    cosh, sinh = with_ctx(jnp.tile(ch, (1, 4)), jnp.tile(sh, (1, 4)))
    return cosm, sinm, cosh, sinh


def _block_diag(w):
    k, m, _ = w.shape
    out = jnp.zeros((k * m, k * m), w.dtype)
    for i in range(k):
        out = out.at[i * m:(i + 1) * m, i * m:(i + 1) * m].set(w[i])
    return out


def _rot_fold(w, gain, n_heads, half):
    k = w.shape[0]
    wg = (w * gain).reshape(k, n_heads, 2, half)
    return jnp.concatenate([-wg[:, :, 1], wg[:, :, 0]], axis=2).reshape(k, n_heads * 2 * half)


def _layer_weights(l, p):
    d = p["w_in"].shape[1]
    w_in = p["w_in"][l]
    sizes = (256, 128, 32, 256, 128, 128, 256, 128, 128, 256, 1024, 4 * d)
    offs = [0]
    for sz in sizes:
        offs.append(offs[-1] + sz)
    cq, ckv, kr, sq, sk, sv, aq, ak, av, lru, gate, merge = (
        w_in[:, offs[i]:offs[i + 1]] for i in range(len(sizes)))

    q_scale = HEAD_DIM ** -0.5 * LOG2E
    mq_scale = MLA_QK ** -0.5 * LOG2E
    g_sq = jnp.tile(p["swa_q_norm"][l] * q_scale, 2)
    g_sk = jnp.tile(p["swa_k_norm"][l], 2)
    g_aq = jnp.tile(p["axa_q_norm"][l] * q_scale, 2)
    g_ak = jnp.tile(p["axa_k_norm"][l], 2)
    g_mq = p["mla_q_norm"][l] * mq_scale
    g_mk = p["mla_k_norm"][l]

    def rope_slab(w_rope, g_rope):
        rot = _rot_fold(w_rope.reshape(w_rope.shape[0], -1), jnp.tile(g_rope, w_rope.shape[1]),
                        w_rope.shape[1], MLA_ROPE // 2).reshape(w_rope.shape)
        return jnp.pad(rot, ((0, 0), (0, 0), (MLA_NOPE, LANES - MLA_QK)))

    kr_slab = jnp.pad(kr, ((0, 0), (MLA_NOPE, LANES - MLA_QK)))
    krr_slab = rope_slab(kr[:, None, :], g_mk[MLA_NOPE:])[:, 0]
    half = HEAD_DIM // 2
    wp = jnp.concatenate([
        cq, ckv, kr_slab, krr_slab,
        sq, _rot_fold(sq, jnp.tile(g_sq, 2), SWA_Q_HEADS, half),
        sk, _rot_fold(sk, g_sk, SWA_KV_HEADS, half), sv,
        aq, _rot_fold(aq, jnp.tile(g_aq, 2), AXA_Q_HEADS, half),
        ak, _rot_fold(ak, g_ak, AXA_KV_HEADS, half), av,
        lru], axis=1).astype(BF16)

    w_uq = p["mla_w_uq"][l].reshape(-1, MLA_HEADS, MLA_QK)
    w_ukv = p["mla_w_ukv"][l].reshape(-1, MLA_HEADS, MLA_NOPE + MLA_V)
    wuq = jnp.pad(w_uq, ((0, 0), (0, 0), (0, LANES - MLA_QK))).reshape(-1, MLA_HEADS * LANES).astype(BF16)
    wuqr = rope_slab(w_uq[:, :, MLA_NOPE:], g_mq[MLA_NOPE:]).reshape(-1, MLA_HEADS * LANES).astype(BF16)
    wuk = jnp.pad(w_ukv[:, :, :MLA_NOPE], ((0, 0), (0, 0), (0, LANES - MLA_NOPE)))
    wuk = wuk.reshape(-1, MLA_HEADS * LANES).astype(BF16)
    wuv = w_ukv[:, :, MLA_NOPE:].reshape(-1, MLA_HEADS * MLA_V).astype(BF16)

    def slab_gain(g):
        return jnp.pad(g, (0, LANES - MLA_QK)).reshape(1, LANES)

    idx = jnp.arange(4 * HEAD_DIM) // HEAD_DIM
    gmat = (idx[:, None] == idx[None, :]).astype(BF16)

    def hi_lo(w):
        hi = w.astype(BF16)
        return hi, (w - hi.astype(F32)).astype(BF16)

    wr = jnp.stack([_block_diag(p["lru_w_r"][l, dd]) for dd in range(2)])
    wi = jnp.stack([_block_diag(p["lru_w_i"][l, dd]) for dd in range(2)])
    wr_hi, wr_lo = hi_lo(wr)
    wi_hi, wi_lo = hi_lo(wi)
    return {
        "nw": p["norm_w"][l].reshape(1, d),
        "wp": wp, "wuq": wuq, "wuqr": wuqr, "wuk": wuk, "wuv": wuv,
        "g_cq": p["mla_cq_norm"][l].reshape(1, -1),
        "g_ckv": p["mla_ckv_norm"][l].reshape(1, -1),
        "g_mq": slab_gain(g_mq), "g_mk": slab_gain(g_mk),
        "g_sq": g_sq.reshape(1, LANES), "g_sk": g_sk.reshape(1, LANES),
        "g_aq": g_aq.reshape(1, LANES), "g_ak": g_ak.reshape(1, LANES),
        "gmat": gmat,
        "wg": gate.astype(BF16), "wm": merge.astype(BF16),
        "wb": p["w_branch"][l].astype(BF16), "wo": p["w_out"][l].astype(BF16),
        "conv_w": p["lru_conv_w"][l], "conv_b": p["lru_conv_b"][l].reshape(1, -1),
        "wr_hi": wr_hi, "wr_lo": wr_lo, "wi_hi": wi_hi, "wi_lo": wi_lo,
        "b_r": p["lru_b_r"][l], "b_i": p["lru_b_i"][l], "lam": p["lru_lambda"][l],
        "sink": p["swa_sink"][l],
    }


def kernel(x, c, ctx, c_ctx, w_mod, b_mod, norm_w, w_in, mla_cq_norm, mla_ckv_norm, mla_w_uq, mla_w_ukv, mla_q_norm, mla_k_norm, swa_q_norm, swa_k_norm, swa_sink, axa_q_norm, axa_k_norm, lru_conv_w, lru_conv_b, lru_w_r, lru_b_r, lru_w_i, lru_b_i, lru_lambda, w_branch, w_out):
    p = dict(norm_w=norm_w, w_in=w_in, mla_cq_norm=mla_cq_norm, mla_ckv_norm=mla_ckv_norm,
             mla_w_uq=mla_w_uq, mla_w_ukv=mla_w_ukv, mla_q_norm=mla_q_norm, mla_k_norm=mla_k_norm,
             swa_q_norm=swa_q_norm, swa_k_norm=swa_k_norm, swa_sink=swa_sink,
             axa_q_norm=axa_q_norm, axa_k_norm=axa_k_norm, lru_conv_w=lru_conv_w,
             lru_conv_b=lru_conv_b, lru_w_r=lru_w_r, lru_b_r=lru_b_r, lru_w_i=lru_w_i,
             lru_b_i=lru_b_i, lru_lambda=lru_lambda, w_branch=w_branch, w_out=w_out)
    b, n, d = x.shape
    n_ctx = ctx.shape[1]
    depth = w_mod.shape[0]
    assert n_ctx == TILE and n % TILE == 0 and n % GRID_W == 0

    rows = jnp.concatenate([c, c_ctx[None, :]], axis=0)
    n_rows = -(-rows.shape[0] // SUBLANES) * SUBLANES
    rows = jnp.pad(rows, ((0, n_rows - rows.shape[0]), (0, 0)))
    mod_all = _modulation(rows, w_mod, b_mod)[:, :b + 1].reshape(depth, b + 1, 3, d)

    tabs = _rope_tables(n, n_ctx)
    xc = jnp.concatenate([ctx, x], axis=1)
    for l in range(depth):
        lw = _layer_weights(l, p)
        mod = mod_all[l]
        upd = l < depth - 1
        mq, mk, mv, sq, sk, sv, aq, ak, av, zl = _project(xc, mod, lw, tabs)
        ya = _dense_attention(mq, mk, mv, upd)
        yb = _window_attention(lw["sink"], sq, sk, sv, upd)
        yc = _dense_attention(aq, ak, av, upd)
        yf, yr = _rglru(zl, lw)
        xc = _merge(xc, mod, lw, ya, yb, yc, yf, yr, upd)
    return xc
```

```python
import functools

import numpy as np
import jax
import jax.numpy as jnp
from jax import lax
from jax.experimental import pallas as pl
from jax.experimental.pallas import tpu as pltpu

GRID_W = 64
N_BRANCH = 4
HEAD_DIM = 64
MLA_HEADS = 4
MLA_NOPE = 64
MLA_ROPE = 32
MLA_V = 64
MLA_QK = MLA_NOPE + MLA_ROPE
SWA_Q_HEADS = 4
SWA_KV_HEADS = 2
WINDOW = 128
AXA_Q_HEADS = 4
AXA_KV_HEADS = 2
LRU_BLOCKS = 4
LRU_C = 8.0
CONV_W = 4
CONV_LEFT = 2
ROPE_THETA = 10000.0
RMS_EPS = 1e-6
NEG_INF = -1e30
LOG2E = 1.4426950408889634

LANES = 128
SUBLANES = 8
TILE = 256
KEY_CHUNKS = 4
VMEM_LIMIT = 56 * 1024 * 1024

F32 = jnp.float32
BF16 = jnp.bfloat16

VEC_NW, VEC_GCQ, VEC_GAINS, VEC_CONVW, VEC_CONVB, VEC_BR, VEC_BI, VEC_LAM, VEC_ROWS = 0, 1, 2, 3, 7, 8, 10, 12, 16
GAIN_CKV, GAIN_MQ, GAIN_MK, GAIN_SQ, GAIN_SK, GAIN_AQ, GAIN_AK = range(7)


def _dot(a, b):
    return jnp.dot(a, b, preferred_element_type=F32)


def _dot_nt(a, b):
    return lax.dot_general(a, b, (((1,), (1,)), ((), ())), preferred_element_type=F32)


def _split(a):
    hi = a.astype(BF16)
    lo = (a - hi.astype(F32)).astype(BF16)
    return hi, lo


def _dot3(a, w_hi, w_lo):
    a_hi, a_lo = _split(a)
    return _dot(a_hi, w_hi) + _dot(a_lo, w_hi) + _dot(a_hi, w_lo)


def _sigmoid(v):
    return 1.0 / (1.0 + jnp.exp(-v))


def _cparams(sem):
    return pltpu.CompilerParams(dimension_semantics=sem, vmem_limit_bytes=VMEM_LIMIT)


def _layer_spec(arr, layer):
    nd = arr.ndim - 1
    return pl.BlockSpec((1,) + arr.shape[1:], lambda *_: (layer,) + (0,) * nd)


def _gain(vec_ref, slot):
    return vec_ref[0, VEC_GAINS:VEC_GAINS + 1, slot * LANES:(slot + 1) * LANES]


def _mod_kernel(c_ref, w_ref, b_ref, o_ref):
    v = c_ref[...]
    a = v * _sigmoid(v)
    w_hi, w_lo = _split(w_ref[0])
    o_ref[0] = _dot3(a, w_hi, w_lo) + b_ref[0]


def _modulation(cvec, w_mod, b_mod):
    depth, d, d3 = w_mod.shape
    r = cvec.shape[0]
    bn = 512
    return pl.pallas_call(
        _mod_kernel,
        grid=(depth, d3 // bn),
        in_specs=[pl.BlockSpec((r, d), lambda l, j: (0, 0)),
                  pl.BlockSpec((1, d, bn), lambda l, j: (l, 0, j)),
                  pl.BlockSpec((1, 1, bn), lambda l, j: (l, 0, j))],
        out_specs=pl.BlockSpec((1, r, bn), lambda l, j: (l, 0, j)),
        out_shape=jax.ShapeDtypeStruct((depth, r, d3), F32),
        compiler_params=_cparams(("arbitrary", "arbitrary")),
        name="modulation",
    )(cvec, w_mod, b_mod.reshape(depth, 1, d3))


def _modulated_norm(x, mod_ref, vec_ref):
    ms = jnp.mean(x * x, axis=-1, keepdims=True)
    y = x * lax.rsqrt(ms + RMS_EPS) * vec_ref[0, VEC_NW:VEC_NW + 1, :]
    return y * (1.0 + mod_ref[0, 0, 1:2, :]) + mod_ref[0, 0, 0:1, :]


def _stream_tile(x_ref, ctx_ref, is_ctx):
    if ctx_ref is None:
        return x_ref[0]
    return jnp.where(is_ctx, ctx_ref[0], x_ref[0])


def _lane_rinv(v, n):
    return lax.rsqrt(jnp.sum(v * v, axis=-1, keepdims=True) * (1.0 / n) + RMS_EPS)


def _head_rinv(v, gmat):
    hi, lo = _split(v * v)
    ss = _dot(hi, gmat) + _dot(lo, gmat)
    return lax.rsqrt(ss * (1.0 / HEAD_DIM) + RMS_EPS)


def _proj_kernel(*refs, split_ctx):
    x_ref, ctx_ref = (refs[0], refs[1]) if split_ctx else (refs[0], None)
    (mod_ref, vec_ref, wp_ref, wuq_ref, wuqr_ref, wuk_ref, wuv_ref, gmat_ref,
     cosm_ref, sinm_ref, cosh_ref, sinh_ref,
     mq_o, mk_o, mv_o, sq_o, sk_o, sv_o, aq_o, ak_o, av_o, lru_o) = refs[2 if split_ctx else 1:]
    x = _stream_tile(x_ref, ctx_ref, pl.program_id(0) == 0)
    h = _modulated_norm(x, mod_ref, vec_ref)
    z = _dot(h.astype(BF16), wp_ref[0])
    t = x.shape[0]
    lo = lax.broadcasted_iota(jnp.int32, (t, LANES), 1) < HEAD_DIM
    cosm, sinm = cosm_ref[...], sinm_ref[...]
    cosh, sinh = cosh_ref[...], sinh_ref[...]
    gmat = gmat_ref[...]

    def pair_store(out, slab):
        swapped = pltpu.roll(slab, HEAD_DIM, 1)
        out[0, 0] = jnp.where(lo, slab, 0.0).astype(BF16)
        out[0, 1] = jnp.where(lo, 0.0, swapped).astype(BF16)
        out[0, 2] = jnp.where(lo, swapped, 0.0).astype(BF16)
        out[0, 3] = jnp.where(lo, 0.0, slab).astype(BF16)

    g_cq = vec_ref[0, VEC_GCQ:VEC_GCQ + 1, 0:256]
    cq = (z[:, 0:256] * _lane_rinv(z[:, 0:256], 256.0) * g_cq).astype(BF16)
    q = _dot(cq, wuq_ref[0])
    qr = _dot(cq, wuqr_ref[0])
    ckv = (z[:, 256:384] * _lane_rinv(z[:, 256:384], 128.0) * _gain(vec_ref, GAIN_CKV)).astype(BF16)
    kn = _dot(ckv, wuk_ref[0])
    vv = _dot(ckv, wuv_ref[0])
    kr = z[:, 384:512]
    krr = z[:, 512:640]
    gc_q = _gain(vec_ref, GAIN_MQ) * cosm
    gc_k = _gain(vec_ref, GAIN_MK) * cosm
    for hh in range(MLA_HEADS):
        sl = slice(hh * LANES, (hh + 1) * LANES)
        r = _lane_rinv(q[:, sl], float(MLA_QK))
        mq_o[0, hh] = ((q[:, sl] * r) * gc_q + (qr[:, sl] * r) * sinm).astype(BF16)
        ks = kn[:, sl] + kr
        r = _lane_rinv(ks, float(MLA_QK))
        mk_o[0, hh] = ((ks * r) * gc_k + (krr * r) * sinm).astype(BF16)
    mv_o[0] = vv.T.astype(BF16)

    def gqa(c0, g_q, g_k, q_o, k_o, v_o, v_transposed):
        zq, zqr = z[:, c0:c0 + 256], z[:, c0 + 256:c0 + 512]
        r = _head_rinv(zq, gmat)
        gc = _gain(vec_ref, g_q) * cosh
        for i in range(2):
            sl = slice(i * LANES, (i + 1) * LANES)
            q_o[0, i] = ((zq[:, sl] * r[:, sl]) * gc + (zqr[:, sl] * r[:, sl]) * sinh).astype(BF16)
        zk, zkr = z[:, c0 + 512:c0 + 640], z[:, c0 + 640:c0 + 768]
        r = _head_rinv(zk, gmat[:LANES, :LANES])
        ks = (zk * r) * (_gain(vec_ref, g_k) * cosh) + (zkr * r) * sinh
        pair_store(k_o, ks)
        zv = z[:, c0 + 768:c0 + 896]
        if v_transposed:
            v_o[0] = zv.T.astype(BF16)
        else:
            pair_store(v_o, zv)

    gqa(640, GAIN_SQ, GAIN_SK, sq_o, sk_o, sv_o, False)
    gqa(1536, GAIN_AQ, GAIN_AK, aq_o, ak_o, av_o, True)
    lru_o[0] = z[:, 2432:2688]


def _project(x, ctx, mod_all, w, tabs, layer):
    split_ctx = ctx is not None
    b, d = x.shape[0], x.shape[2]
    s = x.shape[1] + (ctx.shape[1] if split_ctx else 0)
    nt = s // TILE
    ctx_row = b

    def tile_spec(width):
        return pl.BlockSpec((1, TILE, width), lambda t, i: (i, t, 0))

    def head_spec(nh):
        return pl.BlockSpec((1, nh, TILE, LANES), lambda t, i: (i, 0, t, 0))

    def head_spec_t(nh):
        return pl.BlockSpec((1, nh * HEAD_DIM, TILE), lambda t, i: (i, 0, t))

    if split_ctx:
        streams = [x, ctx]
        stream_specs = [pl.BlockSpec((1, TILE, d), lambda t, i: (i, jnp.maximum(t - 1, 0), 0)),
                        pl.BlockSpec((1, TILE, d), lambda t, i: (jnp.where(t == 0, i, b - 1), 0, 0))]
    else:
        streams, stream_specs = [x], [tile_spec(d)]
    tab_spec = pl.BlockSpec((TILE, LANES), lambda t, i: (t, 0))
    consts = [w["vecs"], w["wp"], w["wuq"], w["wuqr"], w["wuk"], w["wuv"]]
    in_specs = (stream_specs
                + [pl.BlockSpec((1, 1, 3, d), lambda t, i: (layer, jnp.where(t == 0, ctx_row, i), 0, 0))]
                + [_layer_spec(c, layer) for c in consts]
                + [pl.BlockSpec(w["gmat"].shape, lambda t, i: (0, 0))] + [tab_spec] * 4)
    n_slabs = [MLA_HEADS, MLA_HEADS, MLA_HEADS, 2, 4, 4, 2, 4, AXA_KV_HEADS]
    transposed = [False, False, True, False, False, False, False, False, True]
    out_specs = ([head_spec_t(nh) if tr else head_spec(nh) for nh, tr in zip(n_slabs, transposed)]
                 + [tile_spec(4 * HEAD_DIM)])
    out_shape = ([jax.ShapeDtypeStruct((b, nh * HEAD_DIM, s) if tr else (b, nh, s, LANES), BF16)
                  for nh, tr in zip(n_slabs, transposed)]
                 + [jax.ShapeDtypeStruct((b, s, 4 * HEAD_DIM), F32)])
    return pl.pallas_call(
        functools.partial(_proj_kernel, split_ctx=split_ctx),
        grid=(nt, b),
        in_specs=in_specs,
        out_specs=out_specs,
        out_shape=out_shape,
        compiler_params=_cparams(("arbitrary", "arbitrary")),
        name="project",
    )(*streams, mod_all, *consts, w["gmat"], *tabs)


def _dense_attn_kernel(*refs, q_share, ctx_tile_first):
    q_refs, (k_ref, v_ref, o_ref) = refs[:-3], refs[-3:]
    s_len = k_ref.shape[2]
    n_heads = k_ref.shape[1]
    v_share = n_heads * HEAD_DIM // v_ref.shape[1]

    def attend(n_keys):
        n_groups = n_keys // LANES
        n_chunks = min(KEY_CHUNKS, n_groups)
        edges = [(c * n_groups // n_chunks) * LANES for c in range(n_chunks + 1)]
        chunks = [slice(a, b) for a, b in zip(edges[:-1], edges[1:])]

        def scores(unit):
            q_ref, hh = unit
            q = q_ref[0, hh // q_share]
            return [_dot_nt(k_ref[0, hh, sl, :], q) for sl in chunks]

        def head_out(unit, st):
            hh = unit[1]
            m = functools.reduce(jnp.maximum, [jnp.max(sc, axis=0, keepdims=True) for sc in st])
            pt = [jnp.exp2(sc - m) for sc in st]
            l = sum(jnp.sum(pc, axis=0, keepdims=True) for pc in pt)
            band = slice(hh // v_share * HEAD_DIM, (hh // v_share + 1) * HEAD_DIM)
            ot = sum(_dot(v_ref[0, band, sl], pc.astype(BF16)) for sl, pc in zip(chunks, pt))
            return ot / l

        units = [(q_ref, hh) for q_ref in q_refs for hh in range(n_heads)]
        st_next = scores(units[0])
        outs = []
        for u, unit in enumerate(units):
            st = st_next
            if u + 1 < len(units):
                st_next = scores(units[u + 1])
            outs.append(head_out(unit, st))
        for j in range(len(q_refs)):
            o = outs[j * n_heads:(j + 1) * n_heads]
            o_ref[0, j * TILE:(j + 1) * TILE, :] = jnp.concatenate(o, axis=0).T

    if ctx_tile_first:
        pl.when(pl.program_id(1) == 0)(lambda: attend(TILE))
        pl.when(pl.program_id(1) > 0)(lambda: attend(s_len))
    else:
        attend(s_len)


def _dense_attention(q, k, v, with_ctx_queries):
    b, nq_slabs, s, _ = q.shape
    nh = k.shape[1]
    nt = s // TILE
    off = 0 if with_ctx_queries else 1
    tps = 1 if with_ctx_queries or (nt - off) % 2 else 2
    steps = (nt - off) // tps
    kern = functools.partial(_dense_attn_kernel, q_share=nh // nq_slabs,
                             ctx_tile_first=with_ctx_queries)

    def q_spec(j):
        return pl.BlockSpec((1, nq_slabs, TILE, LANES), lambda i, t: (i, 0, tps * t + j + off, 0))

    return pl.pallas_call(
        kern,
        grid=(b, steps),
        in_specs=[q_spec(j) for j in range(tps)]
                 + [pl.BlockSpec((1, nh, s, LANES), lambda i, t: (i, 0, 0, 0)),
                    pl.BlockSpec((1,) + v.shape[1:], lambda i, t: (i, 0, 0))],
        out_specs=pl.BlockSpec((1, tps * TILE, 2 * LANES), lambda i, t: (i, t, 0)),
        out_shape=jax.ShapeDtypeStruct((b, steps * tps * TILE, 2 * LANES), F32),
        compiler_params=_cparams(("arbitrary", "arbitrary")),
        name="dense_attention",
    )(*([q] * tps), k, v)


def _window_attn_kernel(sink_ref, q_ref, k_ref, v_ref, o_ref, *, tile_off, layer):
    s_len = k_ref.shape[2]
    n_tiles = s_len // TILE
    ti = pl.program_id(1) + tile_off
    is_lat = ti > 0
    t0 = ti * TILE
    half = TILE // 2
    left0 = pl.multiple_of(jnp.maximum(t0 - half, 0), half)
    cen0 = pl.multiple_of(t0, TILE)
    right0 = pl.multiple_of(jnp.minimum(t0 + TILE, s_len - half), half)
    r_c = lax.broadcasted_iota(jnp.int32, (TILE, TILE), 0)
    c_c = lax.broadcasted_iota(jnp.int32, (TILE, TILE), 1)
    r_s = lax.broadcasted_iota(jnp.int32, (TILE, half), 0)
    c_s = lax.broadcasted_iota(jnp.int32, (TILE, half), 1)
    ok_cen = (jnp.abs(r_c - c_c) <= WINDOW) & is_lat
    ok_left = ((r_s - c_s + half) <= WINDOW) & (ti > 1)
    ok_right = ((c_s - r_s + TILE) <= WINDOW) & is_lat & (ti < n_tiles - 1)
    slabs = []
    for i in range(2):
        acc = None
        for hh in (2 * i, 2 * i + 1):
            q = q_ref[0, i]
            sink = sink_ref[layer, hh] * LOG2E

            def kv(start, size, hh=hh):
                return k_ref[0, hh, pl.ds(start, size), :], v_ref[0, hh, pl.ds(start, size), :]

            k_x, v_x = kv(0, TILE)
            k_l, v_l = kv(left0, half)
            k_c, v_c = kv(cen0, TILE)
            k_r, v_r = kv(right0, half)
            s_x = _dot_nt(q, k_x)
            s_l = jnp.where(ok_left, _dot_nt(q, k_l), NEG_INF)
            s_c = jnp.where(ok_cen, _dot_nt(q, k_c), NEG_INF)
            s_r = jnp.where(ok_right, _dot_nt(q, k_r), NEG_INF)
            m = jnp.maximum(jnp.maximum(jnp.max(s_x, -1, keepdims=True), jnp.max(s_c, -1, keepdims=True)),
                            jnp.maximum(jnp.max(s_l, -1, keepdims=True), jnp.max(s_r, -1, keepdims=True)))
            m = jnp.maximum(m, sink)
            p_x, p_l, p_c, p_r = (jnp.exp2(s_x - m), jnp.exp2(s_l - m), jnp.exp2(s_c - m), jnp.exp2(s_r - m))
            den = (jnp.sum(p_x, -1, keepdims=True) + jnp.sum(p_l, -1, keepdims=True)
                   + jnp.sum(p_c, -1, keepdims=True) + jnp.sum(p_r, -1, keepdims=True) + jnp.exp2(sink - m))
            num = (_dot(p_x.astype(BF16), v_x) + _dot(p_l.astype(BF16), v_l)
                   + _dot(p_c.astype(BF16), v_c) + _dot(p_r.astype(BF16), v_r))
            o = num / den
            acc = o if acc is None else acc + o
        slabs.append(acc)
    o_ref[0] = jnp.concatenate(slabs, axis=-1)


def _window_attention(sink, q, k, v, with_ctx_queries, layer):
    b, nq_slabs, s, _ = q.shape
    nh = k.shape[1]
    nt = s // TILE
    off = 0 if with_ctx_queries else 1
    nq = nt - off
    kern = functools.partial(_window_attn_kernel, tile_off=off, layer=layer)
    return pl.pallas_call(
        kern,
        grid=(b, nq),
        in_specs=[pl.BlockSpec(memory_space=pltpu.SMEM),
                  pl.BlockSpec((1, nq_slabs, TILE, LANES), lambda i, t: (i, 0, t + off, 0)),
                  pl.BlockSpec((1, nh, s, LANES), lambda i, t: (i, 0, 0, 0)),
                  pl.BlockSpec((1, nh, s, LANES), lambda i, t: (i, 0, 0, 0))],
        out_specs=pl.BlockSpec((1, TILE, 2 * LANES), lambda i, t: (i, t, 0)),
        out_shape=jax.ShapeDtypeStruct((b, nq * TILE, 2 * LANES), F32),
        compiler_params=_cparams(("arbitrary", "arbitrary")),
        name="window_attention",
    )(sink, q, k, v)


def _lru_conv(prev_ref, cur_ref, next_ref, prev_ok, next_ok, vec_ref):
    t = cur_ref.shape[1]
    prev = jnp.where(prev_ok, prev_ref[0], 0.0)
    nxt = jnp.where(next_ok, next_ref[0], 0.0)
    ext = jnp.concatenate([prev, cur_ref[0], nxt], axis=0)
    n = t + 2 * SUBLANES
    c = cur_ref.shape[2]

    def tap(j):
        return vec_ref[0, VEC_CONVW + j:VEC_CONVW + j + 1, 0:c]

    u = vec_ref[0, VEC_CONVB:VEC_CONVB + 1, 0:c] + cur_ref[0] * tap(CONV_LEFT)
    for j in range(CONV_W):
        off = j - CONV_LEFT
        if off == 0:
            continue
        sh = pltpu.roll(ext, (-off) % n, 0)[SUBLANES:SUBLANES + t]
        u = u + sh * tap(j)
    return u


def _lru_scan_tile(u, d, wr_hi, wr_lo, wi_hi, wi_lo, vec_ref, h0, reverse):
    t, c = u.shape

    def vec(row):
        return vec_ref[0, row + d:row + d + 1, 0:c]

    r = _sigmoid(_dot3(u, wr_hi[0, d], wr_lo[0, d]) + vec(VEC_BR))
    i = _sigmoid(_dot3(u, wi_hi[0, d], wi_lo[0, d]) + vec(VEC_BI))
    lam = vec(VEC_LAM)
    softplus_neg = jnp.maximum(-lam, 0.0) + jnp.log1p(jnp.exp(-jnp.abs(lam)))
    log_a = -LRU_C * r * softplus_neg
    a = jnp.exp(log_a)
    bx = jnp.sqrt(1.0 - a * a) * (i * u)
    row = lax.broadcasted_iota(jnp.int32, u.shape, 0)
    step = 1
    while step < t:
        if reverse:
            ok = row < t - step
            a_sh = jnp.where(ok, pltpu.roll(a, t - step, 0), 1.0)
            b_sh = jnp.where(ok, pltpu.roll(bx, t - step, 0), 0.0)
        else:
            ok = row >= step
            a_sh = jnp.where(ok, pltpu.roll(a, step, 0), 1.0)
            b_sh = jnp.where(ok, pltpu.roll(bx, step, 0), 0.0)
        bx = a * b_sh + bx
        a = a * a_sh
        step *= 2
    h = a * h0 + bx
    h_end = h[0:1, :] if reverse else h[t - 1:t, :]
    return h, h_end


def _lru_kernel(fp_ref, fc_ref, fn_ref, rp_ref, rc_ref, rn_ref, vec_ref,
                wr_hi, wr_lo, wi_hi, wi_lo, yf_ref, yb_ref, hf_s, hb_s):
    j = pl.program_id(1)
    nt = pl.num_programs(1)

    @pl.when(j == 0)
    def _():
        hf_s[...] = jnp.zeros_like(hf_s)
        hb_s[...] = jnp.zeros_like(hb_s)

    def bounds(ti):
        return ti >= 2, (ti >= 1) & (ti < nt - 1)

    tf = j
    tr = jnp.where(j == 0, 0, nt - j)
    p_ok, n_ok = bounds(tf)
    uf = _lru_conv(fp_ref, fc_ref, fn_ref, p_ok, n_ok, vec_ref)
    h, h_end = _lru_scan_tile(uf, 0, wr_hi, wr_lo, wi_hi, wi_lo, vec_ref, hf_s[...], False)
    yf_ref[0] = h
    hf_s[...] = h_end
    p_ok, n_ok = bounds(tr)
    ur = _lru_conv(rp_ref, rc_ref, rn_ref, p_ok, n_ok, vec_ref)
    h, h_end = _lru_scan_tile(ur, 1, wr_hi, wr_lo, wi_hi, wi_lo, vec_ref, hb_s[...], True)
    yb_ref[0] = h
    hb_s[...] = h_end


def _rglru(z, w, layer):
    b, s, c = z.shape
    nt = s // TILE
    r8 = TILE // SUBLANES
    n8 = s // SUBLANES

    def rev(j):
        return jnp.where(j == 0, 0, nt - j)

    def cur(f):
        return pl.BlockSpec((1, TILE, c), lambda i, j: (i, f(j), 0))

    def prev(f):
        return pl.BlockSpec((1, SUBLANES, c), lambda i, j: (i, jnp.maximum(f(j) * r8 - 1, 0), 0))

    def nxt(f):
        return pl.BlockSpec((1, SUBLANES, c), lambda i, j: (i, jnp.minimum((f(j) + 1) * r8, n8 - 1), 0))

    ident = lambda j: j
    consts = [w["vecs"], w["wr_hi"], w["wr_lo"], w["wi_hi"], w["wi_lo"]]
    return pl.pallas_call(
        _lru_kernel,
        grid=(b, nt),
        in_specs=[prev(ident), cur(ident), nxt(ident), prev(rev), cur(rev), nxt(rev)]
                 + [_layer_spec(x, layer) for x in consts],
        out_specs=[cur(ident), cur(rev)],
        out_shape=[jax.ShapeDtypeStruct((b, s, c), F32)] * 2,
        scratch_shapes=[pltpu.VMEM((1, c), F32), pltpu.VMEM((1, c), F32)],
        compiler_params=_cparams(("arbitrary", "arbitrary")),
        name="rglru",
    )(z, z, z, z, z, z, *consts)


def _merge_kernel(*refs, split_ctx):
    x_ref, ctx_ref = (refs[0], refs[1]) if split_ctx else (refs[0], None)
    (mod_ref, vec_ref, wg_ref, wm_ref, ya_ref, yb_ref, yc_ref, yf_ref, yr_ref,
     wb_ref, wo_ref, o_ref) = refs[2 if split_ctx else 1:]
    x = _stream_tile(x_ref, ctx_ref, pl.program_id(1) == 0)
    hb = _modulated_norm(x, mod_ref, vec_ref).astype(BF16)
    zg = _dot(hb, wg_ref[0])
    d = x.shape[1]
    bw = ya_ref.shape[2]
    branches = (ya_ref[0], yb_ref[0], yc_ref[0], yf_ref[0] + yr_ref[0])
    mix = jnp.zeros_like(x)
    for k in range(N_BRANCH):
        g = zg[:, k * bw:(k + 1) * bw]
        y = branches[k] * (g * _sigmoid(g))
        proj = _dot(y.astype(BF16), wb_ref[0, k])
        zm = _dot(hb, wm_ref[0, :, k * d:(k + 1) * d])
        mix = mix + _sigmoid(zm) * proj
    o_ref[0] = x + mod_ref[0, 0, 2:3, :] * _dot(mix.astype(BF16), wo_ref[0])


def _merge(x, ctx, mod_all, w, ya, yb, yc, yf, yr, with_ctx, layer):
    split_ctx = ctx is not None
    assert with_ctx or not split_ctx
    b, d = x.shape[0], x.shape[2]
    s = x.shape[1] + (ctx.shape[1] if split_ctx else 0)
    nt = s // TILE
    off = 0 if with_ctx else 1
    nq = nt - off
    ctx_row = b
    bw = ya.shape[2]

    def stream(width):
        return pl.BlockSpec((1, TILE, width), lambda i, t: (i, t + off, 0))

    def local(width):
        return pl.BlockSpec((1, TILE, width), lambda i, t: (i, t, 0))

    if split_ctx:
        streams = [x, ctx]
        stream_specs = [pl.BlockSpec((1, TILE, d), lambda i, t: (i, jnp.maximum(t - 1, 0), 0)),
                        pl.BlockSpec((1, TILE, d), lambda i, t: (i, 0, 0))]
    else:
        streams, stream_specs = [x], [stream(d)]
    consts_a = [w["vecs"], w["wg"], w["wm"]]
    consts_b = [w["wb"], w["wo"]]
    return pl.pallas_call(
        functools.partial(_merge_kernel, split_ctx=split_ctx),
        grid=(b, nq),
        in_specs=stream_specs
                 + [pl.BlockSpec((1, 1, 3, d), lambda i, t: (layer, jnp.where(t + off == 0, ctx_row, i), 0, 0))]
                 + [_layer_spec(c, layer) for c in consts_a]
                 + [local(bw), local(bw), local(bw), stream(bw), stream(bw)]
                 + [_layer_spec(c, layer) for c in consts_b],
        out_specs=local(d),
        out_shape=jax.ShapeDtypeStruct((b, nq * TILE, d), F32),
        compiler_params=_cparams(("arbitrary", "arbitrary")),
        name="merge",
    )(*streams, mod_all, *consts_a, ya, yb, yc, yf, yr, *consts_b)


def _rope_tables(n, n_ctx):
    def axial(rot_dim):
        n_rows = n // GRID_W
        rows = np.repeat(np.arange(n_rows, dtype=np.float64), GRID_W)
        cols = np.tile(np.arange(GRID_W, dtype=np.float64), n_rows)
        quarter = rot_dim // 4
        freqs = ROPE_THETA ** (-np.arange(quarter, dtype=np.float64) / quarter)
        ang = np.concatenate([rows[:, None] * freqs, cols[:, None] * freqs], axis=-1)
        return np.cos(ang), np.sin(ang)

    def with_ctx(cos, sin):
        return (jnp.asarray(np.concatenate([np.ones((n_ctx, LANES)), cos], axis=0), F32),
                jnp.asarray(np.concatenate([np.zeros((n_ctx, LANES)), sin], axis=0), F32))

    cm, sm = axial(MLA_ROPE)
    one, zero = np.ones((n, MLA_NOPE)), np.zeros((n, MLA_NOPE))
    pad1, pad0 = np.ones((n, LANES - MLA_QK)), np.zeros((n, LANES - MLA_QK))
    cosm, sinm = with_ctx(np.concatenate([one, cm, cm, pad1], axis=-1),
                          np.concatenate([zero, sm, sm, pad0], axis=-1))
    ch, sh = axial(HEAD_DIM)
    cosh, sinh = with_ctx(np.tile(ch, (1, 4)), np.tile(sh, (1, 4)))
    return cosm, sinm, cosh, sinh


def _rot_fold(w, gain, n_heads, half):
    l, k, c = w.shape
    wg = (w * gain[:, None, :]).reshape(l, k, n_heads, 2, half)
    return jnp.concatenate([-wg[:, :, :, 1], wg[:, :, :, 0]], axis=3).reshape(l, k, c)


def _prepare_weights(p):
    w_in = p["w_in"]
    depth, d, _ = w_in.shape
    sizes = (256, 128, 32, 256, 128, 128, 256, 128, 128, 256, 1024, 4 * d)
    offs = [0]
    for sz in sizes:
        offs.append(offs[-1] + sz)
    cq, ckv, kr, sq, sk, sv, aq, ak, av, lru, gate, merge = (
        w_in[:, :, offs[i]:offs[i + 1]] for i in range(len(sizes)))

    q_scale = HEAD_DIM ** -0.5 * LOG2E
    mq_scale = MLA_QK ** -0.5 * LOG2E
    g_sq = jnp.tile(p["swa_q_norm"] * q_scale, (1, 2))
    g_sk = jnp.tile(p["swa_k_norm"], (1, 2))
    g_aq = jnp.tile(p["axa_q_norm"] * q_scale, (1, 2))
    g_ak = jnp.tile(p["axa_k_norm"], (1, 2))
    g_mq = p["mla_q_norm"] * mq_scale
    g_mk = p["mla_k_norm"]

    def rope_slab(w_rope, g_rope):
        l, k, nh, r = w_rope.shape
        rot = _rot_fold(w_rope.reshape(l, k, nh * r), jnp.tile(g_rope, (1, nh)), nh, r // 2)
        return jnp.pad(rot.reshape(w_rope.shape), ((0, 0), (0, 0), (0, 0), (MLA_NOPE, LANES - MLA_QK)))

    kr_slab = jnp.pad(kr, ((0, 0), (0, 0), (MLA_NOPE, LANES - MLA_QK)))
    krr_slab = rope_slab(kr[:, :, None, :], g_mk[:, MLA_NOPE:])[:, :, 0]
    half = HEAD_DIM // 2
    wp = jnp.concatenate([
        cq, ckv, kr_slab, krr_slab,
        sq, _rot_fold(sq, jnp.tile(g_sq, (1, 2)), SWA_Q_HEADS, half),
        sk, _rot_fold(sk, g_sk, SWA_KV_HEADS, half), sv,
        aq, _rot_fold(aq, jnp.tile(g_aq, (1, 2)), AXA_Q_HEADS, half),
        ak, _rot_fold(ak, g_ak, AXA_KV_HEADS, half), av,
        lru], axis=2).astype(BF16)

    w_uq = p["mla_w_uq"].reshape(depth, -1, MLA_HEADS, MLA_QK)
    w_ukv = p["mla_w_ukv"].reshape(depth, -1, MLA_HEADS, MLA_NOPE + MLA_V)
    pad4 = lambda a, n: jnp.pad(a, ((0, 0), (0, 0), (0, 0), (0, n)))
    wuq = pad4(w_uq, LANES - MLA_QK).reshape(depth, -1, MLA_HEADS * LANES).astype(BF16)
    wuqr = rope_slab(w_uq[..., MLA_NOPE:], g_mq[:, MLA_NOPE:]).reshape(depth, -1, MLA_HEADS * LANES).astype(BF16)
    wuk = pad4(w_ukv[..., :MLA_NOPE], LANES - MLA_NOPE).reshape(depth, -1, MLA_HEADS * LANES).astype(BF16)
    wuv = w_ukv[..., MLA_NOPE:].reshape(depth, -1, MLA_HEADS * MLA_V).astype(BF16)

    def row(v):
        v = v if v.ndim == 3 else v[:, None, :]
        return jnp.pad(v, ((0, 0), (0, 0), (0, d - v.shape[2])))

    slab = lambda g: jnp.pad(g, ((0, 0), (0, LANES - g.shape[1])))
    gains = jnp.concatenate([p["mla_ckv_norm"], slab(g_mq), slab(g_mk), g_sq, g_sk, g_aq, g_ak], axis=1)
    vec_rows = [row(p["norm_w"]), row(p["mla_cq_norm"]), row(gains), row(p["lru_conv_w"]),
                row(p["lru_conv_b"]), row(p["lru_b_r"]), row(p["lru_b_i"]), row(p["lru_lambda"])]
    n_rows = sum(r.shape[1] for r in vec_rows)
    vecs = jnp.concatenate(vec_rows + [jnp.zeros((depth, VEC_ROWS - n_rows, d), F32)], axis=1)

    idx = np.arange(4 * HEAD_DIM) // HEAD_DIM
    gmat = jnp.asarray(idx[:, None] == idx[None, :], BF16)

    def block_diag_hi_lo(wb):
        l, two, k, m, _ = wb.shape
        full = jnp.einsum("ldkij,kn->ldkinj", wb, jnp.eye(k, dtype=wb.dtype)).reshape(l, two, k * m, k * m)
        hi = full.astype(BF16)
        return hi, (full - hi.astype(F32)).astype(BF16)

    wr_hi, wr_lo = block_diag_hi_lo(p["lru_w_r"])
    wi_hi, wi_lo = block_diag_hi_lo(p["lru_w_i"])
    return {
        "vecs": vecs, "gmat": gmat,
        "wp": wp, "wuq": wuq, "wuqr": wuqr, "wuk": wuk, "wuv": wuv,
        "wg": gate.astype(BF16), "wm": merge.astype(BF16),
        "wb": p["w_branch"].astype(BF16), "wo": p["w_out"].astype(BF16),
        "wr_hi": wr_hi, "wr_lo": wr_lo, "wi_hi": wi_hi, "wi_lo": wi_lo,
    }


def kernel(x, c, ctx, c_ctx, w_mod, b_mod, norm_w, w_in, mla_cq_norm, mla_ckv_norm, mla_w_uq, mla_w_ukv, mla_q_norm, mla_k_norm, swa_q_norm, swa_k_norm, swa_sink, axa_q_norm, axa_k_norm, lru_conv_w, lru_conv_b, lru_w_r, lru_b_r, lru_w_i, lru_b_i, lru_lambda, w_branch, w_out):
    p = dict(norm_w=norm_w, w_in=w_in, mla_cq_norm=mla_cq_norm, mla_ckv_norm=mla_ckv_norm,
             mla_w_uq=mla_w_uq, mla_w_ukv=mla_w_ukv, mla_q_norm=mla_q_norm, mla_k_norm=mla_k_norm,
             swa_q_norm=swa_q_norm, swa_k_norm=swa_k_norm,
             axa_q_norm=axa_q_norm, axa_k_norm=axa_k_norm, lru_conv_w=lru_conv_w,
             lru_conv_b=lru_conv_b, lru_w_r=lru_w_r, lru_b_r=lru_b_r, lru_w_i=lru_w_i,
             lru_b_i=lru_b_i, lru_lambda=lru_lambda, w_branch=w_branch, w_out=w_out)
    b, n, d = x.shape
    n_ctx = ctx.shape[1]
    depth = w_mod.shape[0]
    assert n_ctx == TILE and n % TILE == 0 and n % GRID_W == 0 and b + 1 <= SUBLANES

    rows = jnp.concatenate([c, c_ctx[None, :], jnp.zeros((SUBLANES - b - 1, d), F32)], axis=0)
    mod_all = _modulation(rows, w_mod, b_mod).reshape(depth, SUBLANES, 3, d)

    w = _prepare_weights(p)
    tabs = _rope_tables(n, n_ctx)
    xs, cs = x, ctx
    for layer in range(depth):
        upd = layer < depth - 1
        mq, mk, mv, sq, sk, sv, aq, ak, av, zl = _project(xs, cs, mod_all, w, tabs, layer)
        ya = _dense_attention(mq, mk, mv, upd)
        yb = _window_attention(swa_sink, sq, sk, sv, upd, layer)
        yc = _dense_attention(aq, ak, av, upd)
        yf, yr = _rglru(zl, w, layer)
        if cs is not None and not upd:
            xs, cs = jnp.concatenate([cs, xs], axis=1), None
        xs, cs = _merge(xs, cs, mod_all, w, ya, yb, yc, yf, yr, upd, layer), None
    return xs
```

```python
import functools

import numpy as np
import jax
import jax.numpy as jnp
from jax import lax
from jax.experimental import pallas as pl
from jax.experimental.pallas import tpu as pltpu

GRID_W = 64
N_BRANCH = 4
HEAD_DIM = 64
MLA_HEADS = 4
MLA_NOPE = 64
MLA_ROPE = 32
MLA_V = 64
MLA_QK = MLA_NOPE + MLA_ROPE
SWA_Q_HEADS = 4
SWA_KV_HEADS = 2
WINDOW = 128
AXA_Q_HEADS = 4
AXA_KV_HEADS = 2
LRU_BLOCKS = 4
LRU_C = 8.0
CONV_W = 4
CONV_LEFT = 2
ROPE_THETA = 10000.0
RMS_EPS = 1e-6
NEG_INF = -1e30
LOG2E = 1.4426950408889634

LANES = 128
SUBLANES = 8
TILE = 256
KEY_CHUNKS = 4
CHUNK_GROUPS = 4
SHIFT_MARGIN = 1.02
MIN_ROW_SUM = 2.0 ** -80
VMEM_LIMIT = 56 * 1024 * 1024

F32 = jnp.float32
BF16 = jnp.bfloat16

VEC_NW, VEC_GCQ, VEC_GAINS, VEC_CONVW, VEC_CONVB, VEC_BR, VEC_BI, VEC_LAM, VEC_ROWS = 0, 1, 2, 3, 7, 8, 10, 12, 16
GAIN_CKV, GAIN_MQ, GAIN_MK, GAIN_SQ, GAIN_SK, GAIN_AQ, GAIN_AK = range(7)


def _dot(a, b):
    return jnp.dot(a, b, preferred_element_type=F32)


def _dot_nt(a, b):
    return lax.dot_general(a, b, (((1,), (1,)), ((), ())), preferred_element_type=F32)


def _split(a):
    hi = a.astype(BF16)
    lo = (a - hi.astype(F32)).astype(BF16)
    return hi, lo


def _dot3(a, w_hi, w_lo):
    a_hi, a_lo = _split(a)
    return _dot(a_hi, w_hi) + _dot(a_lo, w_hi) + _dot(a_hi, w_lo)


def _sigmoid(v):
    return 1.0 / (1.0 + jnp.exp(-v))


def _cparams(sem):
    return pltpu.CompilerParams(dimension_semantics=sem, vmem_limit_bytes=VMEM_LIMIT)


def _layer_spec(arr, layer):
    nd = arr.ndim - 1
    return pl.BlockSpec((1,) + arr.shape[1:], lambda *_: (layer,) + (0,) * nd)


def _gain(vec_ref, slot):
    return vec_ref[0, VEC_GAINS:VEC_GAINS + 1, slot * LANES:(slot + 1) * LANES]


def _mod_kernel(c_ref, w_ref, b_ref, o_ref):
    v = c_ref[...]
    a = v * _sigmoid(v)
    w_hi, w_lo = _split(w_ref[0])
    o_ref[0] = _dot3(a, w_hi, w_lo) + b_ref[0]


def _modulation(cvec, w_mod, b_mod):
    depth, d, d3 = w_mod.shape
    r = cvec.shape[0]
    bn = 512
    return pl.pallas_call(
        _mod_kernel,
        grid=(depth, d3 // bn),
        in_specs=[pl.BlockSpec((r, d), lambda l, j: (0, 0)),
                  pl.BlockSpec((1, d, bn), lambda l, j: (l, 0, j)),
                  pl.BlockSpec((1, 1, bn), lambda l, j: (l, 0, j))],
        out_specs=pl.BlockSpec((1, r, bn), lambda l, j: (l, 0, j)),
        out_shape=jax.ShapeDtypeStruct((depth, r, d3), F32),
        compiler_params=_cparams(("arbitrary", "arbitrary")),
        name="modulation",
    )(cvec, w_mod, b_mod.reshape(depth, 1, d3))


def _modulated_norm(x, mod_ref, vec_ref):
    ms = jnp.mean(x * x, axis=-1, keepdims=True)
    y = x * lax.rsqrt(ms + RMS_EPS) * vec_ref[0, VEC_NW:VEC_NW + 1, :]
    return y * (1.0 + mod_ref[0, 0, 1:2, :]) + mod_ref[0, 0, 0:1, :]


def _stream_tile(x_ref, ctx_ref, is_ctx):
    if ctx_ref is None:
        return x_ref[0]
    return jnp.where(is_ctx, ctx_ref[0], x_ref[0])


def _lane_rinv(v, n):
    return lax.rsqrt(jnp.sum(v * v, axis=-1, keepdims=True) * (1.0 / n) + RMS_EPS)


def _head_rinv(v, gmat):
    hi, lo = _split(v * v)
    ss = _dot(hi, gmat) + _dot(lo, gmat)
    return lax.rsqrt(ss * (1.0 / HEAD_DIM) + RMS_EPS)


def _proj_kernel(*refs, split_ctx):
    x_ref, ctx_ref = (refs[0], refs[1]) if split_ctx else (refs[0], None)
    (mod_ref, vec_ref, wp_ref, wuq_ref, wuqr_ref, wuk_ref, wuv_ref, gmat_ref,
     cosm_ref, sinm_ref, cosh_ref, sinh_ref,
     mq_o, mk_o, mv_o, sq_o, sk_o, sv_o, aq_o, ak_o, av_o, lru_o) = refs[2 if split_ctx else 1:]
    x = _stream_tile(x_ref, ctx_ref, pl.program_id(0) == 0)
    h = _modulated_norm(x, mod_ref, vec_ref)
    z = _dot(h.astype(BF16), wp_ref[0])
    t = x.shape[0]
    lo = lax.broadcasted_iota(jnp.int32, (t, LANES), 1) < HEAD_DIM
    cosm, sinm = cosm_ref[...], sinm_ref[...]
    cosh, sinh = cosh_ref[...], sinh_ref[...]
    gmat = gmat_ref[...]

    def pair_store(out, slab):
        swapped = pltpu.roll(slab, HEAD_DIM, 1)
        out[0, 0] = jnp.where(lo, slab, 0.0).astype(BF16)
        out[0, 1] = jnp.where(lo, 0.0, swapped).astype(BF16)
        out[0, 2] = jnp.where(lo, swapped, 0.0).astype(BF16)
        out[0, 3] = jnp.where(lo, 0.0, slab).astype(BF16)

    g_cq = vec_ref[0, VEC_GCQ:VEC_GCQ + 1, 0:256]
    cq = (z[:, 0:256] * _lane_rinv(z[:, 0:256], 256.0) * g_cq).astype(BF16)
    q = _dot(cq, wuq_ref[0])
    qr = _dot(cq, wuqr_ref[0])
    ckv = (z[:, 256:384] * _lane_rinv(z[:, 256:384], 128.0) * _gain(vec_ref, GAIN_CKV)).astype(BF16)
    kn = _dot(ckv, wuk_ref[0])
    vv = _dot(ckv, wuv_ref[0])
    kr = z[:, 384:512]
    krr = z[:, 512:640]
    gc_q = _gain(vec_ref, GAIN_MQ) * cosm
    gc_k = _gain(vec_ref, GAIN_MK) * cosm
    for hh in range(MLA_HEADS):
        sl = slice(hh * LANES, (hh + 1) * LANES)
        r = _lane_rinv(q[:, sl], float(MLA_QK))
        mq_o[0, hh] = ((q[:, sl] * r) * gc_q + (qr[:, sl] * r) * sinm).astype(BF16)
        ks = kn[:, sl] + kr
        r = _lane_rinv(ks, float(MLA_QK))
        mk_o[0, hh] = ((ks * r) * gc_k + (krr * r) * sinm).astype(BF16)
    mv_o[0] = vv.T.astype(BF16)

    def gqa(c0, g_q, g_k, q_o, k_o, v_o, v_transposed):
        zq, zqr = z[:, c0:c0 + 256], z[:, c0 + 256:c0 + 512]
        r = _head_rinv(zq, gmat)
        gc = _gain(vec_ref, g_q) * cosh
        for i in range(2):
            sl = slice(i * LANES, (i + 1) * LANES)
            q_o[0, i] = ((zq[:, sl] * r[:, sl]) * gc + (zqr[:, sl] * r[:, sl]) * sinh).astype(BF16)
        zk, zkr = z[:, c0 + 512:c0 + 640], z[:, c0 + 640:c0 + 768]
        r = _head_rinv(zk, gmat[:LANES, :LANES])
        ks = (zk * r) * (_gain(vec_ref, g_k) * cosh) + (zkr * r) * sinh
        pair_store(k_o, ks)
        zv = z[:, c0 + 768:c0 + 896]
        if v_transposed:
            v_o[0] = zv.T.astype(BF16)
        else:
            pair_store(v_o, zv)

    gqa(640, GAIN_SQ, GAIN_SK, sq_o, sk_o, sv_o, False)
    gqa(1536, GAIN_AQ, GAIN_AK, aq_o, ak_o, av_o, True)
    lru_o[0] = z[:, 2432:2688]


def _project(x, ctx, mod_all, w, tabs, layer):
    split_ctx = ctx is not None
    b, d = x.shape[0], x.shape[2]
    s = x.shape[1] + (ctx.shape[1] if split_ctx else 0)
    nt = s // TILE
    ctx_row = b

    def tile_spec(width):
        return pl.BlockSpec((1, TILE, width), lambda t, i: (i, t, 0))

    def head_spec(nh):
        return pl.BlockSpec((1, nh, TILE, LANES), lambda t, i: (i, 0, t, 0))

    def head_spec_t(nh):
        return pl.BlockSpec((1, nh * HEAD_DIM, TILE), lambda t, i: (i, 0, t))

    if split_ctx:
        streams = [x, ctx]
        stream_specs = [pl.BlockSpec((1, TILE, d), lambda t, i: (i, jnp.maximum(t - 1, 0), 0)),
                        pl.BlockSpec((1, TILE, d), lambda t, i: (jnp.where(t == 0, i, b - 1), 0, 0))]
    else:
        streams, stream_specs = [x], [tile_spec(d)]
    tab_spec = pl.BlockSpec((TILE, LANES), lambda t, i: (t, 0))
    consts = [w["vecs"], w["wp"], w["wuq"], w["wuqr"], w["wuk"], w["wuv"]]
    in_specs = (stream_specs
                + [pl.BlockSpec((1, 1, 3, d), lambda t, i: (layer, jnp.where(t == 0, ctx_row, i), 0, 0))]
                + [_layer_spec(c, layer) for c in consts]
                + [pl.BlockSpec(w["gmat"].shape, lambda t, i: (0, 0))] + [tab_spec] * 4)
    n_slabs = [MLA_HEADS, MLA_HEADS, MLA_HEADS, 2, 4, 4, 2, 4, AXA_KV_HEADS]
    transposed = [False, False, True, False, False, False, False, False, True]
    out_specs = ([head_spec_t(nh) if tr else head_spec(nh) for nh, tr in zip(n_slabs, transposed)]
                 + [tile_spec(4 * HEAD_DIM)])
    out_shape = ([jax.ShapeDtypeStruct((b, nh * HEAD_DIM, s) if tr else (b, nh, s, LANES), BF16)
                  for nh, tr in zip(n_slabs, transposed)]
                 + [jax.ShapeDtypeStruct((b, s, 4 * HEAD_DIM), F32)])
    return pl.pallas_call(
        functools.partial(_proj_kernel, split_ctx=split_ctx),
        grid=(nt, b),
        in_specs=in_specs,
        out_specs=out_specs,
        out_shape=out_shape,
        compiler_params=_cparams(("arbitrary", "arbitrary")),
        name="project",
    )(*streams, mod_all, *consts, w["gmat"], *tabs)


def _dense_attn_kernel(*refs, q_share, gain_slots, head_dim, ctx_tile_first):
    vec_ref, q_refs, (k_ref, v_ref, o_ref) = refs[0], refs[1:-3], refs[-3:]
    s_len = k_ref.shape[2]
    n_heads = k_ref.shape[1]
    v_share = n_heads * HEAD_DIM // v_ref.shape[1]
    units = [(q_ref, hh) for q_ref in q_refs for hh in range(n_heads)]

    g_q, g_k = (jnp.max(jnp.abs(_gain(vec_ref, slot)), axis=-1, keepdims=True) for slot in gain_slots)
    shift = g_q * g_k * (head_dim * SHIFT_MARGIN)

    def band(hh):
        return slice(hh // v_share * HEAD_DIM, (hh // v_share + 1) * HEAD_DIM)

    def key_chunks(n_keys, groups):
        edges = list(range(0, n_keys, groups * LANES)) + [n_keys]
        return [slice(a, b) for a, b in zip(edges[:-1], edges[1:])]

    def store(outs):
        for j in range(len(q_refs)):
            o = outs[j * n_heads:(j + 1) * n_heads]
            o_ref[0, j * TILE:(j + 1) * TILE, :] = jnp.concatenate(o, axis=0).T

    def attend_shifted(n_keys):
        half = n_keys // (2 * LANES) * LANES
        outs, l_min = [], None
        for q_ref, hh in units:
            q = q_ref[0, hh // q_share]
            l, ot = None, None
            for lo_k, hi_k in ((0, half), (half, n_keys)):
                st = _dot_nt(k_ref[0, hh, lo_k:hi_k, :], q)
                for sl in key_chunks(hi_k - lo_k, CHUNK_GROUPS):
                    pt = jnp.exp2(st[sl] - shift)
                    lc = jnp.sum(pt, axis=0, keepdims=True)
                    oc = _dot(v_ref[0, band(hh), lo_k + sl.start:lo_k + sl.stop],
                              pt.astype(BF16))
                    l, ot = (lc, oc) if l is None else (l + lc, ot + oc)
            outs.append(ot / l)
            lm = jnp.min(l)
            l_min = lm if l_min is None else jnp.minimum(l_min, lm)
        store(outs)
        return l_min

    def attend_exact(n_keys):
        n_groups = n_keys // LANES
        chunks = key_chunks(n_keys, -(-n_groups // min(KEY_CHUNKS, n_groups)))

        def scores(unit):
            q_ref, hh = unit
            q = q_ref[0, hh // q_share]
            return [_dot_nt(k_ref[0, hh, sl, :], q) for sl in chunks]

        def head_out(unit, st):
            hh = unit[1]
            m = functools.reduce(jnp.maximum, [jnp.max(sc, axis=0, keepdims=True) for sc in st])
            pt = [jnp.exp2(sc - m) for sc in st]
            l = sum(jnp.sum(pc, axis=0, keepdims=True) for pc in pt)
            ot = sum(_dot(v_ref[0, band(hh), sl], pc.astype(BF16)) for sl, pc in zip(chunks, pt))
            return ot / l

        st_next = scores(units[0])
        outs = []
        for u, unit in enumerate(units):
            st = st_next
            if u + 1 < len(units):
                st_next = scores(units[u + 1])
            outs.append(head_out(unit, st))
        store(outs)

    def attend(n_keys):
        l_min = attend_shifted(n_keys)
        pl.when(jnp.logical_not(l_min >= MIN_ROW_SUM))(lambda: attend_exact(n_keys))

    if ctx_tile_first:
        pl.when(pl.program_id(1) == 0)(lambda: attend(TILE))
        pl.when(pl.program_id(1) > 0)(lambda: attend(s_len))
    else:
        attend(s_len)


def _dense_attention(q, k, v, vecs, layer, gain_slots, head_dim, with_ctx_queries):
    b, nq_slabs, s, _ = q.shape
    nh = k.shape[1]
    nt = s // TILE
    off = 0 if with_ctx_queries else 1
    tps = 1 if with_ctx_queries or (nt - off) % 2 else 2
    steps = (nt - off) // tps
    kern = functools.partial(_dense_attn_kernel, q_share=nh // nq_slabs, gain_slots=gain_slots,
                             head_dim=head_dim, ctx_tile_first=with_ctx_queries)

    def q_spec(j):
        return pl.BlockSpec((1, nq_slabs, TILE, LANES), lambda i, t: (i, 0, tps * t + j + off, 0))

    return pl.pallas_call(
        kern,
        grid=(b, steps),
        in_specs=[_layer_spec(vecs, layer)] + [q_spec(j) for j in range(tps)]
                 + [pl.BlockSpec((1, nh, s, LANES), lambda i, t: (i, 0, 0, 0)),
                    pl.BlockSpec((1,) + v.shape[1:], lambda i, t: (i, 0, 0))],
        out_specs=pl.BlockSpec((1, tps * TILE, 2 * LANES), lambda i, t: (i, t, 0)),
        out_shape=jax.ShapeDtypeStruct((b, steps * tps * TILE, 2 * LANES), F32),
        compiler_params=_cparams(("arbitrary", "arbitrary")),
        name="dense_attention",
    )(vecs, *([q] * tps), k, v)


def _window_attn_kernel(sink_ref, q_ref, k_ref, v_ref, o_ref, *, tile_off, layer):
    s_len = k_ref.shape[2]
    n_tiles = s_len // TILE
    ti = pl.program_id(1) + tile_off
    is_lat = ti > 0
    t0 = ti * TILE
    half = TILE // 2
    left0 = pl.multiple_of(jnp.maximum(t0 - half, 0), half)
    cen0 = pl.multiple_of(t0, TILE)
    right0 = pl.multiple_of(jnp.minimum(t0 + TILE, s_len - half), half)
    r_c = lax.broadcasted_iota(jnp.int32, (TILE, TILE), 0)
    c_c = lax.broadcasted_iota(jnp.int32, (TILE, TILE), 1)
    r_s = lax.broadcasted_iota(jnp.int32, (TILE, half), 0)
    c_s = lax.broadcasted_iota(jnp.int32, (TILE, half), 1)
    ok_cen = (jnp.abs(r_c - c_c) <= WINDOW) & is_lat
    ok_left = ((r_s - c_s + half) <= WINDOW) & (ti > 1)
    ok_right = ((c_s - r_s + TILE) <= WINDOW) & is_lat & (ti < n_tiles - 1)
    def lane_groups(a):
        return [a[:, g * LANES:(g + 1) * LANES] for g in range(a.shape[1] // LANES)]

    def scores(hh):
        q = q_ref[0, hh // 2]
        k = lambda start, size: k_ref[0, hh, pl.ds(start, size), :]
        return (_dot_nt(q, k(0, TILE)),
                jnp.where(ok_left, _dot_nt(q, k(left0, half)), NEG_INF),
                jnp.where(ok_cen, _dot_nt(q, k(cen0, TILE)), NEG_INF),
                jnp.where(ok_right, _dot_nt(q, k(right0, half)), NEG_INF))

    def head_out(hh, sc):
        sink = sink_ref[layer, hh] * LOG2E
        m = functools.reduce(jnp.maximum, [g for piece in sc for g in lane_groups(piece)])
        m = jnp.maximum(jnp.max(m, -1, keepdims=True), sink)
        ps = [jnp.exp2(piece - m) for piece in sc]
        den = sum(g for piece in ps for g in lane_groups(piece))
        den = jnp.sum(den, -1, keepdims=True) + jnp.exp2(sink - m)
        v = lambda start, size: v_ref[0, hh, pl.ds(start, size), :]
        vs = (v(0, TILE), v(left0, half), v(cen0, TILE), v(right0, half))
        num = sum(_dot(p.astype(BF16), vv) for p, vv in zip(ps, vs))
        return num / den

    n_heads = k_ref.shape[1]
    sc_next = scores(0)
    outs = []
    for hh in range(n_heads):
        sc = sc_next
        if hh + 1 < n_heads:
            sc_next = scores(hh + 1)
        outs.append(head_out(hh, sc))
    o_ref[0] = jnp.concatenate([outs[0] + outs[1], outs[2] + outs[3]], axis=-1)


def _window_attention(sink, q, k, v, with_ctx_queries, layer):
    b, nq_slabs, s, _ = q.shape
    nh = k.shape[1]
    nt = s // TILE
    off = 0 if with_ctx_queries else 1
    nq = nt - off
    kern = functools.partial(_window_attn_kernel, tile_off=off, layer=layer)
    return pl.pallas_call(
        kern,
        grid=(b, nq),
        in_specs=[pl.BlockSpec(memory_space=pltpu.SMEM),
                  pl.BlockSpec((1, nq_slabs, TILE, LANES), lambda i, t: (i, 0, t + off, 0)),
                  pl.BlockSpec((1, nh, s, LANES), lambda i, t: (i, 0, 0, 0)),
                  pl.BlockSpec((1, nh, s, LANES), lambda i, t: (i, 0, 0, 0))],
        out_specs=pl.BlockSpec((1, TILE, 2 * LANES), lambda i, t: (i, t, 0)),
        out_shape=jax.ShapeDtypeStruct((b, nq * TILE, 2 * LANES), F32),
        compiler_params=_cparams(("arbitrary", "arbitrary")),
        name="window_attention",
    )(sink, q, k, v)


def _lru_conv(prev_ref, cur_ref, next_ref, prev_ok, next_ok, vec_ref):
    t = cur_ref.shape[1]
    prev = jnp.where(prev_ok, prev_ref[0], 0.0)
    nxt = jnp.where(next_ok, next_ref[0], 0.0)
    ext = jnp.concatenate([prev, cur_ref[0], nxt], axis=0)
    n = t + 2 * SUBLANES
    c = cur_ref.shape[2]

    def tap(j):
        return vec_ref[0, VEC_CONVW + j:VEC_CONVW + j + 1, 0:c]

    u = vec_ref[0, VEC_CONVB:VEC_CONVB + 1, 0:c] + cur_ref[0] * tap(CONV_LEFT)
    for j in range(CONV_W):
        off = j - CONV_LEFT
        if off == 0:
            continue
        sh = pltpu.roll(ext, (-off) % n, 0)[SUBLANES:SUBLANES + t]
        u = u + sh * tap(j)
    return u


def _lru_scan_tile(u, d, wr_hi, wr_lo, wi_hi, wi_lo, vec_ref, h0, reverse):
    t, c = u.shape

    def vec(row):
        return vec_ref[0, row + d:row + d + 1, 0:c]

    r = _sigmoid(_dot3(u, wr_hi[0, d], wr_lo[0, d]) + vec(VEC_BR))
    i = _sigmoid(_dot3(u, wi_hi[0, d], wi_lo[0, d]) + vec(VEC_BI))
    lam = vec(VEC_LAM)
    softplus_neg = jnp.maximum(-lam, 0.0) + jnp.log1p(jnp.exp(-jnp.abs(lam)))
    log_a = -LRU_C * r * softplus_neg
    a = jnp.exp(log_a)
    bx = jnp.sqrt(1.0 - a * a) * (i * u)
    row = lax.broadcasted_iota(jnp.int32, u.shape, 0)
    step = 1
    while step < t:
        if reverse:
            ok = row < t - step
            a_sh = jnp.where(ok, pltpu.roll(a, t - step, 0), 1.0)
            b_sh = jnp.where(ok, pltpu.roll(bx, t - step, 0), 0.0)
        else:
            ok = row >= step
            a_sh = jnp.where(ok, pltpu.roll(a, step, 0), 1.0)
            b_sh = jnp.where(ok, pltpu.roll(bx, step, 0), 0.0)
        bx = a * b_sh + bx
        a = a * a_sh
        step *= 2
    h = a * h0 + bx
    h_end = h[0:1, :] if reverse else h[t - 1:t, :]
    return h, h_end


def _lru_kernel(fp_ref, fc_ref, fn_ref, rp_ref, rc_ref, rn_ref, vec_ref,
                wr_hi, wr_lo, wi_hi, wi_lo, yf_ref, yb_ref, hf_s, hb_s):
    j = pl.program_id(1)
    nt = pl.num_programs(1)

    @pl.when(j == 0)
    def _():
        hf_s[...] = jnp.zeros_like(hf_s)
        hb_s[...] = jnp.zeros_like(hb_s)

    def bounds(ti):
        return ti >= 2, (ti >= 1) & (ti < nt - 1)

    tf = j
    tr = jnp.where(j == 0, 0, nt - j)
    p_ok, n_ok = bounds(tf)
    uf = _lru_conv(fp_ref, fc_ref, fn_ref, p_ok, n_ok, vec_ref)
    h, h_end = _lru_scan_tile(uf, 0, wr_hi, wr_lo, wi_hi, wi_lo, vec_ref, hf_s[...], False)
    yf_ref[0] = h
    hf_s[...] = h_end
    p_ok, n_ok = bounds(tr)
    ur = _lru_conv(rp_ref, rc_ref, rn_ref, p_ok, n_ok, vec_ref)
    h, h_end = _lru_scan_tile(ur, 1, wr_hi, wr_lo, wi_hi, wi_lo, vec_ref, hb_s[...], True)
    yb_ref[0] = h
    hb_s[...] = h_end


def _rglru(z, w, layer):
    b, s, c = z.shape
    nt = s // TILE
    r8 = TILE // SUBLANES
    n8 = s // SUBLANES

    def rev(j):
        return jnp.where(j == 0, 0, nt - j)

    def cur(f):
        return pl.BlockSpec((1, TILE, c), lambda i, j: (i, f(j), 0))

    def prev(f):
        return pl.BlockSpec((1, SUBLANES, c), lambda i, j: (i, jnp.maximum(f(j) * r8 - 1, 0), 0))

    def nxt(f):
        return pl.BlockSpec((1, SUBLANES, c), lambda i, j: (i, jnp.minimum((f(j) + 1) * r8, n8 - 1), 0))

    ident = lambda j: j
    consts = [w["vecs"], w["wr_hi"], w["wr_lo"], w["wi_hi"], w["wi_lo"]]
    return pl.pallas_call(
        _lru_kernel,
        grid=(b, nt),
        in_specs=[prev(ident), cur(ident), nxt(ident), prev(rev), cur(rev), nxt(rev)]
                 + [_layer_spec(x, layer) for x in consts],
        out_specs=[cur(ident), cur(rev)],
        out_shape=[jax.ShapeDtypeStruct((b, s, c), F32)] * 2,
        scratch_shapes=[pltpu.VMEM((1, c), F32), pltpu.VMEM((1, c), F32)],
        compiler_params=_cparams(("arbitrary", "arbitrary")),
        name="rglru",
    )(z, z, z, z, z, z, *consts)


def _merge_kernel(*refs, split_ctx):
    x_ref, ctx_ref = (refs[0], refs[1]) if split_ctx else (refs[0], None)
    (mod_ref, vec_ref, wg_ref, wm_ref, ya_ref, yb_ref, yc_ref, yf_ref, yr_ref,
     wb_ref, wo_ref, o_ref) = refs[2 if split_ctx else 1:]
    x = _stream_tile(x_ref, ctx_ref, pl.program_id(1) == 0)
    hb = _modulated_norm(x, mod_ref, vec_ref).astype(BF16)
    zg = _dot(hb, wg_ref[0])
    d = x.shape[1]
    bw = ya_ref.shape[2]
    branches = (ya_ref[0], yb_ref[0], yc_ref[0], yf_ref[0] + yr_ref[0])
    mix = jnp.zeros_like(x)
    for k in range(N_BRANCH):
        g = zg[:, k * bw:(k + 1) * bw]
        y = branches[k] * (g * _sigmoid(g))
        proj = _dot(y.astype(BF16), wb_ref[0, k])
        zm = _dot(hb, wm_ref[0, :, k * d:(k + 1) * d])
        mix = mix + _sigmoid(zm) * proj
    o_ref[0] = x + mod_ref[0, 0, 2:3, :] * _dot(mix.astype(BF16), wo_ref[0])


def _merge(x, ctx, mod_all, w, ya, yb, yc, yf, yr, with_ctx, layer):
    split_ctx = ctx is not None
    assert with_ctx or not split_ctx
    b, d = x.shape[0], x.shape[2]
    s = x.shape[1] + (ctx.shape[1] if split_ctx else 0)
    nt = s // TILE
    off = 0 if with_ctx else 1
    nq = nt - off
    ctx_row = b
    bw = ya.shape[2]

    def stream(width):
        return pl.BlockSpec((1, TILE, width), lambda i, t: (i, t + off, 0))

    def local(width):
        return pl.BlockSpec((1, TILE, width), lambda i, t: (i, t, 0))

    if split_ctx:
        streams = [x, ctx]
        stream_specs = [pl.BlockSpec((1, TILE, d), lambda i, t: (i, jnp.maximum(t - 1, 0), 0)),
                        pl.BlockSpec((1, TILE, d), lambda i, t: (i, 0, 0))]
    else:
        streams, stream_specs = [x], [stream(d)]
    consts_a = [w["vecs"], w["wg"], w["wm"]]
    consts_b = [w["wb"], w["wo"]]
    return pl.pallas_call(
        functools.partial(_merge_kernel, split_ctx=split_ctx),
        grid=(b, nq),
        in_specs=stream_specs
                 + [pl.BlockSpec((1, 1, 3, d), lambda i, t: (layer, jnp.where(t + off == 0, ctx_row, i), 0, 0))]
                 + [_layer_spec(c, layer) for c in consts_a]
                 + [local(bw), local(bw), local(bw), stream(bw), stream(bw)]
                 + [_layer_spec(c, layer) for c in consts_b],
        out_specs=local(d),
        out_shape=jax.ShapeDtypeStruct((b, nq * TILE, d), F32),
        compiler_params=_cparams(("arbitrary", "arbitrary")),
        name="merge",
    )(*streams, mod_all, *consts_a, ya, yb, yc, yf, yr, *consts_b)


def _rope_tables(n, n_ctx):
    def axial(rot_dim):
        n_rows = n // GRID_W
        rows = np.repeat(np.arange(n_rows, dtype=np.float64), GRID_W)
        cols = np.tile(np.arange(GRID_W, dtype=np.float64), n_rows)
        quarter = rot_dim // 4
        freqs = ROPE_THETA ** (-np.arange(quarter, dtype=np.float64) / quarter)
        ang = np.concatenate([rows[:, None] * freqs, cols[:, None] * freqs], axis=-1)
        return np.cos(ang), np.sin(ang)

    def with_ctx(cos, sin):
        return (jnp.asarray(np.concatenate([np.ones((n_ctx, LANES)), cos], axis=0), F32),
                jnp.asarray(np.concatenate([np.zeros((n_ctx, LANES)), sin], axis=0), F32))

    cm, sm = axial(MLA_ROPE)
    one, zero = np.ones((n, MLA_NOPE)), np.zeros((n, MLA_NOPE))
    pad1, pad0 = np.ones((n, LANES - MLA_QK)), np.zeros((n, LANES - MLA_QK))
    cosm, sinm = with_ctx(np.concatenate([one, cm, cm, pad1], axis=-1),
                          np.concatenate([zero, sm, sm, pad0], axis=-1))
    ch, sh = axial(HEAD_DIM)
    cosh, sinh = with_ctx(np.tile(ch, (1, 4)), np.tile(sh, (1, 4)))
    return cosm, sinm, cosh, sinh


def _rot_fold(w, gain, n_heads, half):
    l, k, c = w.shape
    wg = (w * gain[:, None, :]).reshape(l, k, n_heads, 2, half)
    return jnp.concatenate([-wg[:, :, :, 1], wg[:, :, :, 0]], axis=3).reshape(l, k, c)


def _prepare_weights(p):
    w_in = p["w_in"]
    depth, d, _ = w_in.shape
    sizes = (256, 128, 32, 256, 128, 128, 256, 128, 128, 256, 1024, 4 * d)
    offs = [0]
    for sz in sizes:
        offs.append(offs[-1] + sz)
    cq, ckv, kr, sq, sk, sv, aq, ak, av, lru, gate, merge = (
        w_in[:, :, offs[i]:offs[i + 1]] for i in range(len(sizes)))

    q_scale = HEAD_DIM ** -0.5 * LOG2E
    mq_scale = MLA_QK ** -0.5 * LOG2E
    g_sq = jnp.tile(p["swa_q_norm"] * q_scale, (1, 2))
    g_sk = jnp.tile(p["swa_k_norm"], (1, 2))
    g_aq = jnp.tile(p["axa_q_norm"] * q_scale, (1, 2))
    g_ak = jnp.tile(p["axa_k_norm"], (1, 2))
    g_mq = p["mla_q_norm"] * mq_scale
    g_mk = p["mla_k_norm"]

    def rope_slab(w_rope, g_rope):
        l, k, nh, r = w_rope.shape
        rot = _rot_fold(w_rope.reshape(l, k, nh * r), jnp.tile(g_rope, (1, nh)), nh, r // 2)
        return jnp.pad(rot.reshape(w_rope.shape), ((0, 0), (0, 0), (0, 0), (MLA_NOPE, LANES - MLA_QK)))

    kr_slab = jnp.pad(kr, ((0, 0), (0, 0), (MLA_NOPE, LANES - MLA_QK)))
    krr_slab = rope_slab(kr[:, :, None, :], g_mk[:, MLA_NOPE:])[:, :, 0]
    half = HEAD_DIM // 2
    wp = jnp.concatenate([
        cq, ckv, kr_slab, krr_slab,
        sq, _rot_fold(sq, jnp.tile(g_sq, (1, 2)), SWA_Q_HEADS, half),
        sk, _rot_fold(sk, g_sk, SWA_KV_HEADS, half), sv,
        aq, _rot_fold(aq, jnp.tile(g_aq, (1, 2)), AXA_Q_HEADS, half),
        ak, _rot_fold(ak, g_ak, AXA_KV_HEADS, half), av,
        lru], axis=2).astype(BF16)

    w_uq = p["mla_w_uq"].reshape(depth, -1, MLA_HEADS, MLA_QK)
    w_ukv = p["mla_w_ukv"].reshape(depth, -1, MLA_HEADS, MLA_NOPE + MLA_V)
    pad4 = lambda a, n: jnp.pad(a, ((0, 0), (0, 0), (0, 0), (0, n)))
    wuq = pad4(w_uq, LANES - MLA_QK).reshape(depth, -1, MLA_HEADS * LANES).astype(BF16)
    wuqr = rope_slab(w_uq[..., MLA_NOPE:], g_mq[:, MLA_NOPE:]).reshape(depth, -1, MLA_HEADS * LANES).astype(BF16)
    wuk = pad4(w_ukv[..., :MLA_NOPE], LANES - MLA_NOPE).reshape(depth, -1, MLA_HEADS * LANES).astype(BF16)
    wuv = w_ukv[..., MLA_NOPE:].reshape(depth, -1, MLA_HEADS * MLA_V).astype(BF16)

    def row(v):
        v = v if v.ndim == 3 else v[:, None, :]
        return jnp.pad(v, ((0, 0), (0, 0), (0, d - v.shape[2])))

    slab = lambda g: jnp.pad(g, ((0, 0), (0, LANES - g.shape[1])))
    gains = jnp.concatenate([p["mla_ckv_norm"], slab(g_mq), slab(g_mk), g_sq, g_sk, g_aq, g_ak], axis=1)
    vec_rows = [row(p["norm_w"]), row(p["mla_cq_norm"]), row(gains), row(p["lru_conv_w"]),
                row(p["lru_conv_b"]), row(p["lru_b_r"]), row(p["lru_b_i"]), row(p["lru_lambda"])]
    n_rows = sum(r.shape[1] for r in vec_rows)
    vecs = jnp.concatenate(vec_rows + [jnp.zeros((depth, VEC_ROWS - n_rows, d), F32)], axis=1)

    idx = np.arange(4 * HEAD_DIM) // HEAD_DIM
    gmat = jnp.asarray(idx[:, None] == idx[None, :], BF16)

    def block_diag_hi_lo(wb):
        l, two, k, m, _ = wb.shape
        full = jnp.einsum("ldkij,kn->ldkinj", wb, jnp.eye(k, dtype=wb.dtype)).reshape(l, two, k * m, k * m)
        hi = full.astype(BF16)
        return hi, (full - hi.astype(F32)).astype(BF16)

    wr_hi, wr_lo = block_diag_hi_lo(p["lru_w_r"])
    wi_hi, wi_lo = block_diag_hi_lo(p["lru_w_i"])
    return {
        "vecs": vecs, "gmat": gmat,
        "wp": wp, "wuq": wuq, "wuqr": wuqr, "wuk": wuk, "wuv": wuv,
        "wg": gate.astype(BF16), "wm": merge.astype(BF16),
        "wb": p["w_branch"].astype(BF16), "wo": p["w_out"].astype(BF16),
        "wr_hi": wr_hi, "wr_lo": wr_lo, "wi_hi": wi_hi, "wi_lo": wi_lo,
    }


def kernel(x, c, ctx, c_ctx, w_mod, b_mod, norm_w, w_in, mla_cq_norm, mla_ckv_norm, mla_w_uq, mla_w_ukv, mla_q_norm, mla_k_norm, swa_q_norm, swa_k_norm, swa_sink, axa_q_norm, axa_k_norm, lru_conv_w, lru_conv_b, lru_w_r, lru_b_r, lru_w_i, lru_b_i, lru_lambda, w_branch, w_out):
    p = dict(norm_w=norm_w, w_in=w_in, mla_cq_norm=mla_cq_norm, mla_ckv_norm=mla_ckv_norm,
             mla_w_uq=mla_w_uq, mla_w_ukv=mla_w_ukv, mla_q_norm=mla_q_norm, mla_k_norm=mla_k_norm,
             swa_q_norm=swa_q_norm, swa_k_norm=swa_k_norm,
             axa_q_norm=axa_q_norm, axa_k_norm=axa_k_norm, lru_conv_w=lru_conv_w,
             lru_conv_b=lru_conv_b, lru_w_r=lru_w_r, lru_b_r=lru_b_r, lru_w_i=lru_w_i,
             lru_b_i=lru_b_i, lru_lambda=lru_lambda, w_branch=w_branch, w_out=w_out)
    b, n, d = x.shape
    n_ctx = ctx.shape[1]
    depth = w_mod.shape[0]
    assert n_ctx == TILE and n % TILE == 0 and n % GRID_W == 0 and b + 1 <= SUBLANES

    rows = jnp.concatenate([c, c_ctx[None, :], jnp.zeros((SUBLANES - b - 1, d), F32)], axis=0)
    mod_all = _modulation(rows, w_mod, b_mod).reshape(depth, SUBLANES, 3, d)

    w = _prepare_weights(p)
    tabs = _rope_tables(n, n_ctx)
    xs, cs = x, ctx
    for layer in range(depth):
        upd = layer < depth - 1
        mq, mk, mv, sq, sk, sv, aq, ak, av, zl = _project(xs, cs, mod_all, w, tabs, layer)
        ya = _dense_attention(mq, mk, mv, w["vecs"], layer, (GAIN_MQ, GAIN_MK), MLA_QK, upd)
        yb = _window_attention(swa_sink, sq, sk, sv, upd, layer)
        yc = _dense_attention(aq, ak, av, w["vecs"], layer, (GAIN_AQ, GAIN_AK), HEAD_DIM, upd)
        yf, yr = _rglru(zl, w, layer)
        if cs is not None and not upd:
            xs, cs = jnp.concatenate([cs, xs], axis=1), None
        xs, cs = _merge(xs, cs, mod_all, w, ya, yb, yc, yf, yr, upd, layer), None
    return xs
```

```python
import functools

import numpy as np
import jax
import jax.numpy as jnp
from jax import lax
from jax.experimental import pallas as pl
from jax.experimental.pallas import tpu as pltpu

GRID_W = 64
N_BRANCH = 4
HEAD_DIM = 64
MLA_HEADS = 4
MLA_NOPE = 64
MLA_ROPE = 32
MLA_V = 64
MLA_QK = MLA_NOPE + MLA_ROPE
SWA_Q_HEADS = 4
SWA_KV_HEADS = 2
WINDOW = 128
AXA_Q_HEADS = 4
AXA_KV_HEADS = 2
LRU_BLOCKS = 4
LRU_C = 8.0
CONV_W = 4
CONV_LEFT = 2
ROPE_THETA = 10000.0
RMS_EPS = 1e-6
NEG_INF = -1e30
LOG2E = 1.4426950408889634

LANES = 128
SUBLANES = 8
TILE = 256
KEY_CHUNKS = 4
VMEM_LIMIT = 56 * 1024 * 1024

F32 = jnp.float32
BF16 = jnp.bfloat16

VEC_NW, VEC_GCQ, VEC_GAINS, VEC_CONVW, VEC_CONVB, VEC_BR, VEC_BI, VEC_LAM, VEC_ROWS = 0, 1, 2, 3, 7, 8, 10, 12, 16
GAIN_CKV, GAIN_MQ, GAIN_MK, GAIN_SQ, GAIN_SK, GAIN_AQ, GAIN_AK = range(7)


def _dot(a, b):
    return jnp.dot(a, b, preferred_element_type=F32)


def _dot_nt(a, b):
    return lax.dot_general(a, b, (((1,), (1,)), ((), ())), preferred_element_type=F32)


def _split(a):
    hi = a.astype(BF16)
    lo = (a - hi.astype(F32)).astype(BF16)
    return hi, lo


def _dot3(a, w_hi, w_lo):
    a_hi, a_lo = _split(a)
    return _dot(a_hi, w_hi) + _dot(a_lo, w_hi) + _dot(a_hi, w_lo)


def _sigmoid(v):
    return 1.0 / (1.0 + jnp.exp(-v))


def _cparams(sem):
    return pltpu.CompilerParams(dimension_semantics=sem, vmem_limit_bytes=VMEM_LIMIT)


def _layer_spec(arr, layer):
    nd = arr.ndim - 1
    return pl.BlockSpec((1,) + arr.shape[1:], lambda *_: (layer,) + (0,) * nd)


def _gain(vec_ref, slot):
    return vec_ref[0, VEC_GAINS:VEC_GAINS + 1, slot * LANES:(slot + 1) * LANES]


def _mod_kernel(c_ref, w_ref, b_ref, o_ref):
    v = c_ref[...]
    a = v * _sigmoid(v)
    w_hi, w_lo = _split(w_ref[0])
    o_ref[0] = _dot3(a, w_hi, w_lo) + b_ref[0]


def _modulation(cvec, w_mod, b_mod):
    depth, d, d3 = w_mod.shape
    r = cvec.shape[0]
    bn = 512
    return pl.pallas_call(
        _mod_kernel,
        grid=(depth, d3 // bn),
        in_specs=[pl.BlockSpec((r, d), lambda l, j: (0, 0)),
                  pl.BlockSpec((1, d, bn), lambda l, j: (l, 0, j)),
                  pl.BlockSpec((1, 1, bn), lambda l, j: (l, 0, j))],
        out_specs=pl.BlockSpec((1, r, bn), lambda l, j: (l, 0, j)),
        out_shape=jax.ShapeDtypeStruct((depth, r, d3), F32),
        compiler_params=_cparams(("arbitrary", "arbitrary")),
        name="modulation",
    )(cvec, w_mod, b_mod.reshape(depth, 1, d3))


def _modulated_norm(x, mod_ref, vec_ref):
    ms = jnp.mean(x * x, axis=-1, keepdims=True)
    y = x * lax.rsqrt(ms + RMS_EPS) * vec_ref[0, VEC_NW:VEC_NW + 1, :]
    return y * (1.0 + mod_ref[0, 0, 1:2, :]) + mod_ref[0, 0, 0:1, :]


def _stream_tile(x_ref, ctx_ref, is_ctx):
    if ctx_ref is None:
        return x_ref[0]
    return jnp.where(is_ctx, ctx_ref[0], x_ref[0])


def _lane_rinv(v, n):
    return lax.rsqrt(jnp.sum(v * v, axis=-1, keepdims=True) * (1.0 / n) + RMS_EPS)


def _head_rinv(v, gmat):
    hi, lo = _split(v * v)
    ss = _dot(hi, gmat) + _dot(lo, gmat)
    return lax.rsqrt(ss * (1.0 / HEAD_DIM) + RMS_EPS)


def _proj_kernel(*refs, split_ctx):
    x_ref, ctx_ref = (refs[0], refs[1]) if split_ctx else (refs[0], None)
    (mod_ref, vec_ref, wp_ref, wuq_ref, wuqr_ref, wuk_ref, wuv_ref, gmat_ref,
     cosm_ref, sinm_ref, cosh_ref, sinh_ref,
     mq_o, mk_o, mv_o, sq_o, sk_o, sv_o, aq_o, ak_o, av_o, lru_o) = refs[2 if split_ctx else 1:]
    x = _stream_tile(x_ref, ctx_ref, pl.program_id(0) == 0)
    h = _modulated_norm(x, mod_ref, vec_ref)
    z = _dot(h.astype(BF16), wp_ref[0])
    t = x.shape[0]
    lo = lax.broadcasted_iota(jnp.int32, (t, LANES), 1) < HEAD_DIM
    cosm, sinm = cosm_ref[...], sinm_ref[...]
    cosh, sinh = cosh_ref[...], sinh_ref[...]
    gmat = gmat_ref[...]

    def pair_store(out, slab):
        swapped = pltpu.roll(slab, HEAD_DIM, 1)
        out[0, 0] = jnp.where(lo, slab, 0.0).astype(BF16)
        out[0, 1] = jnp.where(lo, 0.0, swapped).astype(BF16)
        out[0, 2] = jnp.where(lo, swapped, 0.0).astype(BF16)
        out[0, 3] = jnp.where(lo, 0.0, slab).astype(BF16)

    g_cq = vec_ref[0, VEC_GCQ:VEC_GCQ + 1, 0:256]
    cq = (z[:, 0:256] * _lane_rinv(z[:, 0:256], 256.0) * g_cq).astype(BF16)
    q = _dot(cq, wuq_ref[0])
    qr = _dot(cq, wuqr_ref[0])
    ckv = (z[:, 256:384] * _lane_rinv(z[:, 256:384], 128.0) * _gain(vec_ref, GAIN_CKV)).astype(BF16)
    kn = _dot(ckv, wuk_ref[0])
    vv = _dot(ckv, wuv_ref[0])
    kr = z[:, 384:512]
    krr = z[:, 512:640]
    gc_q = _gain(vec_ref, GAIN_MQ) * cosm
    gc_k = _gain(vec_ref, GAIN_MK) * cosm
    for hh in range(MLA_HEADS):
        sl = slice(hh * LANES, (hh + 1) * LANES)
        r = _lane_rinv(q[:, sl], float(MLA_QK))
        mq_o[0, hh] = ((q[:, sl] * r) * gc_q + (qr[:, sl] * r) * sinm).astype(BF16)
        ks = kn[:, sl] + kr
        r = _lane_rinv(ks, float(MLA_QK))
        mk_o[0, hh] = ((ks * r) * gc_k + (krr * r) * sinm).astype(BF16)
    mv_o[0] = vv.T.astype(BF16)

    def gqa(c0, g_q, g_k, q_o, k_o, v_o, v_transposed):
        zq, zqr = z[:, c0:c0 + 256], z[:, c0 + 256:c0 + 512]
        r = _head_rinv(zq, gmat)
        gc = _gain(vec_ref, g_q) * cosh
        for i in range(2):
            sl = slice(i * LANES, (i + 1) * LANES)
            q_o[0, i] = ((zq[:, sl] * r[:, sl]) * gc + (zqr[:, sl] * r[:, sl]) * sinh).astype(BF16)
        zk, zkr = z[:, c0 + 512:c0 + 640], z[:, c0 + 640:c0 + 768]
        r = _head_rinv(zk, gmat[:LANES, :LANES])
        ks = (zk * r) * (_gain(vec_ref, g_k) * cosh) + (zkr * r) * sinh
        pair_store(k_o, ks)
        zv = z[:, c0 + 768:c0 + 896]
        if v_transposed:
            v_o[0] = zv.T.astype(BF16)
        else:
            pair_store(v_o, zv)

    gqa(640, GAIN_SQ, GAIN_SK, sq_o, sk_o, sv_o, False)
    gqa(1536, GAIN_AQ, GAIN_AK, aq_o, ak_o, av_o, True)
    lru_o[0] = z[:, 2432:2688]


def _project(x, ctx, mod_all, w, tabs, layer):
    split_ctx = ctx is not None
    b, d = x.shape[0], x.shape[2]
    s = x.shape[1] + (ctx.shape[1] if split_ctx else 0)
    nt = s // TILE
    ctx_row = b

    def tile_spec(width):
        return pl.BlockSpec((1, TILE, width), lambda t, i: (i, t, 0))

    def head_spec(nh):
        return pl.BlockSpec((1, nh, TILE, LANES), lambda t, i: (i, 0, t, 0))

    def head_spec_t(nh):
        return pl.BlockSpec((1, nh * HEAD_DIM, TILE), lambda t, i: (i, 0, t))

    if split_ctx:
        streams = [x, ctx]
        stream_specs = [pl.BlockSpec((1, TILE, d), lambda t, i: (i, jnp.maximum(t - 1, 0), 0)),
                        pl.BlockSpec((1, TILE, d), lambda t, i: (jnp.where(t == 0, i, b - 1), 0, 0))]
    else:
        streams, stream_specs = [x], [tile_spec(d)]
    tab_spec = pl.BlockSpec((TILE, LANES), lambda t, i: (t, 0))
    consts = [w["vecs"], w["wp"], w["wuq"], w["wuqr"], w["wuk"], w["wuv"]]
    in_specs = (stream_specs
                + [pl.BlockSpec((1, 1, 3, d), lambda t, i: (layer, jnp.where(t == 0, ctx_row, i), 0, 0))]
                + [_layer_spec(c, layer) for c in consts]
                + [pl.BlockSpec(w["gmat"].shape, lambda t, i: (0, 0))] + [tab_spec] * 4)
    n_slabs = [MLA_HEADS, MLA_HEADS, MLA_HEADS, 2, 4, 4, 2, 4, AXA_KV_HEADS]
    transposed = [False, False, True, False, False, False, False, False, True]
    out_specs = ([head_spec_t(nh) if tr else head_spec(nh) for nh, tr in zip(n_slabs, transposed)]
                 + [tile_spec(4 * HEAD_DIM)])
    out_shape = ([jax.ShapeDtypeStruct((b, nh * HEAD_DIM, s) if tr else (b, nh, s, LANES), BF16)
                  for nh, tr in zip(n_slabs, transposed)]
                 + [jax.ShapeDtypeStruct((b, s, 4 * HEAD_DIM), F32)])
    return pl.pallas_call(
        functools.partial(_proj_kernel, split_ctx=split_ctx),
        grid=(nt, b),
        in_specs=in_specs,
        out_specs=out_specs,
        out_shape=out_shape,
        compiler_params=_cparams(("arbitrary", "arbitrary")),
        name="project",
    )(*streams, mod_all, *consts, w["gmat"], *tabs)


def _dense_attn_kernel(*refs, q_share, ctx_tile_first):
    q_refs, (k_ref, v_ref, o_ref) = refs[:-3], refs[-3:]
    s_len = k_ref.shape[2]
    n_heads = k_ref.shape[1]
    v_share = n_heads * HEAD_DIM // v_ref.shape[1]

    def attend(n_keys):
        n_groups = n_keys // LANES
        n_chunks = min(KEY_CHUNKS, n_groups)
        edges = [(c * n_groups // n_chunks) * LANES for c in range(n_chunks + 1)]
        chunks = [slice(a, b) for a, b in zip(edges[:-1], edges[1:])]

        def scores(unit):
            q_ref, hh = unit
            q = q_ref[0, hh // q_share]
            return [_dot_nt(k_ref[0, hh, sl, :], q) for sl in chunks]

        def head_out(unit, st):
            hh = unit[1]
            m = functools.reduce(jnp.maximum, [jnp.max(sc, axis=0, keepdims=True) for sc in st])
            pt = [jnp.exp2(sc - m) for sc in st]
            l = sum(jnp.sum(pc, axis=0, keepdims=True) for pc in pt)
            band = slice(hh // v_share * HEAD_DIM, (hh // v_share + 1) * HEAD_DIM)
            ot = sum(_dot(v_ref[0, band, sl], pc.astype(BF16)) for sl, pc in zip(chunks, pt))
            return ot / l

        units = [(q_ref, hh) for q_ref in q_refs for hh in range(n_heads)]
        st_next = scores(units[0])
        outs = []
        for u, unit in enumerate(units):
            st = st_next
            if u + 1 < len(units):
                st_next = scores(units[u + 1])
            outs.append(head_out(unit, st))
        for j in range(len(q_refs)):
            o = outs[j * n_heads:(j + 1) * n_heads]
            o_ref[0, j * TILE:(j + 1) * TILE, :] = jnp.concatenate(o, axis=0).T

    if ctx_tile_first:
        pl.when(pl.program_id(1) == 0)(lambda: attend(TILE))
        pl.when(pl.program_id(1) > 0)(lambda: attend(s_len))
    else:
        attend(s_len)


def _dense_attention(q, k, v, with_ctx_queries):
    b, nq_slabs, s, _ = q.shape
    nh = k.shape[1]
    nt = s // TILE
    off = 0 if with_ctx_queries else 1
    tps = 1 if with_ctx_queries or (nt - off) % 2 else 2
    steps = (nt - off) // tps
    kern = functools.partial(_dense_attn_kernel, q_share=nh // nq_slabs,
                             ctx_tile_first=with_ctx_queries)

    def q_spec(j):
        return pl.BlockSpec((1, nq_slabs, TILE, LANES), lambda i, t: (i, 0, tps * t + j + off, 0))

    return pl.pallas_call(
        kern,
        grid=(b, steps),
        in_specs=[q_spec(j) for j in range(tps)]
                 + [pl.BlockSpec((1, nh, s, LANES), lambda i, t: (i, 0, 0, 0)),
                    pl.BlockSpec((1,) + v.shape[1:], lambda i, t: (i, 0, 0))],
        out_specs=pl.BlockSpec((1, tps * TILE, 2 * LANES), lambda i, t: (i, t, 0)),
        out_shape=jax.ShapeDtypeStruct((b, steps * tps * TILE, 2 * LANES), F32),
        compiler_params=_cparams(("arbitrary", "arbitrary")),
        name="dense_attention",
    )(*([q] * tps), k, v)


def _window_attn_kernel(sink_ref, q_ref, k_ref, v_ref, o_ref, *, tile_off, layer):
    s_len = k_ref.shape[2]
    n_tiles = s_len // TILE
    ti = pl.program_id(1) + tile_off
    is_lat = ti > 0
    t0 = ti * TILE
    half = TILE // 2
    left0 = pl.multiple_of(jnp.maximum(t0 - half, 0), half)
    cen0 = pl.multiple_of(t0, TILE)
    right0 = pl.multiple_of(jnp.minimum(t0 + TILE, s_len - half), half)
    r_c = lax.broadcasted_iota(jnp.int32, (TILE, TILE), 0)
    c_c = lax.broadcasted_iota(jnp.int32, (TILE, TILE), 1)
    r_s = lax.broadcasted_iota(jnp.int32, (TILE, half), 0)
    c_s = lax.broadcasted_iota(jnp.int32, (TILE, half), 1)
    ok_cen = (jnp.abs(r_c - c_c) <= WINDOW) & is_lat
    ok_left = ((r_s - c_s + half) <= WINDOW) & (ti > 1)
    ok_right = ((c_s - r_s + TILE) <= WINDOW) & is_lat & (ti < n_tiles - 1)
    def lane_groups(a):
        return [a[:, g * LANES:(g + 1) * LANES] for g in range(a.shape[1] // LANES)]

    def scores(hh):
        q = q_ref[0, hh // 2]
        k = lambda start, size: k_ref[0, hh, pl.ds(start, size), :]
        return (_dot_nt(q, k(0, TILE)),
                jnp.where(ok_left, _dot_nt(q, k(left0, half)), NEG_INF),
                jnp.where(ok_cen, _dot_nt(q, k(cen0, TILE)), NEG_INF),
                jnp.where(ok_right, _dot_nt(q, k(right0, half)), NEG_INF))

    def head_out(hh, sc):
        sink = sink_ref[layer, hh] * LOG2E
        m = functools.reduce(jnp.maximum, [g for piece in sc for g in lane_groups(piece)])
        m = jnp.maximum(jnp.max(m, -1, keepdims=True), sink)
        ps = [jnp.exp2(piece - m) for piece in sc]
        den = sum(g for piece in ps for g in lane_groups(piece))
        den = jnp.sum(den, -1, keepdims=True) + jnp.exp2(sink - m)
        v = lambda start, size: v_ref[0, hh, pl.ds(start, size), :]
        vs = (v(0, TILE), v(left0, half), v(cen0, TILE), v(right0, half))
        num = sum(_dot(p.astype(BF16), vv) for p, vv in zip(ps, vs))
        return num / den

    n_heads = k_ref.shape[1]
    sc_next = scores(0)
    outs = []
    for hh in range(n_heads):
        sc = sc_next
        if hh + 1 < n_heads:
            sc_next = scores(hh + 1)
        outs.append(head_out(hh, sc))
    o_ref[0] = jnp.concatenate([outs[0] + outs[1], outs[2] + outs[3]], axis=-1)


def _window_attention(sink, q, k, v, with_ctx_queries, layer):
    b, nq_slabs, s, _ = q.shape
    nh = k.shape[1]
    nt = s // TILE
    off = 0 if with_ctx_queries else 1
    nq = nt - off
    kern = functools.partial(_window_attn_kernel, tile_off=off, layer=layer)
    return pl.pallas_call(
        kern,
        grid=(b, nq),
        in_specs=[pl.BlockSpec(memory_space=pltpu.SMEM),
                  pl.BlockSpec((1, nq_slabs, TILE, LANES), lambda i, t: (i, 0, t + off, 0)),
                  pl.BlockSpec((1, nh, s, LANES), lambda i, t: (i, 0, 0, 0)),
                  pl.BlockSpec((1, nh, s, LANES), lambda i, t: (i, 0, 0, 0))],
        out_specs=pl.BlockSpec((1, TILE, 2 * LANES), lambda i, t: (i, t, 0)),
        out_shape=jax.ShapeDtypeStruct((b, nq * TILE, 2 * LANES), F32),
        compiler_params=_cparams(("arbitrary", "arbitrary")),
        name="window_attention",
    )(sink, q, k, v)


def _lru_conv(prev_ref, cur_ref, next_ref, prev_ok, next_ok, vec_ref):
    t = cur_ref.shape[1]
    prev = jnp.where(prev_ok, prev_ref[0], 0.0)
    nxt = jnp.where(next_ok, next_ref[0], 0.0)
    ext = jnp.concatenate([prev, cur_ref[0], nxt], axis=0)
    n = t + 2 * SUBLANES
    c = cur_ref.shape[2]

    def tap(j):
        return vec_ref[0, VEC_CONVW + j:VEC_CONVW + j + 1, 0:c]

    u = vec_ref[0, VEC_CONVB:VEC_CONVB + 1, 0:c] + cur_ref[0] * tap(CONV_LEFT)
    for j in range(CONV_W):
        off = j - CONV_LEFT
        if off == 0:
            continue
        sh = pltpu.roll(ext, (-off) % n, 0)[SUBLANES:SUBLANES + t]
        u = u + sh * tap(j)
    return u


def _lru_scan_tile(u, d, wr_hi, wr_lo, wi_hi, wi_lo, vec_ref, h0, reverse):
    t, c = u.shape

    def vec(row):
        return vec_ref[0, row + d:row + d + 1, 0:c]

    r = _sigmoid(_dot3(u, wr_hi[0, d], wr_lo[0, d]) + vec(VEC_BR))
    i = _sigmoid(_dot3(u, wi_hi[0, d], wi_lo[0, d]) + vec(VEC_BI))
    lam = vec(VEC_LAM)
    softplus_neg = jnp.maximum(-lam, 0.0) + jnp.log1p(jnp.exp(-jnp.abs(lam)))
    log_a = -LRU_C * r * softplus_neg
    a = jnp.exp(log_a)
    bx = jnp.sqrt(1.0 - a * a) * (i * u)
    row = lax.broadcasted_iota(jnp.int32, u.shape, 0)
    step = 1
    while step < t:
        if reverse:
            ok = row < t - step
            a_sh = jnp.where(ok, pltpu.roll(a, t - step, 0), 1.0)
            b_sh = jnp.where(ok, pltpu.roll(bx, t - step, 0), 0.0)
        else:
            ok = row >= step
            a_sh = jnp.where(ok, pltpu.roll(a, step, 0), 1.0)
            b_sh = jnp.where(ok, pltpu.roll(bx, step, 0), 0.0)
        bx = a * b_sh + bx
        a = a * a_sh
        step *= 2
    h = a * h0 + bx
    h_end = h[0:1, :] if reverse else h[t - 1:t, :]
    return h, h_end


def _lru_kernel(fp_ref, fc_ref, fn_ref, rp_ref, rc_ref, rn_ref, vec_ref,
                wr_hi, wr_lo, wi_hi, wi_lo, yf_ref, yb_ref, hf_s, hb_s):
    j = pl.program_id(1)
    nt = pl.num_programs(1)

    @pl.when(j == 0)
    def _():
        hf_s[...] = jnp.zeros_like(hf_s)
        hb_s[...] = jnp.zeros_like(hb_s)

    def bounds(ti):
        return ti >= 2, (ti >= 1) & (ti < nt - 1)

    tf = j
    tr = jnp.where(j == 0, 0, nt - j)
    p_ok, n_ok = bounds(tf)
    uf = _lru_conv(fp_ref, fc_ref, fn_ref, p_ok, n_ok, vec_ref)
    h, h_end = _lru_scan_tile(uf, 0, wr_hi, wr_lo, wi_hi, wi_lo, vec_ref, hf_s[...], False)
    yf_ref[0] = h
    hf_s[...] = h_end
    p_ok, n_ok = bounds(tr)
    ur = _lru_conv(rp_ref, rc_ref, rn_ref, p_ok, n_ok, vec_ref)
    h, h_end = _lru_scan_tile(ur, 1, wr_hi, wr_lo, wi_hi, wi_lo, vec_ref, hb_s[...], True)
    yb_ref[0] = h
    hb_s[...] = h_end


def _rglru(z, w, layer):
    b, s, c = z.shape
    nt = s // TILE
    r8 = TILE // SUBLANES
    n8 = s // SUBLANES

    def rev(j):
        return jnp.where(j == 0, 0, nt - j)

    def cur(f):
        return pl.BlockSpec((1, TILE, c), lambda i, j: (i, f(j), 0))

    def prev(f):
        return pl.BlockSpec((1, SUBLANES, c), lambda i, j: (i, jnp.maximum(f(j) * r8 - 1, 0), 0))

    def nxt(f):
        return pl.BlockSpec((1, SUBLANES, c), lambda i, j: (i, jnp.minimum((f(j) + 1) * r8, n8 - 1), 0))

    ident = lambda j: j
    consts = [w["vecs"], w["wr_hi"], w["wr_lo"], w["wi_hi"], w["wi_lo"]]
    return pl.pallas_call(
        _lru_kernel,
        grid=(b, nt),
        in_specs=[prev(ident), cur(ident), nxt(ident), prev(rev), cur(rev), nxt(rev)]
                 + [_layer_spec(x, layer) for x in consts],
        out_specs=[cur(ident), cur(rev)],
        out_shape=[jax.ShapeDtypeStruct((b, s, c), F32)] * 2,
        scratch_shapes=[pltpu.VMEM((1, c), F32), pltpu.VMEM((1, c), F32)],
        compiler_params=_cparams(("arbitrary", "arbitrary")),
        name="rglru",
    )(z, z, z, z, z, z, *consts)


def _merge_kernel(*refs, split_ctx):
    x_ref, ctx_ref = (refs[0], refs[1]) if split_ctx else (refs[0], None)
    (mod_ref, vec_ref, wg_ref, wm_ref, ya_ref, yb_ref, yc_ref, yf_ref, yr_ref,
     wb_ref, wo_ref, o_ref) = refs[2 if split_ctx else 1:]
    x = _stream_tile(x_ref, ctx_ref, pl.program_id(1) == 0)
    hb = _modulated_norm(x, mod_ref, vec_ref).astype(BF16)
    zg = _dot(hb, wg_ref[0])
    d = x.shape[1]
    bw = ya_ref.shape[2]
    branches = (ya_ref[0], yb_ref[0], yc_ref[0], yf_ref[0] + yr_ref[0])
    mix = jnp.zeros_like(x)
    for k in range(N_BRANCH):
        g = zg[:, k * bw:(k + 1) * bw]
        y = branches[k] * (g * _sigmoid(g))
        proj = _dot(y.astype(BF16), wb_ref[0, k])
        zm = _dot(hb, wm_ref[0, :, k * d:(k + 1) * d])
        mix = mix + _sigmoid(zm) * proj
    o_ref[0] = x + mod_ref[0, 0, 2:3, :] * _dot(mix.astype(BF16), wo_ref[0])


def _merge(x, ctx, mod_all, w, ya, yb, yc, yf, yr, with_ctx, layer):
    split_ctx = ctx is not None
    assert with_ctx or not split_ctx
    b, d = x.shape[0], x.shape[2]
    s = x.shape[1] + (ctx.shape[1] if split_ctx else 0)
    nt = s // TILE
    off = 0 if with_ctx else 1
    nq = nt - off
    ctx_row = b
    bw = ya.shape[2]

    def stream(width):
        return pl.BlockSpec((1, TILE, width), lambda i, t: (i, t + off, 0))

    def local(width):
        return pl.BlockSpec((1, TILE, width), lambda i, t: (i, t, 0))

    if split_ctx:
        streams = [x, ctx]
        stream_specs = [pl.BlockSpec((1, TILE, d), lambda i, t: (i, jnp.maximum(t - 1, 0), 0)),
                        pl.BlockSpec((1, TILE, d), lambda i, t: (i, 0, 0))]
    else:
        streams, stream_specs = [x], [stream(d)]
    consts_a = [w["vecs"], w["wg"], w["wm"]]
    consts_b = [w["wb"], w["wo"]]
    return pl.pallas_call(
        functools.partial(_merge_kernel, split_ctx=split_ctx),
        grid=(b, nq),
        in_specs=stream_specs
                 + [pl.BlockSpec((1, 1, 3, d), lambda i, t: (layer, jnp.where(t + off == 0, ctx_row, i), 0, 0))]
                 + [_layer_spec(c, layer) for c in consts_a]
                 + [local(bw), local(bw), local(bw), stream(bw), stream(bw)]
                 + [_layer_spec(c, layer) for c in consts_b],
        out_specs=local(d),
        out_shape=jax.ShapeDtypeStruct((b, nq * TILE, d), F32),
        compiler_params=_cparams(("arbitrary", "arbitrary")),
        name="merge",
    )(*streams, mod_all, *consts_a, ya, yb, yc, yf, yr, *consts_b)


def _rope_tables(n, n_ctx):
    def axial(rot_dim):
        n_rows = n // GRID_W
        rows = np.repeat(np.arange(n_rows, dtype=np.float64), GRID_W)
        cols = np.tile(np.arange(GRID_W, dtype=np.float64), n_rows)
        quarter = rot_dim // 4
        freqs = ROPE_THETA ** (-np.arange(quarter, dtype=np.float64) / quarter)
        ang = np.concatenate([rows[:, None] * freqs, cols[:, None] * freqs], axis=-1)
        return np.cos(ang), np.sin(ang)

    def with_ctx(cos, sin):
        return (jnp.asarray(np.concatenate([np.ones((n_ctx, LANES)), cos], axis=0), F32),
                jnp.asarray(np.concatenate([np.zeros((n_ctx, LANES)), sin], axis=0), F32))

    cm, sm = axial(MLA_ROPE)
    one, zero = np.ones((n, MLA_NOPE)), np.zeros((n, MLA_NOPE))
    pad1, pad0 = np.ones((n, LANES - MLA_QK)), np.zeros((n, LANES - MLA_QK))
    cosm, sinm = with_ctx(np.concatenate([one, cm, cm, pad1], axis=-1),
                          np.concatenate([zero, sm, sm, pad0], axis=-1))
    ch, sh = axial(HEAD_DIM)
    cosh, sinh = with_ctx(np.tile(ch, (1, 4)), np.tile(sh, (1, 4)))
    return cosm, sinm, cosh, sinh


def _rot_fold(w, gain, n_heads, half):
    l, k, c = w.shape
    wg = (w * gain[:, None, :]).reshape(l, k, n_heads, 2, half)
    return jnp.concatenate([-wg[:, :, :, 1], wg[:, :, :, 0]], axis=3).reshape(l, k, c)


def _prepare_weights(p):
    w_in = p["w_in"]
    depth, d, _ = w_in.shape
    sizes = (256, 128, 32, 256, 128, 128, 256, 128, 128, 256, 1024, 4 * d)
    offs = [0]
    for sz in sizes:
        offs.append(offs[-1] + sz)
    cq, ckv, kr, sq, sk, sv, aq, ak, av, lru = (
        w_in[:, :, offs[i]:offs[i + 1]] for i in range(10))
    w_in16 = w_in.astype(BF16)
    gate, merge = (w_in16[:, :, offs[i]:offs[i + 1]] for i in (10, 11))

    q_scale = HEAD_DIM ** -0.5 * LOG2E
    mq_scale = MLA_QK ** -0.5 * LOG2E
    g_sq = jnp.tile(p["swa_q_norm"] * q_scale, (1, 2))
    g_sk = jnp.tile(p["swa_k_norm"], (1, 2))
    g_aq = jnp.tile(p["axa_q_norm"] * q_scale, (1, 2))
    g_ak = jnp.tile(p["axa_k_norm"], (1, 2))
    g_mq = p["mla_q_norm"] * mq_scale
    g_mk = p["mla_k_norm"]

    def rope_slab(w_rope, g_rope):
        l, k, nh, r = w_rope.shape
        rot = _rot_fold(w_rope.reshape(l, k, nh * r), jnp.tile(g_rope, (1, nh)), nh, r // 2)
        return jnp.pad(rot.reshape(w_rope.shape), ((0, 0), (0, 0), (0, 0), (MLA_NOPE, LANES - MLA_QK)))

    kr_slab = jnp.pad(kr, ((0, 0), (0, 0), (MLA_NOPE, LANES - MLA_QK)))
    krr_slab = rope_slab(kr[:, :, None, :], g_mk[:, MLA_NOPE:])[:, :, 0]
    half = HEAD_DIM // 2
    wp = jnp.concatenate([
        cq, ckv, kr_slab, krr_slab,
        sq, _rot_fold(sq, jnp.tile(g_sq, (1, 2)), SWA_Q_HEADS, half),
        sk, _rot_fold(sk, g_sk, SWA_KV_HEADS, half), sv,
        aq, _rot_fold(aq, jnp.tile(g_aq, (1, 2)), AXA_Q_HEADS, half),
        ak, _rot_fold(ak, g_ak, AXA_KV_HEADS, half), av,
        lru], axis=2).astype(BF16)

    w_uq = p["mla_w_uq"].reshape(depth, -1, MLA_HEADS, MLA_QK)
    w_ukv = p["mla_w_ukv"].reshape(depth, -1, MLA_HEADS, MLA_NOPE + MLA_V)
    pad4 = lambda a, n: jnp.pad(a, ((0, 0), (0, 0), (0, 0), (0, n)))
    wuq = pad4(w_uq, LANES - MLA_QK).reshape(depth, -1, MLA_HEADS * LANES).astype(BF16)
    wuqr = rope_slab(w_uq[..., MLA_NOPE:], g_mq[:, MLA_NOPE:]).reshape(depth, -1, MLA_HEADS * LANES).astype(BF16)
    wuk = pad4(w_ukv[..., :MLA_NOPE], LANES - MLA_NOPE).reshape(depth, -1, MLA_HEADS * LANES).astype(BF16)
    wuv = w_ukv[..., MLA_NOPE:].reshape(depth, -1, MLA_HEADS * MLA_V).astype(BF16)

    def row(v):
        v = v if v.ndim == 3 else v[:, None, :]
        return jnp.pad(v, ((0, 0), (0, 0), (0, d - v.shape[2])))

    slab = lambda g: jnp.pad(g, ((0, 0), (0, LANES - g.shape[1])))
    gains = jnp.concatenate([p["mla_ckv_norm"], slab(g_mq), slab(g_mk), g_sq, g_sk, g_aq, g_ak], axis=1)
    vec_rows = [row(p["norm_w"]), row(p["mla_cq_norm"]), row(gains), row(p["lru_conv_w"]),
                row(p["lru_conv_b"]), row(p["lru_b_r"]), row(p["lru_b_i"]), row(p["lru_lambda"])]
    n_rows = sum(r.shape[1] for r in vec_rows)
    vecs = jnp.concatenate(vec_rows + [jnp.zeros((depth, VEC_ROWS - n_rows, d), F32)], axis=1)

    idx = np.arange(4 * HEAD_DIM) // HEAD_DIM
    gmat = jnp.asarray(idx[:, None] == idx[None, :], BF16)

    def block_diag_hi_lo(wb):
        l, two, k, m, _ = wb.shape
        full = jnp.einsum("ldkij,kn->ldkinj", wb, jnp.eye(k, dtype=wb.dtype)).reshape(l, two, k * m, k * m)
        hi = full.astype(BF16)
        return hi, (full - hi.astype(F32)).astype(BF16)

    wr_hi, wr_lo = block_diag_hi_lo(p["lru_w_r"])
    wi_hi, wi_lo = block_diag_hi_lo(p["lru_w_i"])
    return {
        "vecs": vecs, "gmat": gmat,
        "wp": wp, "wuq": wuq, "wuqr": wuqr, "wuk": wuk, "wuv": wuv,
        "wg": gate, "wm": merge,
        "wb": p["w_branch"].astype(BF16), "wo": p["w_out"].astype(BF16),
        "wr_hi": wr_hi, "wr_lo": wr_lo, "wi_hi": wi_hi, "wi_lo": wi_lo,
    }


def kernel(x, c, ctx, c_ctx, w_mod, b_mod, norm_w, w_in, mla_cq_norm, mla_ckv_norm, mla_w_uq, mla_w_ukv, mla_q_norm, mla_k_norm, swa_q_norm, swa_k_norm, swa_sink, axa_q_norm, axa_k_norm, lru_conv_w, lru_conv_b, lru_w_r, lru_b_r, lru_w_i, lru_b_i, lru_lambda, w_branch, w_out):
    p = dict(norm_w=norm_w, w_in=w_in, mla_cq_norm=mla_cq_norm, mla_ckv_norm=mla_ckv_norm,
             mla_w_uq=mla_w_uq, mla_w_ukv=mla_w_ukv, mla_q_norm=mla_q_norm, mla_k_norm=mla_k_norm,
             swa_q_norm=swa_q_norm, swa_k_norm=swa_k_norm,
             axa_q_norm=axa_q_norm, axa_k_norm=axa_k_norm, lru_conv_w=lru_conv_w,
             lru_conv_b=lru_conv_b, lru_w_r=lru_w_r, lru_b_r=lru_b_r, lru_w_i=lru_w_i,
             lru_b_i=lru_b_i, lru_lambda=lru_lambda, w_branch=w_branch, w_out=w_out)
    b, n, d = x.shape
    n_ctx = ctx.shape[1]
    depth = w_mod.shape[0]
    assert n_ctx == TILE and n % TILE == 0 and n % GRID_W == 0 and b + 1 <= SUBLANES

    rows = jnp.concatenate([c, c_ctx[None, :], jnp.zeros((SUBLANES - b - 1, d), F32)], axis=0)
    mod_all = _modulation(rows, w_mod, b_mod).reshape(depth, SUBLANES, 3, d)

    w = _prepare_weights(p)
    tabs = _rope_tables(n, n_ctx)
    xs, cs = x, ctx
    for layer in range(depth):
        upd = layer < depth - 1
        mq, mk, mv, sq, sk, sv, aq, ak, av, zl = _project(xs, cs, mod_all, w, tabs, layer)
        ya = _dense_attention(mq, mk, mv, upd)
        yb = _window_attention(swa_sink, sq, sk, sv, upd, layer)
        yc = _dense_attention(aq, ak, av, upd)
        yf, yr = _rglru(zl, w, layer)
        if cs is not None and not upd:
            xs, cs = jnp.concatenate([cs, xs], axis=1), None
        xs, cs = _merge(xs, cs, mod_all, w, ya, yb, yc, yf, yr, upd, layer), None
    return xs
```

```python
import functools

import numpy as np
import jax
import jax.numpy as jnp
from jax import lax
from jax.experimental import pallas as pl
from jax.experimental.pallas import tpu as pltpu

GRID_W = 64
N_BRANCH = 4
HEAD_DIM = 64
MLA_HEADS = 4
MLA_NOPE = 64
MLA_ROPE = 32
MLA_V = 64
MLA_QK = MLA_NOPE + MLA_ROPE
SWA_Q_HEADS = 4
SWA_KV_HEADS = 2
WINDOW = 128
AXA_Q_HEADS = 4
AXA_KV_HEADS = 2
LRU_BLOCKS = 4
LRU_C = 8.0
CONV_W = 4
CONV_LEFT = 2
ROPE_THETA = 10000.0
RMS_EPS = 1e-6
NEG_INF = -1e30
LOG2E = 1.4426950408889634

LANES = 128
SUBLANES = 8
TILE = 256
KEY_CHUNKS = 4
VMEM_LIMIT = 56 * 1024 * 1024

F32 = jnp.float32
BF16 = jnp.bfloat16

VEC_NW, VEC_GCQ, VEC_GAINS, VEC_CONVW, VEC_CONVB, VEC_BR, VEC_BI, VEC_LAM, VEC_ROWS = 0, 1, 2, 3, 7, 8, 10, 12, 16
GAIN_CKV, GAIN_MQ, GAIN_MK, GAIN_SQ, GAIN_SK, GAIN_AQ, GAIN_AK = range(7)


def _dot(a, b):
    return jnp.dot(a, b, preferred_element_type=F32)


def _dot_nt(a, b):
    return lax.dot_general(a, b, (((1,), (1,)), ((), ())), preferred_element_type=F32)


def _split(a):
    hi = a.astype(BF16)
    lo = (a - hi.astype(F32)).astype(BF16)
    return hi, lo


def _dot3(a, w_hi, w_lo):
    a_hi, a_lo = _split(a)
    return _dot(a_hi, w_hi) + _dot(a_lo, w_hi) + _dot(a_hi, w_lo)


def _sigmoid(v):
    return 1.0 / (1.0 + jnp.exp(-v))


def _cparams(sem):
    return pltpu.CompilerParams(dimension_semantics=sem, vmem_limit_bytes=VMEM_LIMIT)


def _layer_spec(arr, layer):
    nd = arr.ndim - 1
    return pl.BlockSpec((1,) + arr.shape[1:], lambda *_: (layer,) + (0,) * nd)


def _gain(vec_ref, slot):
    return vec_ref[0, VEC_GAINS:VEC_GAINS + 1, slot * LANES:(slot + 1) * LANES]


def _mod_kernel(c_ref, w_ref, b_ref, o_ref):
    v = c_ref[...]
    a = v * _sigmoid(v)
    w_hi, w_lo = _split(w_ref[0])
    o_ref[0] = _dot3(a, w_hi, w_lo) + b_ref[0]


def _modulation(cvec, w_mod, b_mod):
    depth, d, d3 = w_mod.shape
    r = cvec.shape[0]
    bn = 512
    return pl.pallas_call(
        _mod_kernel,
        grid=(depth, d3 // bn),
        in_specs=[pl.BlockSpec((r, d), lambda l, j: (0, 0)),
                  pl.BlockSpec((1, d, bn), lambda l, j: (l, 0, j)),
                  pl.BlockSpec((1, 1, bn), lambda l, j: (l, 0, j))],
        out_specs=pl.BlockSpec((1, r, bn), lambda l, j: (l, 0, j)),
        out_shape=jax.ShapeDtypeStruct((depth, r, d3), F32),
        compiler_params=_cparams(("arbitrary", "arbitrary")),
        name="modulation",
    )(cvec, w_mod, b_mod.reshape(depth, 1, d3))


def _modulated_norm(x, mod_ref, vec_ref):
    ms = jnp.mean(x * x, axis=-1, keepdims=True)
    y = x * lax.rsqrt(ms + RMS_EPS) * vec_ref[0, VEC_NW:VEC_NW + 1, :]
    return y * (1.0 + mod_ref[0, 0, 1:2, :]) + mod_ref[0, 0, 0:1, :]


def _stream_tile(x_ref, ctx_ref, is_ctx):
    if ctx_ref is None:
        return x_ref[0]
    return jnp.where(is_ctx, ctx_ref[0], x_ref[0])


def _lane_rinv(v, n):
    return lax.rsqrt(jnp.sum(v * v, axis=-1, keepdims=True) * (1.0 / n) + RMS_EPS)


def _head_rinv(v, gmat):
    hi, lo = _split(v * v)
    ss = _dot(hi, gmat) + _dot(lo, gmat)
    return lax.rsqrt(ss * (1.0 / HEAD_DIM) + RMS_EPS)


def _proj_kernel(*refs, split_ctx):
    x_ref, ctx_ref = (refs[0], refs[1]) if split_ctx else (refs[0], None)
    (mod_ref, vec_ref, wp_ref, wuq_ref, wuqr_ref, wuk_ref, wuv_ref, gmat_ref,
     cosm_ref, sinm_ref, cosh_ref, sinh_ref,
     mq_o, mk_o, mv_o, sq_o, sk_o, sv_o, aq_o, ak_o, av_o, lru_o) = refs[2 if split_ctx else 1:]
    x = _stream_tile(x_ref, ctx_ref, pl.program_id(0) == 0)
    h = _modulated_norm(x, mod_ref, vec_ref)
    z = _dot(h.astype(BF16), wp_ref[0])
    t = x.shape[0]
    lo = lax.broadcasted_iota(jnp.int32, (t, LANES), 1) < HEAD_DIM
    cosm, sinm = cosm_ref[...], sinm_ref[...]
    cosh, sinh = cosh_ref[...], sinh_ref[...]
    gmat = gmat_ref[...]

    def pair_store(out, slab):
        swapped = pltpu.roll(slab, HEAD_DIM, 1)
        out[0, 0] = jnp.where(lo, slab, 0.0).astype(BF16)
        out[0, 1] = jnp.where(lo, 0.0, swapped).astype(BF16)
        out[0, 2] = jnp.where(lo, swapped, 0.0).astype(BF16)
        out[0, 3] = jnp.where(lo, 0.0, slab).astype(BF16)

    g_cq = vec_ref[0, VEC_GCQ:VEC_GCQ + 1, 0:256]
    cq = (z[:, 0:256] * _lane_rinv(z[:, 0:256], 256.0) * g_cq).astype(BF16)
    q = _dot(cq, wuq_ref[0])
    qr = _dot(cq, wuqr_ref[0])
    ckv = (z[:, 256:384] * _lane_rinv(z[:, 256:384], 128.0) * _gain(vec_ref, GAIN_CKV)).astype(BF16)
    kn = _dot(ckv, wuk_ref[0])
    vv = _dot(ckv, wuv_ref[0])
    kr = z[:, 384:512]
    krr = z[:, 512:640]
    gc_q = _gain(vec_ref, GAIN_MQ) * cosm
    gc_k = _gain(vec_ref, GAIN_MK) * cosm
    for hh in range(MLA_HEADS):
        sl = slice(hh * LANES, (hh + 1) * LANES)
        r = _lane_rinv(q[:, sl], float(MLA_QK))
        mq_o[0, hh] = ((q[:, sl] * r) * gc_q + (qr[:, sl] * r) * sinm).astype(BF16)
        ks = kn[:, sl] + kr
        r = _lane_rinv(ks, float(MLA_QK))
        mk_o[0, hh] = ((ks * r) * gc_k + (krr * r) * sinm).astype(BF16)
    mv_o[0] = vv.T.astype(BF16)

    def gqa(c0, g_q, g_k, q_o, k_o, v_o, v_transposed):
        zq, zqr = z[:, c0:c0 + 256], z[:, c0 + 256:c0 + 512]
        r = _head_rinv(zq, gmat)
        gc = _gain(vec_ref, g_q) * cosh
        for i in range(2):
            sl = slice(i * LANES, (i + 1) * LANES)
            q_o[0, i] = ((zq[:, sl] * r[:, sl]) * gc + (zqr[:, sl] * r[:, sl]) * sinh).astype(BF16)
        zk, zkr = z[:, c0 + 512:c0 + 640], z[:, c0 + 640:c0 + 768]
        r = _head_rinv(zk, gmat[:LANES, :LANES])
        ks = (zk * r) * (_gain(vec_ref, g_k) * cosh) + (zkr * r) * sinh
        pair_store(k_o, ks)
        zv = z[:, c0 + 768:c0 + 896]
        if v_transposed:
            v_o[0] = zv.T.astype(BF16)
        else:
            pair_store(v_o, zv)

    gqa(640, GAIN_SQ, GAIN_SK, sq_o, sk_o, sv_o, False)
    gqa(1536, GAIN_AQ, GAIN_AK, aq_o, ak_o, av_o, True)
    lru_o[0] = z[:, 2432:2688]


def _project(x, ctx, mod_all, w, tabs, layer):
    split_ctx = ctx is not None
    b, d = x.shape[0], x.shape[2]
    s = x.shape[1] + (ctx.shape[1] if split_ctx else 0)
    nt = s // TILE
    ctx_row = b

    def tile_spec(width):
        return pl.BlockSpec((1, TILE, width), lambda t, i: (i, t, 0))

    def head_spec(nh):
        return pl.BlockSpec((1, nh, TILE, LANES), lambda t, i: (i, 0, t, 0))

    def head_spec_t(nh):
        return pl.BlockSpec((1, nh * HEAD_DIM, TILE), lambda t, i: (i, 0, t))

    if split_ctx:
        streams = [x, ctx]
        stream_specs = [pl.BlockSpec((1, TILE, d), lambda t, i: (i, jnp.maximum(t - 1, 0), 0)),
                        pl.BlockSpec((1, TILE, d), lambda t, i: (jnp.where(t == 0, i, b - 1), 0, 0))]
    else:
        streams, stream_specs = [x], [tile_spec(d)]
    tab_spec = pl.BlockSpec((TILE, LANES), lambda t, i: (t, 0))
    consts = [w["vecs"], w["wp"], w["wuq"], w["wuqr"], w["wuk"], w["wuv"]]
    in_specs = (stream_specs
                + [pl.BlockSpec((1, 1, 3, d), lambda t, i: (layer, jnp.where(t == 0, ctx_row, i), 0, 0))]
                + [_layer_spec(c, layer) for c in consts]
                + [pl.BlockSpec(w["gmat"].shape, lambda t, i: (0, 0))] + [tab_spec] * 4)
    n_slabs = [MLA_HEADS, MLA_HEADS, MLA_HEADS, 2, 4, 4, 2, 4, AXA_KV_HEADS]
    transposed = [False, False, True, False, False, False, False, False, True]
    out_specs = ([head_spec_t(nh) if tr else head_spec(nh) for nh, tr in zip(n_slabs, transposed)]
                 + [tile_spec(4 * HEAD_DIM)])
    out_shape = ([jax.ShapeDtypeStruct((b, nh * HEAD_DIM, s) if tr else (b, nh, s, LANES), BF16)
                  for nh, tr in zip(n_slabs, transposed)]
                 + [jax.ShapeDtypeStruct((b, s, 4 * HEAD_DIM), F32)])
    return pl.pallas_call(
        functools.partial(_proj_kernel, split_ctx=split_ctx),
        grid=(nt, b),
        in_specs=in_specs,
        out_specs=out_specs,
        out_shape=out_shape,
        compiler_params=_cparams(("arbitrary", "arbitrary")),
        name="project",
    )(*streams, mod_all, *consts, w["gmat"], *tabs)


def _dense_attn_kernel(*refs, q_share, ctx_tile_first):
    q_refs, (k_ref, v_ref, o_ref) = refs[:-3], refs[-3:]
    s_len = k_ref.shape[2]
    n_heads = k_ref.shape[1]
    v_share = n_heads * HEAD_DIM // v_ref.shape[1]

    def attend(n_keys):
        n_groups = n_keys // LANES
        n_chunks = min(KEY_CHUNKS, n_groups)
        edges = [(c * n_groups // n_chunks) * LANES for c in range(n_chunks + 1)]
        chunks = [slice(a, b) for a, b in zip(edges[:-1], edges[1:])]

        def scores(unit):
            q_ref, hh = unit
            q = q_ref[0, hh // q_share]
            return [_dot_nt(k_ref[0, hh, sl, :], q) for sl in chunks]

        def head_out(unit, st):
            hh = unit[1]
            m = functools.reduce(jnp.maximum, [jnp.max(sc, axis=0, keepdims=True) for sc in st])
            pt = [jnp.exp2(sc - m) for sc in st]
            l = sum(jnp.sum(pc, axis=0, keepdims=True) for pc in pt)
            band = slice(hh // v_share * HEAD_DIM, (hh // v_share + 1) * HEAD_DIM)
            ot = sum(_dot(v_ref[0, band, sl], pc.astype(BF16)) for sl, pc in zip(chunks, pt))
            return ot / l

        units = [(q_ref, hh) for q_ref in q_refs for hh in range(n_heads)]
        st_next = scores(units[0])
        outs = []
        for u, unit in enumerate(units):
            st = st_next
            if u + 1 < len(units):
                st_next = scores(units[u + 1])
            outs.append(head_out(unit, st))
        for j in range(len(q_refs)):
            o = outs[j * n_heads:(j + 1) * n_heads]
            o_ref[0, j * TILE:(j + 1) * TILE, :] = jnp.concatenate(o, axis=0).T

    if ctx_tile_first:
        pl.when(pl.program_id(1) == 0)(lambda: attend(TILE))
        pl.when(pl.program_id(1) > 0)(lambda: attend(s_len))
    else:
        attend(s_len)


def _dense_attention(q, k, v, with_ctx_queries):
    b, nq_slabs, s, _ = q.shape
    nh = k.shape[1]
    nt = s // TILE
    off = 0 if with_ctx_queries else 1
    tps = 1 if with_ctx_queries or (nt - off) % 2 else 2
    steps = (nt - off) // tps
    kern = functools.partial(_dense_attn_kernel, q_share=nh // nq_slabs,
                             ctx_tile_first=with_ctx_queries)

    def q_spec(j):
        return pl.BlockSpec((1, nq_slabs, TILE, LANES), lambda i, t: (i, 0, tps * t + j + off, 0))

    return pl.pallas_call(
        kern,
        grid=(b, steps),
        in_specs=[q_spec(j) for j in range(tps)]
                 + [pl.BlockSpec((1, nh, s, LANES), lambda i, t: (i, 0, 0, 0)),
                    pl.BlockSpec((1,) + v.shape[1:], lambda i, t: (i, 0, 0))],
        out_specs=pl.BlockSpec((1, tps * TILE, 2 * LANES), lambda i, t: (i, t, 0)),
        out_shape=jax.ShapeDtypeStruct((b, steps * tps * TILE, 2 * LANES), F32),
        compiler_params=_cparams(("arbitrary", "arbitrary")),
        name="dense_attention",
    )(*([q] * tps), k, v)


def _window_attn_kernel(sink_ref, q_ref, k_ref, v_ref, o_ref, *, tile_off, layer):
    s_len = k_ref.shape[2]
    n_tiles = s_len // TILE
    ti = pl.program_id(1) + tile_off
    is_lat = ti > 0
    t0 = ti * TILE
    half = TILE // 2
    left0 = pl.multiple_of(jnp.maximum(t0 - half, 0), half)
    cen0 = pl.multiple_of(t0, TILE)
    right0 = pl.multiple_of(jnp.minimum(t0 + TILE, s_len - half), half)
    r_c = lax.broadcasted_iota(jnp.int32, (TILE, TILE), 0)
    c_c = lax.broadcasted_iota(jnp.int32, (TILE, TILE), 1)
    r_s = lax.broadcasted_iota(jnp.int32, (TILE, half), 0)
    c_s = lax.broadcasted_iota(jnp.int32, (TILE, half), 1)
    ok_cen = (jnp.abs(r_c - c_c) <= WINDOW) & is_lat
    ok_left = ((r_s - c_s + half) <= WINDOW) & (ti > 1)
    ok_right = ((c_s - r_s + TILE) <= WINDOW) & is_lat & (ti < n_tiles - 1)
    def lane_groups(a):
        return [a[:, g * LANES:(g + 1) * LANES] for g in range(a.shape[1] // LANES)]

    def scores(hh):
        q = q_ref[0, hh // 2]
        k = lambda start, size: k_ref[0, hh, pl.ds(start, size), :]
        return (_dot_nt(q, k(0, TILE)),
                jnp.where(ok_left, _dot_nt(q, k(left0, half)), NEG_INF),
                jnp.where(ok_cen, _dot_nt(q, k(cen0, TILE)), NEG_INF),
                jnp.where(ok_right, _dot_nt(q, k(right0, half)), NEG_INF))

    def head_out(hh, sc):
        sink = sink_ref[layer, hh] * LOG2E
        m = functools.reduce(jnp.maximum, [g for piece in sc for g in lane_groups(piece)])
        m = jnp.maximum(jnp.max(m, -1, keepdims=True), sink)
        ps = [jnp.exp2(piece - m) for piece in sc]
        den = sum(g for piece in ps for g in lane_groups(piece))
        den = jnp.sum(den, -1, keepdims=True) + jnp.exp2(sink - m)
        v = lambda start, size: v_ref[0, hh, pl.ds(start, size), :]
        vs = (v(0, TILE), v(left0, half), v(cen0, TILE), v(right0, half))
        num = sum(_dot(p.astype(BF16), vv) for p, vv in zip(ps, vs))
        return num / den

    n_heads = k_ref.shape[1]
    sc_next = scores(0)
    outs = []
    for hh in range(n_heads):
        sc = sc_next
        if hh + 1 < n_heads:
            sc_next = scores(hh + 1)
        outs.append(head_out(hh, sc))
    o_ref[0] = jnp.concatenate([outs[0] + outs[1], outs[2] + outs[3]], axis=-1)


def _window_attention(sink, q, k, v, with_ctx_queries, layer):
    b, nq_slabs, s, _ = q.shape
    nh = k.shape[1]
    nt = s // TILE
    off = 0 if with_ctx_queries else 1
    nq = nt - off
    kern = functools.partial(_window_attn_kernel, tile_off=off, layer=layer)
    return pl.pallas_call(
        kern,
        grid=(b, nq),
        in_specs=[pl.BlockSpec(memory_space=pltpu.SMEM),
                  pl.BlockSpec((1, nq_slabs, TILE, LANES), lambda i, t: (i, 0, t + off, 0)),
                  pl.BlockSpec((1, nh, s, LANES), lambda i, t: (i, 0, 0, 0)),
                  pl.BlockSpec((1, nh, s, LANES), lambda i, t: (i, 0, 0, 0))],
        out_specs=pl.BlockSpec((1, TILE, 2 * LANES), lambda i, t: (i, t, 0)),
        out_shape=jax.ShapeDtypeStruct((b, nq * TILE, 2 * LANES), F32),
        compiler_params=_cparams(("arbitrary", "arbitrary")),
        name="window_attention",
    )(sink, q, k, v)


def _lru_conv(prev_ref, cur_ref, next_ref, prev_ok, next_ok, vec_ref):
    t = cur_ref.shape[1]
    prev = jnp.where(prev_ok, prev_ref[0], 0.0)
    nxt = jnp.where(next_ok, next_ref[0], 0.0)
    ext = jnp.concatenate([prev, cur_ref[0], nxt], axis=0)
    n = t + 2 * SUBLANES
    c = cur_ref.shape[2]

    def tap(j):
        return vec_ref[0, VEC_CONVW + j:VEC_CONVW + j + 1, 0:c]

    u = vec_ref[0, VEC_CONVB:VEC_CONVB + 1, 0:c] + cur_ref[0] * tap(CONV_LEFT)
    for j in range(CONV_W):
        off = j - CONV_LEFT
        if off == 0:
            continue
        sh = pltpu.roll(ext, (-off) % n, 0)[SUBLANES:SUBLANES + t]
        u = u + sh * tap(j)
    return u


def _lru_scan_tile(u, d, wr_hi, wr_lo, wi_hi, wi_lo, vec_ref, h0, reverse):
    t, c = u.shape

    def vec(row):
        return vec_ref[0, row + d:row + d + 1, 0:c]

    r = _sigmoid(_dot3(u, wr_hi[0, d], wr_lo[0, d]) + vec(VEC_BR))
    i = _sigmoid(_dot3(u, wi_hi[0, d], wi_lo[0, d]) + vec(VEC_BI))
    lam = vec(VEC_LAM)
    softplus_neg = jnp.maximum(-lam, 0.0) + jnp.log1p(jnp.exp(-jnp.abs(lam)))
    log_a = -LRU_C * r * softplus_neg
    a = jnp.exp(log_a)
    bx = jnp.sqrt(1.0 - a * a) * (i * u)
    row = lax.broadcasted_iota(jnp.int32, u.shape, 0)
    step = 1
    while step < t:
        if reverse:
            ok = row < t - step
            a_sh = jnp.where(ok, pltpu.roll(a, t - step, 0), 1.0)
            b_sh = jnp.where(ok, pltpu.roll(bx, t - step, 0), 0.0)
        else:
            ok = row >= step
            a_sh = jnp.where(ok, pltpu.roll(a, step, 0), 1.0)
            b_sh = jnp.where(ok, pltpu.roll(bx, step, 0), 0.0)
        bx = a * b_sh + bx
        a = a * a_sh
        step *= 2
    h = a * h0 + bx
    h_end = h[0:1, :] if reverse else h[t - 1:t, :]
    return h, h_end


def _lru_kernel(fp_ref, fc_ref, fn_ref, rp_ref, rc_ref, rn_ref, vec_ref,
                wr_hi, wr_lo, wi_hi, wi_lo, yf_ref, yb_ref, hf_s, hb_s):
    j = pl.program_id(1)
    nt = pl.num_programs(1)

    @pl.when(j == 0)
    def _():
        hf_s[...] = jnp.zeros_like(hf_s)
        hb_s[...] = jnp.zeros_like(hb_s)

    def bounds(ti):
        return ti >= 2, (ti >= 1) & (ti < nt - 1)

    tf = j
    tr = jnp.where(j == 0, 0, nt - j)
    p_ok, n_ok = bounds(tf)
    uf = _lru_conv(fp_ref, fc_ref, fn_ref, p_ok, n_ok, vec_ref)
    h, h_end = _lru_scan_tile(uf, 0, wr_hi, wr_lo, wi_hi, wi_lo, vec_ref, hf_s[...], False)
    yf_ref[0] = h
    hf_s[...] = h_end
    p_ok, n_ok = bounds(tr)
    ur = _lru_conv(rp_ref, rc_ref, rn_ref, p_ok, n_ok, vec_ref)
    h, h_end = _lru_scan_tile(ur, 1, wr_hi, wr_lo, wi_hi, wi_lo, vec_ref, hb_s[...], True)
    yb_ref[0] = h
    hb_s[...] = h_end


def _rglru(z, w, layer):
    b, s, c = z.shape
    nt = s // TILE
    r8 = TILE // SUBLANES
    n8 = s // SUBLANES

    def rev(j):
        return jnp.where(j == 0, 0, nt - j)

    def cur(f):
        return pl.BlockSpec((1, TILE, c), lambda i, j: (i, f(j), 0))

    def prev(f):
        return pl.BlockSpec((1, SUBLANES, c), lambda i, j: (i, jnp.maximum(f(j) * r8 - 1, 0), 0))

    def nxt(f):
        return pl.BlockSpec((1, SUBLANES, c), lambda i, j: (i, jnp.minimum((f(j) + 1) * r8, n8 - 1), 0))

    ident = lambda j: j
    consts = [w["vecs"], w["wr_hi"], w["wr_lo"], w["wi_hi"], w["wi_lo"]]
    return pl.pallas_call(
        _lru_kernel,
        grid=(b, nt),
        in_specs=[prev(ident), cur(ident), nxt(ident), prev(rev), cur(rev), nxt(rev)]
                 + [_layer_spec(x, layer) for x in consts],
        out_specs=[cur(ident), cur(rev)],
        out_shape=[jax.ShapeDtypeStruct((b, s, c), F32)] * 2,
        scratch_shapes=[pltpu.VMEM((1, c), F32), pltpu.VMEM((1, c), F32)],
        compiler_params=_cparams(("arbitrary", "arbitrary")),
        name="rglru",
    )(z, z, z, z, z, z, *consts)


def _merge_kernel(*refs, split_ctx):
    x_ref, ctx_ref = (refs[0], refs[1]) if split_ctx else (refs[0], None)
    (mod_ref, vec_ref, wg_ref, wm_ref, ya_ref, yb_ref, yc_ref, yf_ref, yr_ref,
     wb_ref, wo_ref, o_ref) = refs[2 if split_ctx else 1:]
    x = _stream_tile(x_ref, ctx_ref, pl.program_id(1) == 0)
    hb = _modulated_norm(x, mod_ref, vec_ref).astype(BF16)
    zg = _dot(hb, wg_ref[0])
    d = x.shape[1]
    bw = ya_ref.shape[2]
    branches = (ya_ref[0], yb_ref[0], yc_ref[0], yf_ref[0] + yr_ref[0])
    mix = jnp.zeros_like(x)
    for k in range(N_BRANCH):
        g = zg[:, k * bw:(k + 1) * bw]
        y = branches[k] * (g * _sigmoid(g))
        proj = _dot(y.astype(BF16), wb_ref[0, k])
        zm = _dot(hb, wm_ref[0, :, k * d:(k + 1) * d])
        mix = mix + _sigmoid(zm) * proj
    o_ref[0] = x + mod_ref[0, 0, 2:3, :] * _dot(mix.astype(BF16), wo_ref[0])


def _merge(x, ctx, mod_all, w, ya, yb, yc, yf, yr, with_ctx, layer):
    split_ctx = ctx is not None
    assert with_ctx or not split_ctx
    b, d = x.shape[0], x.shape[2]
    s = x.shape[1] + (ctx.shape[1] if split_ctx else 0)
    nt = s // TILE
    off = 0 if with_ctx else 1
    nq = nt - off
    ctx_row = b
    bw = ya.shape[2]

    def stream(width):
        return pl.BlockSpec((1, TILE, width), lambda i, t: (i, t + off, 0))

    def local(width):
        return pl.BlockSpec((1, TILE, width), lambda i, t: (i, t, 0))

    if split_ctx:
        streams = [x, ctx]
        stream_specs = [pl.BlockSpec((1, TILE, d), lambda i, t: (i, jnp.maximum(t - 1, 0), 0)),
                        pl.BlockSpec((1, TILE, d), lambda i, t: (i, 0, 0))]
    else:
        streams, stream_specs = [x], [stream(d)]
    consts_a = [w["vecs"], w["wg"], w["wm"]]
    consts_b = [w["wb"], w["wo"]]
    return pl.pallas_call(
        functools.partial(_merge_kernel, split_ctx=split_ctx),
        grid=(b, nq),
        in_specs=stream_specs
                 + [pl.BlockSpec((1, 1, 3, d), lambda i, t: (layer, jnp.where(t + off == 0, ctx_row, i), 0, 0))]
                 + [_layer_spec(c, layer) for c in consts_a]
                 + [local(bw), local(bw), local(bw), stream(bw), stream(bw)]
                 + [_layer_spec(c, layer) for c in consts_b],
        out_specs=local(d),
        out_shape=jax.ShapeDtypeStruct((b, nq * TILE, d), F32),
        compiler_params=_cparams(("arbitrary", "arbitrary")),
        name="merge",
    )(*streams, mod_all, *consts_a, ya, yb, yc, yf, yr, *consts_b)


def _rope_tables(n, n_ctx):
    def axial(rot_dim):
        n_rows = n // GRID_W
        rows = np.repeat(np.arange(n_rows, dtype=np.float64), GRID_W)
        cols = np.tile(np.arange(GRID_W, dtype=np.float64), n_rows)
        quarter = rot_dim // 4
        freqs = ROPE_THETA ** (-np.arange(quarter, dtype=np.float64) / quarter)
        ang = np.concatenate([rows[:, None] * freqs, cols[:, None] * freqs], axis=-1)
        return np.cos(ang), np.sin(ang)

    def with_ctx(cos, sin):
        return (jnp.asarray(np.concatenate([np.ones((n_ctx, LANES)), cos], axis=0), F32),
                jnp.asarray(np.concatenate([np.zeros((n_ctx, LANES)), sin], axis=0), F32))

    cm, sm = axial(MLA_ROPE)
    one, zero = np.ones((n, MLA_NOPE)), np.zeros((n, MLA_NOPE))
    pad1, pad0 = np.ones((n, LANES - MLA_QK)), np.zeros((n, LANES - MLA_QK))
    cosm, sinm = with_ctx(np.concatenate([one, cm, cm, pad1], axis=-1),
                          np.concatenate([zero, sm, sm, pad0], axis=-1))
    ch, sh = axial(HEAD_DIM)
    cosh, sinh = with_ctx(np.tile(ch, (1, 4)), np.tile(sh, (1, 4)))
    return cosm, sinm, cosh, sinh


def _place(w, mat, gain=None):
    wg = w if gain is None else w * gain[:, None, :]
    return jnp.einsum("lkc,cp->lkp", wg, jnp.asarray(mat, F32), precision=lax.Precision.HIGHEST)


def _copy_matrix(n_heads, in_width, in_start, count, out_width, out_start):
    m = np.zeros((n_heads * in_width, n_heads * out_width), np.float32)
    for h in range(n_heads):
        for j in range(count):
            m[h * in_width + in_start + j, h * out_width + out_start + j] = 1.0
    return m


def _partner_matrix(n_heads, in_width, in_start, half, out_width, out_start):
    m = np.zeros((n_heads * in_width, n_heads * out_width), np.float32)
    for h in range(n_heads):
        for j in range(half):
            x1, x2 = h * in_width + in_start + j, h * in_width + in_start + half + j
            o1, o2 = h * out_width + out_start + j, h * out_width + out_start + half + j
            m[x2, o1] = -1.0
            m[x1, o2] = 1.0
    return m


IN_SIZES = (256, 128, 32, 256, 128, 128, 256, 128, 128, 256, 1024)
PREP_ROWS = 256


def _prep_kernel(wt_ref, vec_ref, pc_ref, pm_ref, ph_ref, wp_o, wg_o, wm_o):
    offs = [0]
    for sz in IN_SIZES:
        offs.append(offs[-1] + sz)

    def rows(a, b):
        return wt_ref[0, a:b, :].T

    def piece(i):
        return rows(offs[i], offs[i + 1])

    def put(start, val):
        wp_o[0, :, start:start + val.shape[1]] = val.astype(BF16)

    def partner(val, gain, mat):
        return _dot((val * gain).astype(BF16), mat)

    put(0, piece(0))
    put(256, piece(1))
    z = rows(offs[2], offs[2] + LANES)
    hi = z.astype(BF16)
    r1 = z - hi.astype(F32)
    mid = r1.astype(BF16)
    lo = (r1 - mid.astype(F32)).astype(BF16)
    pc = pc_ref[...]
    kr_slab = _dot(hi, pc) + _dot(mid, pc) + _dot(lo, pc)
    put(384, kr_slab)
    put(512, partner(kr_slab, _gain(vec_ref, GAIN_MK), pm_ref[...]))
    ph = ph_ref[...]
    col = 640
    for first, q_slot, k_slot in ((3, GAIN_SQ, GAIN_SK), (6, GAIN_AQ, GAIN_AK)):
        q, k, v = piece(first), piece(first + 1), piece(first + 2)
        g_q = _gain(vec_ref, q_slot)
        put(col, q)
        put(col + 256, partner(q, jnp.concatenate([g_q, g_q], axis=1), ph))
        put(col + 512, k)
        put(col + 640, partner(k, _gain(vec_ref, k_slot), ph[:LANES, :LANES]))
        put(col + 768, v)
        col += 896
    put(col, piece(9))
    wg_o[0] = piece(10).astype(BF16)
    for c in range(offs[11], wt_ref.shape[1], 4 * LANES):
        wm_o[0, :, c - offs[11]:c - offs[11] + 4 * LANES] = rows(c, c + 4 * LANES).astype(BF16)


def _prep_in_weights(w_in, vecs):
    depth, d, cols = w_in.shape
    n_merge = cols - sum(IN_SIZES)
    half = HEAD_DIM // 2
    mats = [_copy_matrix(1, LANES, 0, MLA_ROPE, LANES, MLA_NOPE),
            _partner_matrix(1, LANES, MLA_NOPE, MLA_ROPE // 2, LANES, MLA_NOPE),
            _partner_matrix(4, HEAD_DIM, 0, half, HEAD_DIM, 0)]
    mats = [jnp.asarray(m, BF16) for m in mats]
    wp_cols = 2688

    def out_rows(width):
        return pl.BlockSpec((1, PREP_ROWS, width), lambda l, r: (l, r, 0))

    return pl.pallas_call(
        _prep_kernel,
        grid=(depth, d // PREP_ROWS),
        in_specs=[pl.BlockSpec((1, cols, PREP_ROWS), lambda l, r: (l, 0, r)),
                  pl.BlockSpec((1,) + vecs.shape[1:], lambda l, r: (l, 0, 0))]
                 + [pl.BlockSpec(m.shape, lambda l, r: (0, 0)) for m in mats],
        out_specs=[out_rows(wp_cols), out_rows(IN_SIZES[10]), out_rows(n_merge)],
        out_shape=[jax.ShapeDtypeStruct((depth, d, wp_cols), BF16),
                   jax.ShapeDtypeStruct((depth, d, IN_SIZES[10]), BF16),
                   jax.ShapeDtypeStruct((depth, d, n_merge), BF16)],
        compiler_params=_cparams(("arbitrary", "arbitrary")),
        name="prep_weights",
    )(jnp.transpose(w_in, (0, 2, 1)), vecs, *mats)


def _prepare_weights(p):
    depth, d, _ = p["w_in"].shape
    q_scale = HEAD_DIM ** -0.5 * LOG2E
    mq_scale = MLA_QK ** -0.5 * LOG2E
    g_sq = jnp.tile(p["swa_q_norm"] * q_scale, (1, 2))
    g_sk = jnp.tile(p["swa_k_norm"], (1, 2))
    g_aq = jnp.tile(p["axa_q_norm"] * q_scale, (1, 2))
    g_ak = jnp.tile(p["axa_k_norm"], (1, 2))
    g_mq = p["mla_q_norm"] * mq_scale
    g_mk = p["mla_k_norm"]

    w_uq, w_ukv = p["mla_w_uq"], p["mla_w_ukv"]
    kv_width = MLA_NOPE + MLA_V
    wuq = _place(w_uq, _copy_matrix(MLA_HEADS, MLA_QK, 0, MLA_QK, LANES, 0)).astype(BF16)
    wuqr = _place(w_uq, _partner_matrix(MLA_HEADS, MLA_QK, MLA_NOPE, MLA_ROPE // 2, LANES, MLA_NOPE),
                  jnp.tile(g_mq, (1, MLA_HEADS))).astype(BF16)
    wuk = _place(w_ukv, _copy_matrix(MLA_HEADS, kv_width, 0, MLA_NOPE, LANES, 0)).astype(BF16)
    wuv = _place(w_ukv, _copy_matrix(MLA_HEADS, kv_width, MLA_NOPE, MLA_V, MLA_V, 0)).astype(BF16)

    def row(v):
        v = v if v.ndim == 3 else v[:, None, :]
        return jnp.pad(v, ((0, 0), (0, 0), (0, d - v.shape[2])))

    slab = lambda g: jnp.pad(g, ((0, 0), (0, LANES - g.shape[1])))
    gains = jnp.concatenate([p["mla_ckv_norm"], slab(g_mq), slab(g_mk), g_sq, g_sk, g_aq, g_ak], axis=1)
    vec_rows = [row(p["norm_w"]), row(p["mla_cq_norm"]), row(gains), row(p["lru_conv_w"]),
                row(p["lru_conv_b"]), row(p["lru_b_r"]), row(p["lru_b_i"]), row(p["lru_lambda"])]
    n_rows = sum(r.shape[1] for r in vec_rows)
    vecs = jnp.concatenate(vec_rows + [jnp.zeros((depth, VEC_ROWS - n_rows, d), F32)], axis=1)
    wp, wg, wm = _prep_in_weights(p["w_in"], vecs)

    idx = np.arange(4 * HEAD_DIM) // HEAD_DIM
    gmat = jnp.asarray(idx[:, None] == idx[None, :], BF16)

    def block_diag_hi_lo(wb):
        l, two, k, m, _ = wb.shape
        full = jnp.einsum("ldkij,kn->ldkinj", wb, jnp.eye(k, dtype=wb.dtype)).reshape(l, two, k * m, k * m)
        hi = full.astype(BF16)
        return hi, (full - hi.astype(F32)).astype(BF16)

    wr_hi, wr_lo = block_diag_hi_lo(p["lru_w_r"])
    wi_hi, wi_lo = block_diag_hi_lo(p["lru_w_i"])
    return {
        "vecs": vecs, "gmat": gmat,
        "wp": wp, "wuq": wuq, "wuqr": wuqr, "wuk": wuk, "wuv": wuv,
        "wg": wg, "wm": wm,
        "wb": p["w_branch"].astype(BF16), "wo": p["w_out"].astype(BF16),
        "wr_hi": wr_hi, "wr_lo": wr_lo, "wi_hi": wi_hi, "wi_lo": wi_lo,
    }


def kernel(x, c, ctx, c_ctx, w_mod, b_mod, norm_w, w_in, mla_cq_norm, mla_ckv_norm, mla_w_uq, mla_w_ukv, mla_q_norm, mla_k_norm, swa_q_norm, swa_k_norm, swa_sink, axa_q_norm, axa_k_norm, lru_conv_w, lru_conv_b, lru_w_r, lru_b_r, lru_w_i, lru_b_i, lru_lambda, w_branch, w_out):
    p = dict(norm_w=norm_w, w_in=w_in, mla_cq_norm=mla_cq_norm, mla_ckv_norm=mla_ckv_norm,
             mla_w_uq=mla_w_uq, mla_w_ukv=mla_w_ukv, mla_q_norm=mla_q_norm, mla_k_norm=mla_k_norm,
             swa_q_norm=swa_q_norm, swa_k_norm=swa_k_norm,
             axa_q_norm=axa_q_norm, axa_k_norm=axa_k_norm, lru_conv_w=lru_conv_w,
             lru_conv_b=lru_conv_b, lru_w_r=lru_w_r, lru_b_r=lru_b_r, lru_w_i=lru_w_i,
             lru_b_i=lru_b_i, lru_lambda=lru_lambda, w_branch=w_branch, w_out=w_out)
    b, n, d = x.shape
    n_ctx = ctx.shape[1]
    depth = w_mod.shape[0]
    assert n_ctx == TILE and n % TILE == 0 and n % GRID_W == 0 and b + 1 <= SUBLANES

    rows = jnp.concatenate([c, c_ctx[None, :], jnp.zeros((SUBLANES - b - 1, d), F32)], axis=0)
    mod_all = _modulation(rows, w_mod, b_mod).reshape(depth, SUBLANES, 3, d)

    w = _prepare_weights(p)
    tabs = _rope_tables(n, n_ctx)
    xs, cs = x, ctx
    for layer in range(depth):
        upd = layer < depth - 1
        mq, mk, mv, sq, sk, sv, aq, ak, av, zl = _project(xs, cs, mod_all, w, tabs, layer)
        ya = _dense_attention(mq, mk, mv, upd)
        yb = _window_attention(swa_sink, sq, sk, sv, upd, layer)
        yc = _dense_attention(aq, ak, av, upd)
        yf, yr = _rglru(zl, w, layer)
        if cs is not None and not upd:
            xs, cs = jnp.concatenate([cs, xs], axis=1), None
        xs, cs = _merge(xs, cs, mod_all, w, ya, yb, yc, yf, yr, upd, layer), None
    return xs
```

```python
import functools

import numpy as np
import jax
import jax.numpy as jnp
from jax import lax
from jax.experimental import pallas as pl
from jax.experimental.pallas import tpu as pltpu

GRID_W = 64
N_BRANCH = 4
HEAD_DIM = 64
MLA_HEADS = 4
MLA_NOPE = 64
MLA_ROPE = 32
MLA_V = 64
MLA_QK = MLA_NOPE + MLA_ROPE
SWA_Q_HEADS = 4
SWA_KV_HEADS = 2
WINDOW = 128
AXA_Q_HEADS = 4
AXA_KV_HEADS = 2
LRU_BLOCKS = 4
LRU_C = 8.0
CONV_W = 4
CONV_LEFT = 2
ROPE_THETA = 10000.0
RMS_EPS = 1e-6
NEG_INF = -1e30
LOG2E = 1.4426950408889634

LANES = 128
SUBLANES = 8
TILE = 256
SUB_BATCH = 2
KEY_CHUNKS = 4
VMEM_LIMIT = 56 * 1024 * 1024

F32 = jnp.float32
BF16 = jnp.bfloat16

VEC_NW, VEC_GCQ, VEC_GAINS, VEC_CONVW, VEC_CONVB, VEC_BR, VEC_BI, VEC_LAM, VEC_ROWS = 0, 1, 2, 3, 7, 8, 10, 12, 16
GAIN_CKV, GAIN_MQ, GAIN_MK, GAIN_SQ, GAIN_SK, GAIN_AQ, GAIN_AK = range(7)


def _dot(a, b):
    return jnp.dot(a, b, preferred_element_type=F32)


def _dot_nt(a, b):
    return lax.dot_general(a, b, (((1,), (1,)), ((), ())), preferred_element_type=F32)


def _split(a):
    hi = a.astype(BF16)
    lo = (a - hi.astype(F32)).astype(BF16)
    return hi, lo


def _dot3(a, w_hi, w_lo):
    a_hi, a_lo = _split(a)
    return _dot(a_hi, w_hi) + _dot(a_lo, w_hi) + _dot(a_hi, w_lo)


def _sigmoid(v):
    return 1.0 / (1.0 + jnp.exp(-v))


def _cparams(sem):
    return pltpu.CompilerParams(dimension_semantics=sem, vmem_limit_bytes=VMEM_LIMIT)


def _layer_spec(arr, layer):
    nd = arr.ndim - 1
    return pl.BlockSpec((1,) + arr.shape[1:], lambda *_: (layer,) + (0,) * nd)


def _gain(vec_ref, slot):
    return vec_ref[0, VEC_GAINS:VEC_GAINS + 1, slot * LANES:(slot + 1) * LANES]


def _mod_kernel(c_ref, w_ref, b_ref, o_ref):
    v = c_ref[...]
    a = v * _sigmoid(v)
    w_hi, w_lo = _split(w_ref[0])
    o_ref[0] = _dot3(a, w_hi, w_lo) + b_ref[0]


def _modulation(cvec, w_mod, b_mod):
    depth, d, d3 = w_mod.shape
    r = cvec.shape[0]
    bn = 512
    return pl.pallas_call(
        _mod_kernel,
        grid=(depth, d3 // bn),
        in_specs=[pl.BlockSpec((r, d), lambda l, j: (0, 0)),
                  pl.BlockSpec((1, d, bn), lambda l, j: (l, 0, j)),
                  pl.BlockSpec((1, 1, bn), lambda l, j: (l, 0, j))],
        out_specs=pl.BlockSpec((1, r, bn), lambda l, j: (l, 0, j)),
        out_shape=jax.ShapeDtypeStruct((depth, r, d3), F32),
        compiler_params=_cparams(("arbitrary", "arbitrary")),
        name="modulation",
    )(cvec, w_mod, b_mod.reshape(depth, 1, d3))


def _modulated_norm(x, mod, vec_ref):
    ms = jnp.mean(x * x, axis=-1, keepdims=True)
    y = x * lax.rsqrt(ms + RMS_EPS) * vec_ref[0, VEC_NW:VEC_NW + 1, :]
    return y * (1.0 + mod[1:2, :]) + mod[0:1, :]


def _stream_tile(x_ref, ctx_ref, is_ctx, j=0):
    if ctx_ref is None:
        return x_ref[j]
    return jnp.where(is_ctx, ctx_ref[j], x_ref[j])


def _lane_rinv(v, n):
    return lax.rsqrt(jnp.sum(v * v, axis=-1, keepdims=True) * (1.0 / n) + RMS_EPS)


def _head_rinv(v, gmat):
    hi, lo = _split(v * v)
    ss = _dot(hi, gmat) + _dot(lo, gmat)
    return lax.rsqrt(ss * (1.0 / HEAD_DIM) + RMS_EPS)


def _proj_kernel(*refs, split_ctx):
    x_ref, ctx_ref = (refs[0], refs[1]) if split_ctx else (refs[0], None)
    (mod_ref, vec_ref, wp_ref, wuq_ref, wuqr_ref, wuk_ref, wuv_ref, gmat_ref,
     cosm_ref, sinm_ref, cosh_ref, sinh_ref,
     mq_o, mk_o, mv_o, sq_o, sk_o, sv_o, aq_o, ak_o, av_o, lru_o) = refs[2 if split_ctx else 1:]
    n_sub = x_ref.shape[0]
    is_ctx = pl.program_id(0) == 0
    def project_in(j):
        x = _stream_tile(x_ref, ctx_ref, is_ctx, j)
        mod = mod_ref[0, pl.ds(jnp.where(is_ctx, 0, j), 1)][0]
        hb = _modulated_norm(x, mod, vec_ref).astype(BF16)
        return _dot(hb, wp_ref[0])

    t = x_ref.shape[1]
    lo = lax.broadcasted_iota(jnp.int32, (t, LANES), 1) < HEAD_DIM
    cosm, sinm = cosm_ref[...], sinm_ref[...]
    cosh, sinh = cosh_ref[...], sinh_ref[...]
    gmat = gmat_ref[...]

    def pair_store(j, out, slab):
        swapped = pltpu.roll(slab, HEAD_DIM, 1)
        out[j, 0] = jnp.where(lo, slab, 0.0).astype(BF16)
        out[j, 1] = jnp.where(lo, 0.0, swapped).astype(BF16)
        out[j, 2] = jnp.where(lo, swapped, 0.0).astype(BF16)
        out[j, 3] = jnp.where(lo, 0.0, slab).astype(BF16)

    def heads(j, z):
        g_cq = vec_ref[0, VEC_GCQ:VEC_GCQ + 1, 0:256]
        cq = (z[:, 0:256] * _lane_rinv(z[:, 0:256], 256.0) * g_cq).astype(BF16)
        q = _dot(cq, wuq_ref[0])
        qr = _dot(cq, wuqr_ref[0])
        ckv = (z[:, 256:384] * _lane_rinv(z[:, 256:384], 128.0) * _gain(vec_ref, GAIN_CKV)).astype(BF16)
        kn = _dot(ckv, wuk_ref[0])
        vv = _dot(ckv, wuv_ref[0])
        kr = z[:, 384:512]
        krr = z[:, 512:640]
        gc_q = _gain(vec_ref, GAIN_MQ) * cosm
        gc_k = _gain(vec_ref, GAIN_MK) * cosm
        for hh in range(MLA_HEADS):
            sl = slice(hh * LANES, (hh + 1) * LANES)
            r = _lane_rinv(q[:, sl], float(MLA_QK))
            mq_o[j, hh] = ((q[:, sl] * r) * gc_q + (qr[:, sl] * r) * sinm).astype(BF16)
            ks = kn[:, sl] + kr
            r = _lane_rinv(ks, float(MLA_QK))
            mk_o[j, hh] = ((ks * r) * gc_k + (krr * r) * sinm).astype(BF16)
        mv_o[j] = vv.T.astype(BF16)

        def gqa(c0, g_q, g_k, q_o, k_o, v_o, v_transposed):
            zq, zqr = z[:, c0:c0 + 256], z[:, c0 + 256:c0 + 512]
            r = _head_rinv(zq, gmat)
            gc = _gain(vec_ref, g_q) * cosh
            for i in range(2):
                sl = slice(i * LANES, (i + 1) * LANES)
                q_o[j, i] = ((zq[:, sl] * r[:, sl]) * gc + (zqr[:, sl] * r[:, sl]) * sinh).astype(BF16)
            zk, zkr = z[:, c0 + 512:c0 + 640], z[:, c0 + 640:c0 + 768]
            r = _head_rinv(zk, gmat[:LANES, :LANES])
            ks = (zk * r) * (_gain(vec_ref, g_k) * cosh) + (zkr * r) * sinh
            pair_store(j, k_o, ks)
            zv = z[:, c0 + 768:c0 + 896]
            if v_transposed:
                v_o[j] = zv.T.astype(BF16)
            else:
                pair_store(j, v_o, zv)

        gqa(640, GAIN_SQ, GAIN_SK, sq_o, sk_o, sv_o, False)
        gqa(1536, GAIN_AQ, GAIN_AK, aq_o, ak_o, av_o, True)
        lru_o[j] = z[:, 2432:2688]

    for j in range(n_sub):
        heads(j, project_in(j))


def _project(x, ctx, mod_all, w, tabs, layer):
    split_ctx = ctx is not None
    b, d = x.shape[0], x.shape[2]
    s = x.shape[1] + (ctx.shape[1] if split_ctx else 0)
    nt = s // TILE
    ctx_row = b

    assert b % SUB_BATCH == 0 and ctx_row % SUB_BATCH == 0
    nbb = b // SUB_BATCH

    def tile_spec(width):
        return pl.BlockSpec((SUB_BATCH, TILE, width), lambda t, i: (i, t, 0))

    def head_spec(nh):
        return pl.BlockSpec((SUB_BATCH, nh, TILE, LANES), lambda t, i: (i, 0, t, 0))

    def head_spec_t(nh):
        return pl.BlockSpec((SUB_BATCH, nh * HEAD_DIM, TILE), lambda t, i: (i, 0, t))

    if split_ctx:
        streams = [x, ctx]
        stream_specs = [pl.BlockSpec((SUB_BATCH, TILE, d), lambda t, i: (i, jnp.maximum(t - 1, 0), 0)),
                        pl.BlockSpec((SUB_BATCH, TILE, d), lambda t, i: (jnp.where(t == 0, i, nbb - 1), 0, 0))]
    else:
        streams, stream_specs = [x], [tile_spec(d)]
    tab_spec = pl.BlockSpec((TILE, LANES), lambda t, i: (t, 0))
    consts = [w["vecs"], w["wp"], w["wuq"], w["wuqr"], w["wuk"], w["wuv"]]
    in_specs = (stream_specs
                + [pl.BlockSpec((1, SUB_BATCH, 3, d),
                                lambda t, i: (layer, jnp.where(t == 0, ctx_row // SUB_BATCH, i), 0, 0))]
                + [_layer_spec(c, layer) for c in consts]
                + [pl.BlockSpec(w["gmat"].shape, lambda t, i: (0, 0))] + [tab_spec] * 4)
    n_slabs = [MLA_HEADS, MLA_HEADS, MLA_HEADS, 2, 4, 4, 2, 4, AXA_KV_HEADS]
    transposed = [False, False, True, False, False, False, False, False, True]
    out_specs = ([head_spec_t(nh) if tr else head_spec(nh) for nh, tr in zip(n_slabs, transposed)]
                 + [tile_spec(4 * HEAD_DIM)])
    out_shape = ([jax.ShapeDtypeStruct((b, nh * HEAD_DIM, s) if tr else (b, nh, s, LANES), BF16)
                  for nh, tr in zip(n_slabs, transposed)]
                 + [jax.ShapeDtypeStruct((b, s, 4 * HEAD_DIM), F32)])
    return pl.pallas_call(
        functools.partial(_proj_kernel, split_ctx=split_ctx),
        grid=(nt, nbb),
        in_specs=in_specs,
        out_specs=out_specs,
        out_shape=out_shape,
        compiler_params=_cparams(("arbitrary", "arbitrary")),
        name="project",
    )(*streams, mod_all, *consts, w["gmat"], *tabs)


def _dense_attn_kernel(*refs, q_share, ctx_tile_first):
    q_refs, (k_ref, v_ref, o_ref) = refs[:-3], refs[-3:]
    s_len = k_ref.shape[2]
    n_heads = k_ref.shape[1]
    v_share = n_heads * HEAD_DIM // v_ref.shape[1]

    def attend(n_keys):
        n_groups = n_keys // LANES
        n_chunks = min(KEY_CHUNKS, n_groups)
        edges = [(c * n_groups // n_chunks) * LANES for c in range(n_chunks + 1)]
        chunks = [slice(a, b) for a, b in zip(edges[:-1], edges[1:])]

        def scores(unit):
            q_ref, hh = unit
            q = q_ref[0, hh // q_share]
            return [_dot_nt(k_ref[0, hh, sl, :], q) for sl in chunks]

        def head_out(unit, st):
            hh = unit[1]
            m = functools.reduce(jnp.maximum, [jnp.max(sc, axis=0, keepdims=True) for sc in st])
            pt = [jnp.exp2(sc - m) for sc in st]
            l = sum(jnp.sum(pc, axis=0, keepdims=True) for pc in pt)
            band = slice(hh // v_share * HEAD_DIM, (hh // v_share + 1) * HEAD_DIM)
            ot = sum(_dot(v_ref[0, band, sl], pc.astype(BF16)) for sl, pc in zip(chunks, pt))
            return ot / l

        units = [(q_ref, hh) for q_ref in q_refs for hh in range(n_heads)]
        st_next = scores(units[0])
        outs = []
        for u, unit in enumerate(units):
            st = st_next
            if u + 1 < len(units):
                st_next = scores(units[u + 1])
            outs.append(head_out(unit, st))
        for j in range(len(q_refs)):
            o = outs[j * n_heads:(j + 1) * n_heads]
            o_ref[0, j * TILE:(j + 1) * TILE, :] = jnp.concatenate(o, axis=0).T

    if ctx_tile_first:
        pl.when(pl.program_id(1) == 0)(lambda: attend(TILE))
        pl.when(pl.program_id(1) > 0)(lambda: attend(s_len))
    else:
        attend(s_len)


def _dense_attention(q, k, v, with_ctx_queries):
    b, nq_slabs, s, _ = q.shape
    nh = k.shape[1]
    nt = s // TILE
    off = 0 if with_ctx_queries else 1
    tps = 1 if with_ctx_queries or (nt - off) % 2 else 2
    steps = (nt - off) // tps
    kern = functools.partial(_dense_attn_kernel, q_share=nh // nq_slabs,
                             ctx_tile_first=with_ctx_queries)

    def q_spec(j):
        return pl.BlockSpec((1, nq_slabs, TILE, LANES), lambda i, t: (i, 0, tps * t + j + off, 0))

    return pl.pallas_call(
        kern,
        grid=(b, steps),
        in_specs=[q_spec(j) for j in range(tps)]
                 + [pl.BlockSpec((1, nh, s, LANES), lambda i, t: (i, 0, 0, 0)),
                    pl.BlockSpec((1,) + v.shape[1:], lambda i, t: (i, 0, 0))],
        out_specs=pl.BlockSpec((1, tps * TILE, 2 * LANES), lambda i, t: (i, t, 0)),
        out_shape=jax.ShapeDtypeStruct((b, steps * tps * TILE, 2 * LANES), F32),
        compiler_params=_cparams(("arbitrary", "arbitrary")),
        name="dense_attention",
    )(*([q] * tps), k, v)


def _window_attn_kernel(sink_ref, q_ref, k_ref, v_ref, o_ref, *, tile_off, layer):
    s_len = k_ref.shape[2]
    n_tiles = s_len // TILE
    ti = pl.program_id(1) + tile_off
    is_lat = ti > 0
    t0 = ti * TILE
    half = TILE // 2
    left0 = pl.multiple_of(jnp.maximum(t0 - half, 0), half)
    cen0 = pl.multiple_of(t0, TILE)
    right0 = pl.multiple_of(jnp.minimum(t0 + TILE, s_len - half), half)
    r_c = lax.broadcasted_iota(jnp.int32, (TILE, TILE), 0)
    c_c = lax.broadcasted_iota(jnp.int32, (TILE, TILE), 1)
    r_s = lax.broadcasted_iota(jnp.int32, (TILE, half), 0)
    c_s = lax.broadcasted_iota(jnp.int32, (TILE, half), 1)
    ok_cen = (jnp.abs(r_c - c_c) <= WINDOW) & is_lat
    ok_left = ((r_s - c_s + half) <= WINDOW) & (ti > 1)
    ok_right = ((c_s - r_s + TILE) <= WINDOW) & is_lat & (ti < n_tiles - 1)
    def lane_groups(a):
        return [a[:, g * LANES:(g + 1) * LANES] for g in range(a.shape[1] // LANES)]

    def scores(hh):
        q = q_ref[0, hh // 2]
        k = lambda start, size: k_ref[0, hh, pl.ds(start, size), :]
        return (_dot_nt(q, k(0, TILE)),
                jnp.where(ok_left, _dot_nt(q, k(left0, half)), NEG_INF),
                jnp.where(ok_cen, _dot_nt(q, k(cen0, TILE)), NEG_INF),
                jnp.where(ok_right, _dot_nt(q, k(right0, half)), NEG_INF))

    def head_out(hh, sc):
        sink = sink_ref[layer, hh] * LOG2E
        m = functools.reduce(jnp.maximum, [g for piece in sc for g in lane_groups(piece)])
        m = jnp.maximum(jnp.max(m, -1, keepdims=True), sink)
        ps = [jnp.exp2(piece - m) for piece in sc]
        den = sum(g for piece in ps for g in lane_groups(piece))
        den = jnp.sum(den, -1, keepdims=True) + jnp.exp2(sink - m)
        v = lambda start, size: v_ref[0, hh, pl.ds(start, size), :]
        vs = (v(0, TILE), v(left0, half), v(cen0, TILE), v(right0, half))
        num = sum(_dot(p.astype(BF16), vv) for p, vv in zip(ps, vs))
        return num / den

    n_heads = k_ref.shape[1]
    sc_next = scores(0)
    outs = []
    for hh in range(n_heads):
        sc = sc_next
        if hh + 1 < n_heads:
            sc_next = scores(hh + 1)
        outs.append(head_out(hh, sc))
    o_ref[0] = jnp.concatenate([outs[0] + outs[1], outs[2] + outs[3]], axis=-1)


def _window_attention(sink, q, k, v, with_ctx_queries, layer):
    b, nq_slabs, s, _ = q.shape
    nh = k.shape[1]
    nt = s // TILE
    off = 0 if with_ctx_queries else 1
    nq = nt - off
    kern = functools.partial(_window_attn_kernel, tile_off=off, layer=layer)
    return pl.pallas_call(
        kern,
        grid=(b, nq),
        in_specs=[pl.BlockSpec(memory_space=pltpu.SMEM),
                  pl.BlockSpec((1, nq_slabs, TILE, LANES), lambda i, t: (i, 0, t + off, 0)),
                  pl.BlockSpec((1, nh, s, LANES), lambda i, t: (i, 0, 0, 0)),
                  pl.BlockSpec((1, nh, s, LANES), lambda i, t: (i, 0, 0, 0))],
        out_specs=pl.BlockSpec((1, TILE, 2 * LANES), lambda i, t: (i, t, 0)),
        out_shape=jax.ShapeDtypeStruct((b, nq * TILE, 2 * LANES), F32),
        compiler_params=_cparams(("arbitrary", "arbitrary")),
        name="window_attention",
    )(sink, q, k, v)


def _lru_conv(prev_ref, cur_ref, next_ref, prev_ok, next_ok, vec_ref):
    t = cur_ref.shape[1]
    prev = jnp.where(prev_ok, prev_ref[0], 0.0)
    nxt = jnp.where(next_ok, next_ref[0], 0.0)
    ext = jnp.concatenate([prev, cur_ref[0], nxt], axis=0)
    n = t + 2 * SUBLANES
    c = cur_ref.shape[2]

    def tap(j):
        return vec_ref[0, VEC_CONVW + j:VEC_CONVW + j + 1, 0:c]

    u = vec_ref[0, VEC_CONVB:VEC_CONVB + 1, 0:c] + cur_ref[0] * tap(CONV_LEFT)
    for j in range(CONV_W):
        off = j - CONV_LEFT
        if off == 0:
            continue
        sh = pltpu.roll(ext, (-off) % n, 0)[SUBLANES:SUBLANES + t]
        u = u + sh * tap(j)
    return u


def _lru_scan_tile(u, d, wr_hi, wr_lo, wi_hi, wi_lo, vec_ref, h0, reverse):
    t, c = u.shape

    def vec(row):
        return vec_ref[0, row + d:row + d + 1, 0:c]

    r = _sigmoid(_dot3(u, wr_hi[0, d], wr_lo[0, d]) + vec(VEC_BR))
    i = _sigmoid(_dot3(u, wi_hi[0, d], wi_lo[0, d]) + vec(VEC_BI))
    lam = vec(VEC_LAM)
    softplus_neg = jnp.maximum(-lam, 0.0) + jnp.log1p(jnp.exp(-jnp.abs(lam)))
    log_a = -LRU_C * r * softplus_neg
    a = jnp.exp(log_a)
    bx = jnp.sqrt(1.0 - a * a) * (i * u)
    row = lax.broadcasted_iota(jnp.int32, u.shape, 0)
    step = 1
    while step < t:
        if reverse:
            ok = row < t - step
            a_sh = jnp.where(ok, pltpu.roll(a, t - step, 0), 1.0)
            b_sh = jnp.where(ok, pltpu.roll(bx, t - step, 0), 0.0)
        else:
            ok = row >= step
            a_sh = jnp.where(ok, pltpu.roll(a, step, 0), 1.0)
            b_sh = jnp.where(ok, pltpu.roll(bx, step, 0), 0.0)
        bx = a * b_sh + bx
        a = a * a_sh
        step *= 2
    h = a * h0 + bx
    h_end = h[0:1, :] if reverse else h[t - 1:t, :]
    return h, h_end


def _lru_kernel(fp_ref, fc_ref, fn_ref, rp_ref, rc_ref, rn_ref, vec_ref,
                wr_hi, wr_lo, wi_hi, wi_lo, yf_ref, yb_ref, hf_s, hb_s):
    j = pl.program_id(1)
    nt = pl.num_programs(1)

    @pl.when(j == 0)
    def _():
        hf_s[...] = jnp.zeros_like(hf_s)
        hb_s[...] = jnp.zeros_like(hb_s)

    def bounds(ti):
        return ti >= 2, (ti >= 1) & (ti < nt - 1)

    tf = j
    tr = jnp.where(j == 0, 0, nt - j)
    p_ok, n_ok = bounds(tf)
    uf = _lru_conv(fp_ref, fc_ref, fn_ref, p_ok, n_ok, vec_ref)
    h, h_end = _lru_scan_tile(uf, 0, wr_hi, wr_lo, wi_hi, wi_lo, vec_ref, hf_s[...], False)
    yf_ref[0] = h
    hf_s[...] = h_end
    p_ok, n_ok = bounds(tr)
    ur = _lru_conv(rp_ref, rc_ref, rn_ref, p_ok, n_ok, vec_ref)
    h, h_end = _lru_scan_tile(ur, 1, wr_hi, wr_lo, wi_hi, wi_lo, vec_ref, hb_s[...], True)
    yb_ref[0] = h
    hb_s[...] = h_end


def _rglru(z, w, layer):
    b, s, c = z.shape
    nt = s // TILE
    r8 = TILE // SUBLANES
    n8 = s // SUBLANES

    def rev(j):
        return jnp.where(j == 0, 0, nt - j)

    def cur(f):
        return pl.BlockSpec((1, TILE, c), lambda i, j: (i, f(j), 0))

    def prev(f):
        return pl.BlockSpec((1, SUBLANES, c), lambda i, j: (i, jnp.maximum(f(j) * r8 - 1, 0), 0))

    def nxt(f):
        return pl.BlockSpec((1, SUBLANES, c), lambda i, j: (i, jnp.minimum((f(j) + 1) * r8, n8 - 1), 0))

    ident = lambda j: j
    consts = [w["vecs"], w["wr_hi"], w["wr_lo"], w["wi_hi"], w["wi_lo"]]
    return pl.pallas_call(
        _lru_kernel,
        grid=(b, nt),
        in_specs=[prev(ident), cur(ident), nxt(ident), prev(rev), cur(rev), nxt(rev)]
                 + [_layer_spec(x, layer) for x in consts],
        out_specs=[cur(ident), cur(rev)],
        out_shape=[jax.ShapeDtypeStruct((b, s, c), F32)] * 2,
        scratch_shapes=[pltpu.VMEM((1, c), F32), pltpu.VMEM((1, c), F32)],
        compiler_params=_cparams(("arbitrary", "arbitrary")),
        name="rglru",
    )(z, z, z, z, z, z, *consts)


def _merge_kernel(*refs, split_ctx, ctx_tile_first):
    x_ref, ctx_ref = (refs[0], refs[1]) if split_ctx else (refs[0], None)
    (mod_ref, vec_ref, wg_ref, wm_ref, ya_ref, yb_ref, yc_ref, yf_ref, yr_ref,
     wb_ref, wo_ref, o_ref) = refs[2 if split_ctx else 1:]
    is_ctx = (pl.program_id(1) == 0) if ctx_tile_first else False
    d = x_ref.shape[2]
    bw = ya_ref.shape[2]
    for j in range(x_ref.shape[0]):
        x = _stream_tile(x_ref, ctx_ref, is_ctx, j)
        mod = mod_ref[0, pl.ds(jnp.where(is_ctx, 0, j), 1)][0]
        hb = _modulated_norm(x, mod, vec_ref).astype(BF16)
        zg = _dot(hb, wg_ref[0])
        branches = (ya_ref[j], yb_ref[j], yc_ref[j], yf_ref[j] + yr_ref[j])
        mix = jnp.zeros_like(x)
        for k in range(N_BRANCH):
            g = zg[:, k * bw:(k + 1) * bw]
            y = branches[k] * (g * _sigmoid(g))
            proj = _dot(y.astype(BF16), wb_ref[0, k])
            zm = _dot(hb, wm_ref[0, :, k * d:(k + 1) * d])
            mix = mix + _sigmoid(zm) * proj
        o_ref[j] = x + mod[2:3, :] * _dot(mix.astype(BF16), wo_ref[0])


def _merge(x, ctx, mod_all, w, ya, yb, yc, yf, yr, with_ctx, layer):
    split_ctx = ctx is not None
    assert with_ctx or not split_ctx
    b, d = x.shape[0], x.shape[2]
    s = x.shape[1] + (ctx.shape[1] if split_ctx else 0)
    nt = s // TILE
    off = 0 if with_ctx else 1
    nq = nt - off
    ctx_row = b
    bw = ya.shape[2]
    assert b % SUB_BATCH == 0 and ctx_row % SUB_BATCH == 0
    nbb = b // SUB_BATCH

    def stream(width):
        return pl.BlockSpec((SUB_BATCH, TILE, width), lambda i, t: (i, t + off, 0))

    def local(width):
        return pl.BlockSpec((SUB_BATCH, TILE, width), lambda i, t: (i, t, 0))

    if split_ctx:
        streams = [x, ctx]
        stream_specs = [pl.BlockSpec((SUB_BATCH, TILE, d), lambda i, t: (i, jnp.maximum(t - 1, 0), 0)),
                        pl.BlockSpec((SUB_BATCH, TILE, d), lambda i, t: (i, 0, 0))]
    else:
        streams, stream_specs = [x], [stream(d)]
    consts_a = [w["vecs"], w["wg"], w["wm"]]
    consts_b = [w["wb"], w["wo"]]
    return pl.pallas_call(
        functools.partial(_merge_kernel, split_ctx=split_ctx, ctx_tile_first=with_ctx),
        grid=(nbb, nq),
        in_specs=stream_specs
                 + [pl.BlockSpec((1, SUB_BATCH, 3, d),
                                 lambda i, t: (layer, jnp.where(t + off == 0, ctx_row // SUB_BATCH, i), 0, 0))]
                 + [_layer_spec(c, layer) for c in consts_a]
                 + [local(bw), local(bw), local(bw), stream(bw), stream(bw)]
                 + [_layer_spec(c, layer) for c in consts_b],
        out_specs=local(d),
        out_shape=jax.ShapeDtypeStruct((b, nq * TILE, d), F32),
        compiler_params=_cparams(("arbitrary", "arbitrary")),
        name="merge",
    )(*streams, mod_all, *consts_a, ya, yb, yc, yf, yr, *consts_b)


def _rope_tables(n, n_ctx):
    def axial(rot_dim):
        n_rows = n // GRID_W
        rows = np.repeat(np.arange(n_rows, dtype=np.float64), GRID_W)
        cols = np.tile(np.arange(GRID_W, dtype=np.float64), n_rows)
        quarter = rot_dim // 4
        freqs = ROPE_THETA ** (-np.arange(quarter, dtype=np.float64) / quarter)
        ang = np.concatenate([rows[:, None] * freqs, cols[:, None] * freqs], axis=-1)
        return np.cos(ang), np.sin(ang)

    def with_ctx(cos, sin):
        return (jnp.asarray(np.concatenate([np.ones((n_ctx, LANES)), cos], axis=0), F32),
                jnp.asarray(np.concatenate([np.zeros((n_ctx, LANES)), sin], axis=0), F32))

    cm, sm = axial(MLA_ROPE)
    one, zero = np.ones((n, MLA_NOPE)), np.zeros((n, MLA_NOPE))
    pad1, pad0 = np.ones((n, LANES - MLA_QK)), np.zeros((n, LANES - MLA_QK))
    cosm, sinm = with_ctx(np.concatenate([one, cm, cm, pad1], axis=-1),
                          np.concatenate([zero, sm, sm, pad0], axis=-1))
    ch, sh = axial(HEAD_DIM)
    cosh, sinh = with_ctx(np.tile(ch, (1, 4)), np.tile(sh, (1, 4)))
    return cosm, sinm, cosh, sinh


def _place(w, mat, gain=None):
    wg = w if gain is None else w * gain[:, None, :]
    return jnp.einsum("lkc,cp->lkp", wg, jnp.asarray(mat, F32), precision=lax.Precision.HIGHEST)


def _copy_matrix(n_heads, in_width, in_start, count, out_width, out_start):
    m = np.zeros((n_heads * in_width, n_heads * out_width), np.float32)
    for h in range(n_heads):
        for j in range(count):
            m[h * in_width + in_start + j, h * out_width + out_start + j] = 1.0
    return m


def _partner_matrix(n_heads, in_width, in_start, half, out_width, out_start):
    m = np.zeros((n_heads * in_width, n_heads * out_width), np.float32)
    for h in range(n_heads):
        for j in range(half):
            x1, x2 = h * in_width + in_start + j, h * in_width + in_start + half + j
            o1, o2 = h * out_width + out_start + j, h * out_width + out_start + half + j
            m[x2, o1] = -1.0
            m[x1, o2] = 1.0
    return m


IN_SIZES = (256, 128, 32, 256, 128, 128, 256, 128, 128, 256, 1024)
PREP_ROWS = 256


def _prep_kernel(wt_ref, vec_ref, pc_ref, pm_ref, ph_ref, wp_o, wg_o, wm_o):
    offs = [0]
    for sz in IN_SIZES:
        offs.append(offs[-1] + sz)

    def rows(a, b):
        return wt_ref[0, a:b, :].T

    def piece(i):
        return rows(offs[i], offs[i + 1])

    def put(start, val):
        wp_o[0, :, start:start + val.shape[1]] = val.astype(BF16)

    def partner(val, gain, mat):
        return _dot((val * gain).astype(BF16), mat)

    put(0, piece(0))
    put(256, piece(1))
    z = rows(offs[2], offs[2] + LANES)
    hi = z.astype(BF16)
    r1 = z - hi.astype(F32)
    mid = r1.astype(BF16)
    lo = (r1 - mid.astype(F32)).astype(BF16)
    pc = pc_ref[...]
    kr_slab = _dot(hi, pc) + _dot(mid, pc) + _dot(lo, pc)
    put(384, kr_slab)
    put(512, partner(kr_slab, _gain(vec_ref, GAIN_MK), pm_ref[...]))
    ph = ph_ref[...]
    col = 640
    for first, q_slot, k_slot in ((3, GAIN_SQ, GAIN_SK), (6, GAIN_AQ, GAIN_AK)):
        q, k, v = piece(first), piece(first + 1), piece(first + 2)
        g_q = _gain(vec_ref, q_slot)
        put(col, q)
        put(col + 256, partner(q, jnp.concatenate([g_q, g_q], axis=1), ph))
        put(col + 512, k)
        put(col + 640, partner(k, _gain(vec_ref, k_slot), ph[:LANES, :LANES]))
        put(col + 768, v)
        col += 896
    put(col, piece(9))
    wg_o[0] = piece(10).astype(BF16)
    for c in range(offs[11], wt_ref.shape[1], 4 * LANES):
        wm_o[0, :, c - offs[11]:c - offs[11] + 4 * LANES] = rows(c, c + 4 * LANES).astype(BF16)


def _prep_in_weights(w_in, vecs):
    depth, d, cols = w_in.shape
    n_merge = cols - sum(IN_SIZES)
    half = HEAD_DIM // 2
    mats = [_copy_matrix(1, LANES, 0, MLA_ROPE, LANES, MLA_NOPE),
            _partner_matrix(1, LANES, MLA_NOPE, MLA_ROPE // 2, LANES, MLA_NOPE),
            _partner_matrix(4, HEAD_DIM, 0, half, HEAD_DIM, 0)]
    mats = [jnp.asarray(m, BF16) for m in mats]
    wp_cols = 2688

    def out_rows(width):
        return pl.BlockSpec((1, PREP_ROWS, width), lambda l, r: (l, r, 0))

    return pl.pallas_call(
        _prep_kernel,
        grid=(depth, d // PREP_ROWS),
        in_specs=[pl.BlockSpec((1, cols, PREP_ROWS), lambda l, r: (l, 0, r)),
                  pl.BlockSpec((1,) + vecs.shape[1:], lambda l, r: (l, 0, 0))]
                 + [pl.BlockSpec(m.shape, lambda l, r: (0, 0)) for m in mats],
        out_specs=[out_rows(wp_cols), out_rows(IN_SIZES[10]), out_rows(n_merge)],
        out_shape=[jax.ShapeDtypeStruct((depth, d, wp_cols), BF16),
                   jax.ShapeDtypeStruct((depth, d, IN_SIZES[10]), BF16),
                   jax.ShapeDtypeStruct((depth, d, n_merge), BF16)],
        compiler_params=_cparams(("arbitrary", "arbitrary")),
        name="prep_weights",
    )(jnp.transpose(w_in, (0, 2, 1)), vecs, *mats)


def _prepare_weights(p):
    depth, d, _ = p["w_in"].shape
    q_scale = HEAD_DIM ** -0.5 * LOG2E
    mq_scale = MLA_QK ** -0.5 * LOG2E
    g_sq = jnp.tile(p["swa_q_norm"] * q_scale, (1, 2))
    g_sk = jnp.tile(p["swa_k_norm"], (1, 2))
    g_aq = jnp.tile(p["axa_q_norm"] * q_scale, (1, 2))
    g_ak = jnp.tile(p["axa_k_norm"], (1, 2))
    g_mq = p["mla_q_norm"] * mq_scale
    g_mk = p["mla_k_norm"]

    w_uq, w_ukv = p["mla_w_uq"], p["mla_w_ukv"]
    kv_width = MLA_NOPE + MLA_V
    wuq = _place(w_uq, _copy_matrix(MLA_HEADS, MLA_QK, 0, MLA_QK, LANES, 0)).astype(BF16)
    wuqr = _place(w_uq, _partner_matrix(MLA_HEADS, MLA_QK, MLA_NOPE, MLA_ROPE // 2, LANES, MLA_NOPE),
                  jnp.tile(g_mq, (1, MLA_HEADS))).astype(BF16)
    wuk = _place(w_ukv, _copy_matrix(MLA_HEADS, kv_width, 0, MLA_NOPE, LANES, 0)).astype(BF16)
    wuv = _place(w_ukv, _copy_matrix(MLA_HEADS, kv_width, MLA_NOPE, MLA_V, MLA_V, 0)).astype(BF16)

    def row(v):
        v = v if v.ndim == 3 else v[:, None, :]
        return jnp.pad(v, ((0, 0), (0, 0), (0, d - v.shape[2])))

    slab = lambda g: jnp.pad(g, ((0, 0), (0, LANES - g.shape[1])))
    gains = jnp.concatenate([p["mla_ckv_norm"], slab(g_mq), slab(g_mk), g_sq, g_sk, g_aq, g_ak], axis=1)
    vec_rows = [row(p["norm_w"]), row(p["mla_cq_norm"]), row(gains), row(p["lru_conv_w"]),
                row(p["lru_conv_b"]), row(p["lru_b_r"]), row(p["lru_b_i"]), row(p["lru_lambda"])]
    n_rows = sum(r.shape[1] for r in vec_rows)
    vecs = jnp.concatenate(vec_rows + [jnp.zeros((depth, VEC_ROWS - n_rows, d), F32)], axis=1)
    wp, wg, wm = _prep_in_weights(p["w_in"], vecs)

    idx = np.arange(4 * HEAD_DIM) // HEAD_DIM
    gmat = jnp.asarray(idx[:, None] == idx[None, :], BF16)

    def block_diag_hi_lo(wb):
        l, two, k, m, _ = wb.shape
        full = jnp.einsum("ldkij,kn->ldkinj", wb, jnp.eye(k, dtype=wb.dtype)).reshape(l, two, k * m, k * m)
        hi = full.astype(BF16)
        return hi, (full - hi.astype(F32)).astype(BF16)

    wr_hi, wr_lo = block_diag_hi_lo(p["lru_w_r"])
    wi_hi, wi_lo = block_diag_hi_lo(p["lru_w_i"])
    return {
        "vecs": vecs, "gmat": gmat,
        "wp": wp, "wuq": wuq, "wuqr": wuqr, "wuk": wuk, "wuv": wuv,
        "wg": wg, "wm": wm,
        "wb": p["w_branch"].astype(BF16), "wo": p["w_out"].astype(BF16),
        "wr_hi": wr_hi, "wr_lo": wr_lo, "wi_hi": wi_hi, "wi_lo": wi_lo,
    }


def kernel(x, c, ctx, c_ctx, w_mod, b_mod, norm_w, w_in, mla_cq_norm, mla_ckv_norm, mla_w_uq, mla_w_ukv, mla_q_norm, mla_k_norm, swa_q_norm, swa_k_norm, swa_sink, axa_q_norm, axa_k_norm, lru_conv_w, lru_conv_b, lru_w_r, lru_b_r, lru_w_i, lru_b_i, lru_lambda, w_branch, w_out):
    p = dict(norm_w=norm_w, w_in=w_in, mla_cq_norm=mla_cq_norm, mla_ckv_norm=mla_ckv_norm,
             mla_w_uq=mla_w_uq, mla_w_ukv=mla_w_ukv, mla_q_norm=mla_q_norm, mla_k_norm=mla_k_norm,
             swa_q_norm=swa_q_norm, swa_k_norm=swa_k_norm,
             axa_q_norm=axa_q_norm, axa_k_norm=axa_k_norm, lru_conv_w=lru_conv_w,
             lru_conv_b=lru_conv_b, lru_w_r=lru_w_r, lru_b_r=lru_b_r, lru_w_i=lru_w_i,
             lru_b_i=lru_b_i, lru_lambda=lru_lambda, w_branch=w_branch, w_out=w_out)
    b, n, d = x.shape
    n_ctx = ctx.shape[1]
    depth = w_mod.shape[0]
    assert n_ctx == TILE and n % TILE == 0 and n % GRID_W == 0 and b + 1 <= SUBLANES

    rows = jnp.concatenate([c, c_ctx[None, :], jnp.zeros((SUBLANES - b - 1, d), F32)], axis=0)
    mod_all = _modulation(rows, w_mod, b_mod).reshape(depth, SUBLANES, 3, d)

    w = _prepare_weights(p)
    tabs = _rope_tables(n, n_ctx)
    xs, cs = x, ctx
    for layer in range(depth):
        upd = layer < depth - 1
        mq, mk, mv, sq, sk, sv, aq, ak, av, zl = _project(xs, cs, mod_all, w, tabs, layer)
        ya = _dense_attention(mq, mk, mv, upd)
        yb = _window_attention(swa_sink, sq, sk, sv, upd, layer)
        yc = _dense_attention(aq, ak, av, upd)
        yf, yr = _rglru(zl, w, layer)
        if cs is not None and not upd:
            xs, cs = jnp.concatenate([cs, xs], axis=1), None
        xs, cs = _merge(xs, cs, mod_all, w, ya, yb, yc, yf, yr, upd, layer), None
    return xs
```

```python
import functools

import numpy as np
import jax
import jax.numpy as jnp
from jax import lax
from jax.experimental import pallas as pl
from jax.experimental.pallas import tpu as pltpu

GRID_W = 64
N_BRANCH = 4
HEAD_DIM = 64
MLA_HEADS = 4
MLA_NOPE = 64
MLA_ROPE = 32
MLA_V = 64
MLA_QK = MLA_NOPE + MLA_ROPE
SWA_Q_HEADS = 4
SWA_KV_HEADS = 2
WINDOW = 128
AXA_Q_HEADS = 4
AXA_KV_HEADS = 2
LRU_BLOCKS = 4
LRU_C = 8.0
CONV_W = 4
CONV_LEFT = 2
ROPE_THETA = 10000.0
RMS_EPS = 1e-6
NEG_INF = -1e30
LOG2E = 1.4426950408889634

LANES = 128
SUBLANES = 8
TILE = 256
SUB_BATCH = 2
KEY_CHUNKS = 4
VMEM_LIMIT = 56 * 1024 * 1024

F32 = jnp.float32
BF16 = jnp.bfloat16

VEC_NW, VEC_GCQ, VEC_GAINS, VEC_CONVW, VEC_CONVB, VEC_BR, VEC_BI, VEC_LAM, VEC_ROWS = 0, 1, 2, 3, 7, 8, 10, 12, 16
GAIN_CKV, GAIN_MQ, GAIN_MK, GAIN_SQ, GAIN_SK, GAIN_AQ, GAIN_AK = range(7)


def _dot(a, b):
    return jnp.dot(a, b, preferred_element_type=F32)


def _dot_nt(a, b):
    return lax.dot_general(a, b, (((1,), (1,)), ((), ())), preferred_element_type=F32)


def _split(a):
    hi = a.astype(BF16)
    lo = (a - hi.astype(F32)).astype(BF16)
    return hi, lo


def _dot3(a, w_hi, w_lo):
    a_hi, a_lo = _split(a)
    return _dot(a_hi, w_hi) + _dot(a_lo, w_hi) + _dot(a_hi, w_lo)


def _sigmoid(v):
    return 1.0 / (1.0 + jnp.exp(-v))


def _cparams(sem):
    return pltpu.CompilerParams(dimension_semantics=sem, vmem_limit_bytes=VMEM_LIMIT)


def _layer_spec(arr, layer):
    nd = arr.ndim - 1
    return pl.BlockSpec((1,) + arr.shape[1:], lambda *_: (layer,) + (0,) * nd)


def _gain(vec_ref, slot):
    return vec_ref[0, VEC_GAINS:VEC_GAINS + 1, slot * LANES:(slot + 1) * LANES]


def _mod_kernel(c_ref, w_ref, b_ref, o_ref):
    v = c_ref[...]
    a = v * _sigmoid(v)
    w_hi, w_lo = _split(w_ref[0])
    o_ref[0] = _dot3(a, w_hi, w_lo) + b_ref[0]


def _modulation(cvec, w_mod, b_mod):
    depth, d, d3 = w_mod.shape
    r = cvec.shape[0]
    bn = 512
    return pl.pallas_call(
        _mod_kernel,
        grid=(depth, d3 // bn),
        in_specs=[pl.BlockSpec((r, d), lambda l, j: (0, 0)),
                  pl.BlockSpec((1, d, bn), lambda l, j: (l, 0, j)),
                  pl.BlockSpec((1, 1, bn), lambda l, j: (l, 0, j))],
        out_specs=pl.BlockSpec((1, r, bn), lambda l, j: (l, 0, j)),
        out_shape=jax.ShapeDtypeStruct((depth, r, d3), F32),
        compiler_params=_cparams(("arbitrary", "arbitrary")),
        name="modulation",
    )(cvec, w_mod, b_mod.reshape(depth, 1, d3))


def _modulated_norm(x, mod, vec_ref):
    ms = jnp.mean(x * x, axis=-1, keepdims=True)
    y = x * lax.rsqrt(ms + RMS_EPS) * vec_ref[0, VEC_NW:VEC_NW + 1, :]
    return y * (1.0 + mod[1:2, :]) + mod[0:1, :]


def _stream_tile(x_ref, ctx_ref, is_ctx, j=0):
    if ctx_ref is None:
        return x_ref[j]
    return jnp.where(is_ctx, ctx_ref[j], x_ref[j])


def _lane_rinv(v, n):
    return lax.rsqrt(jnp.sum(v * v, axis=-1, keepdims=True) * (1.0 / n) + RMS_EPS)


def _head_rinv(v, gmat):
    hi, lo = _split(v * v)
    ss = _dot(hi, gmat) + _dot(lo, gmat)
    return lax.rsqrt(ss * (1.0 / HEAD_DIM) + RMS_EPS)


def _proj_kernel(*refs, split_ctx):
    x_ref, ctx_ref = (refs[0], refs[1]) if split_ctx else (refs[0], None)
    (mod_ref, vec_ref, wp_ref, wuq_ref, wuqr_ref, wuk_ref, wuv_ref, gmat_ref,
     cosm_ref, sinm_ref, cosh_ref, sinh_ref,
     mq_o, mk_o, mv_o, sq_o, sk_o, sv_o, aq_o, ak_o, av_o, lru_o) = refs[2 if split_ctx else 1:]
    n_sub = x_ref.shape[0]
    is_ctx = pl.program_id(0) == 0
    def project_in(j):
        x = _stream_tile(x_ref, ctx_ref, is_ctx, j)
        mod = mod_ref[0, pl.ds(jnp.where(is_ctx, 0, j), 1)][0]
        hb = _modulated_norm(x, mod, vec_ref).astype(BF16)
        return _dot(hb, wp_ref[0])

    t = x_ref.shape[1]
    lo = lax.broadcasted_iota(jnp.int32, (t, LANES), 1) < HEAD_DIM
    cosm, sinm = cosm_ref[...], sinm_ref[...]
    cosh, sinh = cosh_ref[...], sinh_ref[...]
    gmat = gmat_ref[...]

    def pair_store(j, out, slab):
        swapped = pltpu.roll(slab, HEAD_DIM, 1)
        out[j, 0] = jnp.where(lo, slab, 0.0).astype(BF16)
        out[j, 1] = jnp.where(lo, 0.0, swapped).astype(BF16)
        out[j, 2] = jnp.where(lo, swapped, 0.0).astype(BF16)
        out[j, 3] = jnp.where(lo, 0.0, slab).astype(BF16)

    def heads(j, z):
        g_cq = vec_ref[0, VEC_GCQ:VEC_GCQ + 1, 0:256]
        cq = (z[:, 0:256] * _lane_rinv(z[:, 0:256], 256.0) * g_cq).astype(BF16)
        q = _dot(cq, wuq_ref[0])
        qr = _dot(cq, wuqr_ref[0])
        ckv = (z[:, 256:384] * _lane_rinv(z[:, 256:384], 128.0) * _gain(vec_ref, GAIN_CKV)).astype(BF16)
        kn = _dot(ckv, wuk_ref[0])
        vv = _dot(ckv, wuv_ref[0])
        kr = z[:, 384:512]
        krr = z[:, 512:640]
        gc_q = _gain(vec_ref, GAIN_MQ) * cosm
        gc_k = _gain(vec_ref, GAIN_MK) * cosm
        for hh in range(MLA_HEADS):
            sl = slice(hh * LANES, (hh + 1) * LANES)
            r = _lane_rinv(q[:, sl], float(MLA_QK))
            mq_o[j, hh] = ((q[:, sl] * r) * gc_q + (qr[:, sl] * r) * sinm).astype(BF16)
            ks = kn[:, sl] + kr
            r = _lane_rinv(ks, float(MLA_QK))
            mk_o[j, hh] = ((ks * r) * gc_k + (krr * r) * sinm).astype(BF16)
        mv_o[j] = vv.T.astype(BF16)

        def gqa(c0, g_q, g_k, q_o, k_o, v_o, v_transposed):
            zq, zqr = z[:, c0:c0 + 256], z[:, c0 + 256:c0 + 512]
            r = _head_rinv(zq, gmat)
            gc = _gain(vec_ref, g_q) * cosh
            for i in range(2):
                sl = slice(i * LANES, (i + 1) * LANES)
                q_o[j, i] = ((zq[:, sl] * r[:, sl]) * gc + (zqr[:, sl] * r[:, sl]) * sinh).astype(BF16)
            zk, zkr = z[:, c0 + 512:c0 + 640], z[:, c0 + 640:c0 + 768]
            r = _head_rinv(zk, gmat[:LANES, :LANES])
            ks = (zk * r) * (_gain(vec_ref, g_k) * cosh) + (zkr * r) * sinh
            pair_store(j, k_o, ks)
            zv = z[:, c0 + 768:c0 + 896]
            if v_transposed:
                v_o[j] = zv.T.astype(BF16)
            else:
                pair_store(j, v_o, zv)

        gqa(640, GAIN_SQ, GAIN_SK, sq_o, sk_o, sv_o, False)
        gqa(1536, GAIN_AQ, GAIN_AK, aq_o, ak_o, av_o, True)
        lru_o[j] = z[:, 2432:2688]

    for j in range(n_sub):
        heads(j, project_in(j))


def _project(x, ctx, mod_all, w, tabs, layer):
    split_ctx = ctx is not None
    b, d = x.shape[0], x.shape[2]
    s = x.shape[1] + (ctx.shape[1] if split_ctx else 0)
    nt = s // TILE
    ctx_row = b

    assert b % SUB_BATCH == 0 and ctx_row % SUB_BATCH == 0
    nbb = b // SUB_BATCH

    def tile_spec(width):
        return pl.BlockSpec((SUB_BATCH, TILE, width), lambda t, i: (i, t, 0))

    def head_spec(nh):
        return pl.BlockSpec((SUB_BATCH, nh, TILE, LANES), lambda t, i: (i, 0, t, 0))

    def head_spec_t(nh):
        return pl.BlockSpec((SUB_BATCH, nh * HEAD_DIM, TILE), lambda t, i: (i, 0, t))

    if split_ctx:
        streams = [x, ctx]
        stream_specs = [pl.BlockSpec((SUB_BATCH, TILE, d), lambda t, i: (i, jnp.maximum(t - 1, 0), 0)),
                        pl.BlockSpec((SUB_BATCH, TILE, d), lambda t, i: (jnp.where(t == 0, i, nbb - 1), 0, 0))]
    else:
        streams, stream_specs = [x], [tile_spec(d)]
    tab_spec = pl.BlockSpec((TILE, LANES), lambda t, i: (t, 0))
    consts = [w["vecs"], w["wp"], w["wuq"], w["wuqr"], w["wuk"], w["wuv"]]
    in_specs = (stream_specs
                + [pl.BlockSpec((1, SUB_BATCH, 3, d),
                                lambda t, i: (layer, jnp.where(t == 0, ctx_row // SUB_BATCH, i), 0, 0))]
                + [_layer_spec(c, layer) for c in consts]
                + [pl.BlockSpec(w["gmat"].shape, lambda t, i: (0, 0))] + [tab_spec] * 4)
    n_slabs = [MLA_HEADS, MLA_HEADS, MLA_HEADS, 2, 4, 4, 2, 4, AXA_KV_HEADS]
    transposed = [False, False, True, False, False, False, False, False, True]
    out_specs = ([head_spec_t(nh) if tr else head_spec(nh) for nh, tr in zip(n_slabs, transposed)]
                 + [tile_spec(4 * HEAD_DIM)])
    out_shape = ([jax.ShapeDtypeStruct((b, nh * HEAD_DIM, s) if tr else (b, nh, s, LANES), BF16)
                  for nh, tr in zip(n_slabs, transposed)]
                 + [jax.ShapeDtypeStruct((b, s, 4 * HEAD_DIM), F32)])
    return pl.pallas_call(
        functools.partial(_proj_kernel, split_ctx=split_ctx),
        grid=(nt, nbb),
        in_specs=in_specs,
        out_specs=out_specs,
        out_shape=out_shape,
        compiler_params=_cparams(("arbitrary", "arbitrary")),
        name="project",
    )(*streams, mod_all, *consts, w["gmat"], *tabs)


def _dense_attn_kernel(q_ref, k_ref, v_ref, o_ref, *, q_share, ctx_tile_first):
    s_len = k_ref.shape[2]
    n_heads = k_ref.shape[1]
    v_share = n_heads * HEAD_DIM // v_ref.shape[1]

    def attend(n_keys):
        n_groups = n_keys // LANES
        n_chunks = min(KEY_CHUNKS, n_groups)
        edges = [(c * n_groups // n_chunks) * LANES for c in range(n_chunks + 1)]
        chunks = [slice(a, b) for a, b in zip(edges[:-1], edges[1:])]

        def scores(unit):
            j, hh = unit
            q = q_ref[j, hh // q_share]
            return [_dot_nt(k_ref[j, hh, sl, :], q) for sl in chunks]

        def head_out(unit, st):
            j, hh = unit
            m = functools.reduce(jnp.maximum, [jnp.max(sc, axis=0, keepdims=True) for sc in st])
            pt = [jnp.exp2(sc - m) for sc in st]
            l = sum(jnp.sum(pc, axis=0, keepdims=True) for pc in pt)
            band = slice(hh // v_share * HEAD_DIM, (hh // v_share + 1) * HEAD_DIM)
            ot = sum(_dot(v_ref[j, band, sl], pc.astype(BF16)) for sl, pc in zip(chunks, pt))
            return ot / l

        units = [(j, hh) for j in range(q_ref.shape[0]) for hh in range(n_heads)]
        st_next = scores(units[0])
        outs = []
        for u, unit in enumerate(units):
            st = st_next
            if u + 1 < len(units):
                st_next = scores(units[u + 1])
            outs.append(head_out(unit, st))
        for j in range(q_ref.shape[0]):
            o_ref[j] = jnp.concatenate(outs[j * n_heads:(j + 1) * n_heads], axis=0).T

    if ctx_tile_first:
        pl.when(pl.program_id(1) == 0)(lambda: attend(TILE))
        pl.when(pl.program_id(1) > 0)(lambda: attend(s_len))
    else:
        attend(s_len)


def _dense_attention(q, k, v, with_ctx_queries):
    b, nq_slabs, s, _ = q.shape
    nh = k.shape[1]
    nt = s // TILE
    off = 0 if with_ctx_queries else 1
    nq = nt - off
    assert b % SUB_BATCH == 0
    kern = functools.partial(_dense_attn_kernel, q_share=nh // nq_slabs,
                             ctx_tile_first=with_ctx_queries)
    return pl.pallas_call(
        kern,
        grid=(b // SUB_BATCH, nq),
        in_specs=[pl.BlockSpec((SUB_BATCH, nq_slabs, TILE, LANES), lambda i, t: (i, 0, t + off, 0)),
                  pl.BlockSpec((SUB_BATCH, nh, s, LANES), lambda i, t: (i, 0, 0, 0)),
                  pl.BlockSpec((SUB_BATCH,) + v.shape[1:], lambda i, t: (i, 0, 0))],
        out_specs=pl.BlockSpec((SUB_BATCH, TILE, 2 * LANES), lambda i, t: (i, t, 0)),
        out_shape=jax.ShapeDtypeStruct((b, nq * TILE, 2 * LANES), F32),
        compiler_params=_cparams(("arbitrary", "arbitrary")),
        name="dense_attention",
    )(q, k, v)


def _window_attn_kernel(sink_ref, q_ref, k_ref, v_ref, o_ref, *, tile_off, layer):
    s_len = k_ref.shape[2]
    n_tiles = s_len // TILE
    ti = pl.program_id(1) + tile_off
    is_lat = ti > 0
    t0 = ti * TILE
    half = TILE // 2
    left0 = pl.multiple_of(jnp.maximum(t0 - half, 0), half)
    cen0 = pl.multiple_of(t0, TILE)
    right0 = pl.multiple_of(jnp.minimum(t0 + TILE, s_len - half), half)
    r_c = lax.broadcasted_iota(jnp.int32, (TILE, TILE), 0)
    c_c = lax.broadcasted_iota(jnp.int32, (TILE, TILE), 1)
    r_s = lax.broadcasted_iota(jnp.int32, (TILE, half), 0)
    c_s = lax.broadcasted_iota(jnp.int32, (TILE, half), 1)
    ok_cen = (jnp.abs(r_c - c_c) <= WINDOW) & is_lat
    ok_left = ((r_s - c_s + half) <= WINDOW) & (ti > 1)
    ok_right = ((c_s - r_s + TILE) <= WINDOW) & is_lat & (ti < n_tiles - 1)
    def lane_groups(a):
        return [a[:, g * LANES:(g + 1) * LANES] for g in range(a.shape[1] // LANES)]

    def scores(unit):
        j, hh = unit
        q = q_ref[j, hh // 2]
        k = lambda start, size: k_ref[j, hh, pl.ds(start, size), :]
        return (_dot_nt(q, k(0, TILE)),
                jnp.where(ok_left, _dot_nt(q, k(left0, half)), NEG_INF),
                jnp.where(ok_cen, _dot_nt(q, k(cen0, TILE)), NEG_INF),
                jnp.where(ok_right, _dot_nt(q, k(right0, half)), NEG_INF))

    def head_out(unit, sc):
        j, hh = unit
        sink = sink_ref[layer, hh] * LOG2E
        m = functools.reduce(jnp.maximum, [g for piece in sc for g in lane_groups(piece)])
        m = jnp.maximum(jnp.max(m, -1, keepdims=True), sink)
        ps = [jnp.exp2(piece - m) for piece in sc]
        den = sum(g for piece in ps for g in lane_groups(piece))
        den = jnp.sum(den, -1, keepdims=True) + jnp.exp2(sink - m)
        v = lambda start, size: v_ref[j, hh, pl.ds(start, size), :]
        vs = (v(0, TILE), v(left0, half), v(cen0, TILE), v(right0, half))
        num = sum(_dot(p.astype(BF16), vv) for p, vv in zip(ps, vs))
        return num / den

    n_heads = k_ref.shape[1]
    units = [(j, hh) for j in range(q_ref.shape[0]) for hh in range(n_heads)]
    sc_next = scores(units[0])
    outs = []
    for u, unit in enumerate(units):
        sc = sc_next
        if u + 1 < len(units):
            sc_next = scores(units[u + 1])
        outs.append(head_out(unit, sc))
    for j in range(q_ref.shape[0]):
        o = outs[j * n_heads:(j + 1) * n_heads]
        o_ref[j] = jnp.concatenate([o[0] + o[1], o[2] + o[3]], axis=-1)


def _window_attention(sink, q, k, v, with_ctx_queries, layer):
    b, nq_slabs, s, _ = q.shape
    nh = k.shape[1]
    nt = s // TILE
    off = 0 if with_ctx_queries else 1
    nq = nt - off
    assert b % SUB_BATCH == 0
    kern = functools.partial(_window_attn_kernel, tile_off=off, layer=layer)
    return pl.pallas_call(
        kern,
        grid=(b // SUB_BATCH, nq),
        in_specs=[pl.BlockSpec(memory_space=pltpu.SMEM),
                  pl.BlockSpec((SUB_BATCH, nq_slabs, TILE, LANES), lambda i, t: (i, 0, t + off, 0)),
                  pl.BlockSpec((SUB_BATCH, nh, s, LANES), lambda i, t: (i, 0, 0, 0)),
                  pl.BlockSpec((SUB_BATCH, nh, s, LANES), lambda i, t: (i, 0, 0, 0))],
        out_specs=pl.BlockSpec((SUB_BATCH, TILE, 2 * LANES), lambda i, t: (i, t, 0)),
        out_shape=jax.ShapeDtypeStruct((b, nq * TILE, 2 * LANES), F32),
        compiler_params=_cparams(("arbitrary", "arbitrary")),
        name="window_attention",
    )(sink, q, k, v)


def _lru_conv(j, prev_ref, cur_ref, next_ref, prev_ok, next_ok, vec_ref):
    t = cur_ref.shape[1]
    prev = jnp.where(prev_ok, prev_ref[j], 0.0)
    nxt = jnp.where(next_ok, next_ref[j], 0.0)
    ext = jnp.concatenate([prev, cur_ref[j], nxt], axis=0)
    n = t + 2 * SUBLANES
    c = cur_ref.shape[2]

    def tap(j):
        return vec_ref[0, VEC_CONVW + j:VEC_CONVW + j + 1, 0:c]

    u = vec_ref[0, VEC_CONVB:VEC_CONVB + 1, 0:c] + cur_ref[j] * tap(CONV_LEFT)
    for j in range(CONV_W):
        off = j - CONV_LEFT
        if off == 0:
            continue
        sh = pltpu.roll(ext, (-off) % n, 0)[SUBLANES:SUBLANES + t]
        u = u + sh * tap(j)
    return u


def _lru_scan_tile(u, d, wr_hi, wr_lo, wi_hi, wi_lo, vec_ref, h0, reverse):
    t, c = u.shape

    def vec(row):
        return vec_ref[0, row + d:row + d + 1, 0:c]

    r = _sigmoid(_dot3(u, wr_hi[0, d], wr_lo[0, d]) + vec(VEC_BR))
    i = _sigmoid(_dot3(u, wi_hi[0, d], wi_lo[0, d]) + vec(VEC_BI))
    lam = vec(VEC_LAM)
    softplus_neg = jnp.maximum(-lam, 0.0) + jnp.log1p(jnp.exp(-jnp.abs(lam)))
    log_a = -LRU_C * r * softplus_neg
    a = jnp.exp(log_a)
    bx = jnp.sqrt(1.0 - a * a) * (i * u)
    row = lax.broadcasted_iota(jnp.int32, u.shape, 0)
    step = 1
    while step < t:
        if reverse:
            ok = row < t - step
            a_sh = jnp.where(ok, pltpu.roll(a, t - step, 0), 1.0)
            b_sh = jnp.where(ok, pltpu.roll(bx, t - step, 0), 0.0)
        else:
            ok = row >= step
            a_sh = jnp.where(ok, pltpu.roll(a, step, 0), 1.0)
            b_sh = jnp.where(ok, pltpu.roll(bx, step, 0), 0.0)
        bx = a * b_sh + bx
        a = a * a_sh
        step *= 2
    h = a * h0 + bx
    h_end = h[0:1, :] if reverse else h[t - 1:t, :]
    return h, h_end


def _lru_kernel(fp_ref, fc_ref, fn_ref, rp_ref, rc_ref, rn_ref, vec_ref,
                wr_hi, wr_lo, wi_hi, wi_lo, yf_ref, yb_ref, hf_s, hb_s):
    j = pl.program_id(1)
    nt = pl.num_programs(1)

    @pl.when(j == 0)
    def _():
        hf_s[...] = jnp.zeros_like(hf_s)
        hb_s[...] = jnp.zeros_like(hb_s)

    def bounds(ti):
        return ti >= 2, (ti >= 1) & (ti < nt - 1)

    tf = j
    tr = jnp.where(j == 0, 0, nt - j)
    for e in range(fc_ref.shape[0]):
        p_ok, n_ok = bounds(tf)
        uf = _lru_conv(e, fp_ref, fc_ref, fn_ref, p_ok, n_ok, vec_ref)
        h, h_end = _lru_scan_tile(uf, 0, wr_hi, wr_lo, wi_hi, wi_lo, vec_ref, hf_s[e], False)
        yf_ref[e] = h
        hf_s[e] = h_end
        p_ok, n_ok = bounds(tr)
        ur = _lru_conv(e, rp_ref, rc_ref, rn_ref, p_ok, n_ok, vec_ref)
        h, h_end = _lru_scan_tile(ur, 1, wr_hi, wr_lo, wi_hi, wi_lo, vec_ref, hb_s[e], True)
        yb_ref[e] = h
        hb_s[e] = h_end


def _rglru(z, w, layer):
    b, s, c = z.shape
    nt = s // TILE
    r8 = TILE // SUBLANES
    n8 = s // SUBLANES

    def rev(j):
        return jnp.where(j == 0, 0, nt - j)

    assert b % SUB_BATCH == 0

    def cur(f):
        return pl.BlockSpec((SUB_BATCH, TILE, c), lambda i, j: (i, f(j), 0))

    def prev(f):
        return pl.BlockSpec((SUB_BATCH, SUBLANES, c), lambda i, j: (i, jnp.maximum(f(j) * r8 - 1, 0), 0))

    def nxt(f):
        return pl.BlockSpec((SUB_BATCH, SUBLANES, c), lambda i, j: (i, jnp.minimum((f(j) + 1) * r8, n8 - 1), 0))

    ident = lambda j: j
    consts = [w["vecs"], w["wr_hi"], w["wr_lo"], w["wi_hi"], w["wi_lo"]]
    return pl.pallas_call(
        _lru_kernel,
        grid=(b // SUB_BATCH, nt),
        in_specs=[prev(ident), cur(ident), nxt(ident), prev(rev), cur(rev), nxt(rev)]
                 + [_layer_spec(x, layer) for x in consts],
        out_specs=[cur(ident), cur(rev)],
        out_shape=[jax.ShapeDtypeStruct((b, s, c), F32)] * 2,
        scratch_shapes=[pltpu.VMEM((SUB_BATCH, 1, c), F32), pltpu.VMEM((SUB_BATCH, 1, c), F32)],
        compiler_params=_cparams(("arbitrary", "arbitrary")),
        name="rglru",
    )(z, z, z, z, z, z, *consts)


def _merge_kernel(*refs, split_ctx, ctx_tile_first):
    x_ref, ctx_ref = (refs[0], refs[1]) if split_ctx else (refs[0], None)
    (mod_ref, vec_ref, wg_ref, wm_ref, ya_ref, yb_ref, yc_ref, yf_ref, yr_ref,
     wb_ref, wo_ref, o_ref) = refs[2 if split_ctx else 1:]
    is_ctx = (pl.program_id(1) == 0) if ctx_tile_first else False
    d = x_ref.shape[2]
    bw = ya_ref.shape[2]
    for j in range(x_ref.shape[0]):
        x = _stream_tile(x_ref, ctx_ref, is_ctx, j)
        mod = mod_ref[0, pl.ds(jnp.where(is_ctx, 0, j), 1)][0]
        hb = _modulated_norm(x, mod, vec_ref).astype(BF16)
        zg = _dot(hb, wg_ref[0])
        branches = (ya_ref[j], yb_ref[j], yc_ref[j], yf_ref[j] + yr_ref[j])
        mix = jnp.zeros_like(x)
        for k in range(N_BRANCH):
            g = zg[:, k * bw:(k + 1) * bw]
            y = branches[k] * (g * _sigmoid(g))
            proj = _dot(y.astype(BF16), wb_ref[0, k])
            zm = _dot(hb, wm_ref[0, :, k * d:(k + 1) * d])
            mix = mix + _sigmoid(zm) * proj
        o_ref[j] = x + mod[2:3, :] * _dot(mix.astype(BF16), wo_ref[0])


def _merge(x, ctx, mod_all, w, ya, yb, yc, yf, yr, with_ctx, layer):
    split_ctx = ctx is not None
    assert with_ctx or not split_ctx
    b, d = x.shape[0], x.shape[2]
    s = x.shape[1] + (ctx.shape[1] if split_ctx else 0)
    nt = s // TILE
    off = 0 if with_ctx else 1
    nq = nt - off
    ctx_row = b
    bw = ya.shape[2]
    assert b % SUB_BATCH == 0 and ctx_row % SUB_BATCH == 0
    nbb = b // SUB_BATCH

    def stream(width):
        return pl.BlockSpec((SUB_BATCH, TILE, width), lambda i, t: (i, t + off, 0))

    def local(width):
        return pl.BlockSpec((SUB_BATCH, TILE, width), lambda i, t: (i, t, 0))

    if split_ctx:
        streams = [x, ctx]
        stream_specs = [pl.BlockSpec((SUB_BATCH, TILE, d), lambda i, t: (i, jnp.maximum(t - 1, 0), 0)),
                        pl.BlockSpec((SUB_BATCH, TILE, d), lambda i, t: (i, 0, 0))]
    else:
        streams, stream_specs = [x], [stream(d)]
    consts_a = [w["vecs"], w["wg"], w["wm"]]
    consts_b = [w["wb"], w["wo"]]
    return pl.pallas_call(
        functools.partial(_merge_kernel, split_ctx=split_ctx, ctx_tile_first=with_ctx),
        grid=(nbb, nq),
        in_specs=stream_specs
                 + [pl.BlockSpec((1, SUB_BATCH, 3, d),
                                 lambda i, t: (layer, jnp.where(t + off == 0, ctx_row // SUB_BATCH, i), 0, 0))]
                 + [_layer_spec(c, layer) for c in consts_a]
                 + [local(bw), local(bw), local(bw), stream(bw), stream(bw)]
                 + [_layer_spec(c, layer) for c in consts_b],
        out_specs=local(d),
        out_shape=jax.ShapeDtypeStruct((b, nq * TILE, d), F32),
        compiler_params=_cparams(("arbitrary", "arbitrary")),
        name="merge",
    )(*streams, mod_all, *consts_a, ya, yb, yc, yf, yr, *consts_b)


def _rope_tables(n, n_ctx):
    def axial(rot_dim):
        n_rows = n // GRID_W
        rows = np.repeat(np.arange(n_rows, dtype=np.float64), GRID_W)
        cols = np.tile(np.arange(GRID_W, dtype=np.float64), n_rows)
        quarter = rot_dim // 4
        freqs = ROPE_THETA ** (-np.arange(quarter, dtype=np.float64) / quarter)
        ang = np.concatenate([rows[:, None] * freqs, cols[:, None] * freqs], axis=-1)
        return np.cos(ang), np.sin(ang)

    def with_ctx(cos, sin):
        return (jnp.asarray(np.concatenate([np.ones((n_ctx, LANES)), cos], axis=0), F32),
                jnp.asarray(np.concatenate([np.zeros((n_ctx, LANES)), sin], axis=0), F32))

    cm, sm = axial(MLA_ROPE)
    one, zero = np.ones((n, MLA_NOPE)), np.zeros((n, MLA_NOPE))
    pad1, pad0 = np.ones((n, LANES - MLA_QK)), np.zeros((n, LANES - MLA_QK))
    cosm, sinm = with_ctx(np.concatenate([one, cm, cm, pad1], axis=-1),
                          np.concatenate([zero, sm, sm, pad0], axis=-1))
    ch, sh = axial(HEAD_DIM)
    cosh, sinh = with_ctx(np.tile(ch, (1, 4)), np.tile(sh, (1, 4)))
    return cosm, sinm, cosh, sinh


def _place(w, mat, gain=None):
    wg = w if gain is None else w * gain[:, None, :]
    return jnp.einsum("lkc,cp->lkp", wg, jnp.asarray(mat, F32), precision=lax.Precision.HIGHEST)


def _copy_matrix(n_heads, in_width, in_start, count, out_width, out_start):
    m = np.zeros((n_heads * in_width, n_heads * out_width), np.float32)
    for h in range(n_heads):
        for j in range(count):
            m[h * in_width + in_start + j, h * out_width + out_start + j] = 1.0
    return m


def _partner_matrix(n_heads, in_width, in_start, half, out_width, out_start):
    m = np.zeros((n_heads * in_width, n_heads * out_width), np.float32)
    for h in range(n_heads):
        for j in range(half):
            x1, x2 = h * in_width + in_start + j, h * in_width + in_start + half + j
            o1, o2 = h * out_width + out_start + j, h * out_width + out_start + half + j
            m[x2, o1] = -1.0
            m[x1, o2] = 1.0
    return m


IN_SIZES = (256, 128, 32, 256, 128, 128, 256, 128, 128, 256, 1024)
PREP_ROWS = 256


def _prep_kernel(wt_ref, vec_ref, pc_ref, pm_ref, ph_ref, wp_o, wg_o, wm_o):
    offs = [0]
    for sz in IN_SIZES:
        offs.append(offs[-1] + sz)

    def rows(a, b):
        return wt_ref[0, a:b, :].T

    def piece(i):
        return rows(offs[i], offs[i + 1])

    def put(start, val):
        wp_o[0, :, start:start + val.shape[1]] = val.astype(BF16)

    def partner(val, gain, mat):
        return _dot((val * gain).astype(BF16), mat)

    put(0, piece(0))
    put(256, piece(1))
    z = rows(offs[2], offs[2] + LANES)
    hi = z.astype(BF16)
    r1 = z - hi.astype(F32)
    mid = r1.astype(BF16)
    lo = (r1 - mid.astype(F32)).astype(BF16)
    pc = pc_ref[...]
    kr_slab = _dot(hi, pc) + _dot(mid, pc) + _dot(lo, pc)
    put(384, kr_slab)
    put(512, partner(kr_slab, _gain(vec_ref, GAIN_MK), pm_ref[...]))
    ph = ph_ref[...]
    col = 640
    for first, q_slot, k_slot in ((3, GAIN_SQ, GAIN_SK), (6, GAIN_AQ, GAIN_AK)):
        q, k, v = piece(first), piece(first + 1), piece(first + 2)
        g_q = _gain(vec_ref, q_slot)
        put(col, q)
        put(col + 256, partner(q, jnp.concatenate([g_q, g_q], axis=1), ph))
        put(col + 512, k)
        put(col + 640, partner(k, _gain(vec_ref, k_slot), ph[:LANES, :LANES]))
        put(col + 768, v)
        col += 896
    put(col, piece(9))
    wg_o[0] = piece(10).astype(BF16)
    for c in range(offs[11], wt_ref.shape[1], 4 * LANES):
        wm_o[0, :, c - offs[11]:c - offs[11] + 4 * LANES] = rows(c, c + 4 * LANES).astype(BF16)


def _prep_in_weights(w_in, vecs):
    depth, d, cols = w_in.shape
    n_merge = cols - sum(IN_SIZES)
    half = HEAD_DIM // 2
    mats = [_copy_matrix(1, LANES, 0, MLA_ROPE, LANES, MLA_NOPE),
            _partner_matrix(1, LANES, MLA_NOPE, MLA_ROPE // 2, LANES, MLA_NOPE),
            _partner_matrix(4, HEAD_DIM, 0, half, HEAD_DIM, 0)]
    mats = [jnp.asarray(m, BF16) for m in mats]
    wp_cols = 2688

    def out_rows(width):
        return pl.BlockSpec((1, PREP_ROWS, width), lambda l, r: (l, r, 0))

    return pl.pallas_call(
        _prep_kernel,
        grid=(depth, d // PREP_ROWS),
        in_specs=[pl.BlockSpec((1, cols, PREP_ROWS), lambda l, r: (l, 0, r)),
                  pl.BlockSpec((1,) + vecs.shape[1:], lambda l, r: (l, 0, 0))]
                 + [pl.BlockSpec(m.shape, lambda l, r: (0, 0)) for m in mats],
        out_specs=[out_rows(wp_cols), out_rows(IN_SIZES[10]), out_rows(n_merge)],
        out_shape=[jax.ShapeDtypeStruct((depth, d, wp_cols), BF16),
                   jax.ShapeDtypeStruct((depth, d, IN_SIZES[10]), BF16),
                   jax.ShapeDtypeStruct((depth, d, n_merge), BF16)],
        compiler_params=_cparams(("arbitrary", "arbitrary")),
        name="prep_weights",
    )(jnp.transpose(w_in, (0, 2, 1)), vecs, *mats)


def _prepare_weights(p):
    depth, d, _ = p["w_in"].shape
    q_scale = HEAD_DIM ** -0.5 * LOG2E
    mq_scale = MLA_QK ** -0.5 * LOG2E
    g_sq = jnp.tile(p["swa_q_norm"] * q_scale, (1, 2))
    g_sk = jnp.tile(p["swa_k_norm"], (1, 2))
    g_aq = jnp.tile(p["axa_q_norm"] * q_scale, (1, 2))
    g_ak = jnp.tile(p["axa_k_norm"], (1, 2))
    g_mq = p["mla_q_norm"] * mq_scale
    g_mk = p["mla_k_norm"]

    w_uq, w_ukv = p["mla_w_uq"], p["mla_w_ukv"]
    kv_width = MLA_NOPE + MLA_V
    wuq = _place(w_uq, _copy_matrix(MLA_HEADS, MLA_QK, 0, MLA_QK, LANES, 0)).astype(BF16)
    wuqr = _place(w_uq, _partner_matrix(MLA_HEADS, MLA_QK, MLA_NOPE, MLA_ROPE // 2, LANES, MLA_NOPE),
                  jnp.tile(g_mq, (1, MLA_HEADS))).astype(BF16)
    wuk = _place(w_ukv, _copy_matrix(MLA_HEADS, kv_width, 0, MLA_NOPE, LANES, 0)).astype(BF16)
    wuv = _place(w_ukv, _copy_matrix(MLA_HEADS, kv_width, MLA_NOPE, MLA_V, MLA_V, 0)).astype(BF16)

    def row(v):
        v = v if v.ndim == 3 else v[:, None, :]
        return jnp.pad(v, ((0, 0), (0, 0), (0, d - v.shape[2])))

    slab = lambda g: jnp.pad(g, ((0, 0), (0, LANES - g.shape[1])))
    gains = jnp.concatenate([p["mla_ckv_norm"], slab(g_mq), slab(g_mk), g_sq, g_sk, g_aq, g_ak], axis=1)
    vec_rows = [row(p["norm_w"]), row(p["mla_cq_norm"]), row(gains), row(p["lru_conv_w"]),
                row(p["lru_conv_b"]), row(p["lru_b_r"]), row(p["lru_b_i"]), row(p["lru_lambda"])]
    n_rows = sum(r.shape[1] for r in vec_rows)
    vecs = jnp.concatenate(vec_rows + [jnp.zeros((depth, VEC_ROWS - n_rows, d), F32)], axis=1)
    wp, wg, wm = _prep_in_weights(p["w_in"], vecs)

    idx = np.arange(4 * HEAD_DIM) // HEAD_DIM
    gmat = jnp.asarray(idx[:, None] == idx[None, :], BF16)

    def block_diag_hi_lo(wb):
        l, two, k, m, _ = wb.shape
        full = jnp.einsum("ldkij,kn->ldkinj", wb, jnp.eye(k, dtype=wb.dtype)).reshape(l, two, k * m, k * m)
        hi = full.astype(BF16)
        return hi, (full - hi.astype(F32)).astype(BF16)

    wr_hi, wr_lo = block_diag_hi_lo(p["lru_w_r"])
    wi_hi, wi_lo = block_diag_hi_lo(p["lru_w_i"])
    return {
        "vecs": vecs, "gmat": gmat,
        "wp": wp, "wuq": wuq, "wuqr": wuqr, "wuk": wuk, "wuv": wuv,
        "wg": wg, "wm": wm,
        "wb": p["w_branch"].astype(BF16), "wo": p["w_out"].astype(BF16),
        "wr_hi": wr_hi, "wr_lo": wr_lo, "wi_hi": wi_hi, "wi_lo": wi_lo,
    }


def kernel(x, c, ctx, c_ctx, w_mod, b_mod, norm_w, w_in, mla_cq_norm, mla_ckv_norm, mla_w_uq, mla_w_ukv, mla_q_norm, mla_k_norm, swa_q_norm, swa_k_norm, swa_sink, axa_q_norm, axa_k_norm, lru_conv_w, lru_conv_b, lru_w_r, lru_b_r, lru_w_i, lru_b_i, lru_lambda, w_branch, w_out):
    p = dict(norm_w=norm_w, w_in=w_in, mla_cq_norm=mla_cq_norm, mla_ckv_norm=mla_ckv_norm,
             mla_w_uq=mla_w_uq, mla_w_ukv=mla_w_ukv, mla_q_norm=mla_q_norm, mla_k_norm=mla_k_norm,
             swa_q_norm=swa_q_norm, swa_k_norm=swa_k_norm,
             axa_q_norm=axa_q_norm, axa_k_norm=axa_k_norm, lru_conv_w=lru_conv_w,
             lru_conv_b=lru_conv_b, lru_w_r=lru_w_r, lru_b_r=lru_b_r, lru_w_i=lru_w_i,
             lru_b_i=lru_b_i, lru_lambda=lru_lambda, w_branch=w_branch, w_out=w_out)
    b, n, d = x.shape
    n_ctx = ctx.shape[1]
    depth = w_mod.shape[0]
    assert n_ctx == TILE and n % TILE == 0 and n % GRID_W == 0 and b + 1 <= SUBLANES

    rows = jnp.concatenate([c, c_ctx[None, :], jnp.zeros((SUBLANES - b - 1, d), F32)], axis=0)
    mod_all = _modulation(rows, w_mod, b_mod).reshape(depth, SUBLANES, 3, d)

    w = _prepare_weights(p)
    tabs = _rope_tables(n, n_ctx)
    xs, cs = x, ctx
    for layer in range(depth):
        upd = layer < depth - 1
        mq, mk, mv, sq, sk, sv, aq, ak, av, zl = _project(xs, cs, mod_all, w, tabs, layer)
        ya = _dense_attention(mq, mk, mv, upd)
        yb = _window_attention(swa_sink, sq, sk, sv, upd, layer)
        yc = _dense_attention(aq, ak, av, upd)
        yf, yr = _rglru(zl, w, layer)
        if cs is not None and not upd:
            xs, cs = jnp.concatenate([cs, xs], axis=1), None
        xs, cs = _merge(xs, cs, mod_all, w, ya, yb, yc, yf, yr, upd, layer), None
    return xs
```

```python
import functools

import numpy as np
import jax
import jax.numpy as jnp
from jax import lax
from jax.experimental import pallas as pl
from jax.experimental.pallas import tpu as pltpu

GRID_W = 64
N_BRANCH = 4
HEAD_DIM = 64
MLA_HEADS = 4
MLA_NOPE = 64
MLA_ROPE = 32
MLA_V = 64
MLA_QK = MLA_NOPE + MLA_ROPE
SWA_Q_HEADS = 4
SWA_KV_HEADS = 2
WINDOW = 128
AXA_Q_HEADS = 4
AXA_KV_HEADS = 2
LRU_BLOCKS = 4
LRU_C = 8.0
CONV_W = 4
CONV_LEFT = 2
ROPE_THETA = 10000.0
RMS_EPS = 1e-6
NEG_INF = -1e30
LOG2E = 1.4426950408889634

LANES = 128
SUBLANES = 8
TILE = 256
SUB_BATCH = 2
KEY_CHUNKS = 4
VMEM_LIMIT = 56 * 1024 * 1024

F32 = jnp.float32
BF16 = jnp.bfloat16

VEC_NW, VEC_GCQ, VEC_GAINS, VEC_CONVW, VEC_CONVB, VEC_BR, VEC_BI, VEC_LAM, VEC_ROWS = 0, 1, 2, 3, 7, 8, 10, 12, 16
GAIN_CKV, GAIN_MQ, GAIN_MK, GAIN_SQ, GAIN_SK, GAIN_AQ, GAIN_AK = range(7)


def _dot(a, b):
    return jnp.dot(a, b, preferred_element_type=F32)


def _dot_nt(a, b):
    return lax.dot_general(a, b, (((1,), (1,)), ((), ())), preferred_element_type=F32)


def _split(a):
    hi = a.astype(BF16)
    lo = (a - hi.astype(F32)).astype(BF16)
    return hi, lo


def _dot3(a, w_hi, w_lo):
    a_hi, a_lo = _split(a)
    return _dot(a_hi, w_hi) + _dot(a_lo, w_hi) + _dot(a_hi, w_lo)


def _sigmoid(v):
    return 1.0 / (1.0 + jnp.exp2(v * -LOG2E))


def _cparams(sem):
    return pltpu.CompilerParams(dimension_semantics=sem, vmem_limit_bytes=VMEM_LIMIT)


def _layer_spec(arr, layer):
    nd = arr.ndim - 1
    return pl.BlockSpec((1,) + arr.shape[1:], lambda *_: (layer,) + (0,) * nd)


def _gain(vec_ref, slot):
    return vec_ref[0, VEC_GAINS:VEC_GAINS + 1, slot * LANES:(slot + 1) * LANES]


def _mod_kernel(c_ref, w_ref, b_ref, o_ref):
    v = c_ref[...]
    a = v * _sigmoid(v)
    w_hi, w_lo = _split(w_ref[0])
    o_ref[0] = _dot3(a, w_hi, w_lo) + b_ref[0]


def _modulation(cvec, w_mod, b_mod):
    depth, d, d3 = w_mod.shape
    r = cvec.shape[0]
    bn = 512
    return pl.pallas_call(
        _mod_kernel,
        grid=(depth, d3 // bn),
        in_specs=[pl.BlockSpec((r, d), lambda l, j: (0, 0)),
                  pl.BlockSpec((1, d, bn), lambda l, j: (l, 0, j)),
                  pl.BlockSpec((1, 1, bn), lambda l, j: (l, 0, j))],
        out_specs=pl.BlockSpec((1, r, bn), lambda l, j: (l, 0, j)),
        out_shape=jax.ShapeDtypeStruct((depth, r, d3), F32),
        compiler_params=_cparams(("arbitrary", "arbitrary")),
        name="modulation",
    )(cvec, w_mod, b_mod.reshape(depth, 1, d3))


def _modulated_norm(x, mod, vec_ref):
    ms = jnp.mean(x * x, axis=-1, keepdims=True)
    y = x * lax.rsqrt(ms + RMS_EPS) * vec_ref[0, VEC_NW:VEC_NW + 1, :]
    return y * (1.0 + mod[1:2, :]) + mod[0:1, :]


def _stream_tile(x_ref, ctx_ref, is_ctx, j=0):
    if ctx_ref is None:
        return x_ref[j]
    return jnp.where(is_ctx, ctx_ref[j], x_ref[j])


def _lane_rinv(v, n):
    return lax.rsqrt(jnp.sum(v * v, axis=-1, keepdims=True) * (1.0 / n) + RMS_EPS)


def _head_rinv(v, gmat):
    hi, lo = _split(v * v)
    ss = _dot(hi, gmat) + _dot(lo, gmat)
    return lax.rsqrt(ss * (1.0 / HEAD_DIM) + RMS_EPS)


def _proj_kernel(*refs, split_ctx):
    x_ref, ctx_ref = (refs[0], refs[1]) if split_ctx else (refs[0], None)
    (mod_ref, vec_ref, wp_ref, wuq_ref, wuqr_ref, wuk_ref, wuv_ref, gmat_ref,
     cosm_ref, sinm_ref, cosh_ref, sinh_ref,
     mq_o, mk_o, mv_o, sq_o, sk_o, sv_o, aq_o, ak_o, av_o, lru_o) = refs[2 if split_ctx else 1:]
    n_sub = x_ref.shape[0]
    is_ctx = pl.program_id(0) == 0
    def project_in(j):
        x = _stream_tile(x_ref, ctx_ref, is_ctx, j)
        mod = mod_ref[0, pl.ds(jnp.where(is_ctx, 0, j), 1)][0]
        hb = _modulated_norm(x, mod, vec_ref).astype(BF16)
        return _dot(hb, wp_ref[0])

    t = x_ref.shape[1]
    lo = lax.broadcasted_iota(jnp.int32, (t, LANES), 1) < HEAD_DIM
    cosm, sinm = cosm_ref[...], sinm_ref[...]
    cosh, sinh = cosh_ref[...], sinh_ref[...]
    gmat = gmat_ref[...]

    def pair_store(j, out, slab):
        swapped = pltpu.roll(slab, HEAD_DIM, 1)
        out[j, 0] = jnp.where(lo, slab, 0.0).astype(BF16)
        out[j, 1] = jnp.where(lo, 0.0, swapped).astype(BF16)
        out[j, 2] = jnp.where(lo, swapped, 0.0).astype(BF16)
        out[j, 3] = jnp.where(lo, 0.0, slab).astype(BF16)

    def heads(j, z):
        g_cq = vec_ref[0, VEC_GCQ:VEC_GCQ + 1, 0:256]
        cq = (z[:, 0:256] * _lane_rinv(z[:, 0:256], 256.0) * g_cq).astype(BF16)
        q = _dot(cq, wuq_ref[0])
        qr = _dot(cq, wuqr_ref[0])
        ckv = (z[:, 256:384] * _lane_rinv(z[:, 256:384], 128.0) * _gain(vec_ref, GAIN_CKV)).astype(BF16)
        kn = _dot(ckv, wuk_ref[0])
        vv = _dot(ckv, wuv_ref[0])
        kr = z[:, 384:512]
        krr = z[:, 512:640]
        gc_q = _gain(vec_ref, GAIN_MQ) * cosm
        gc_k = _gain(vec_ref, GAIN_MK) * cosm
        for hh in range(MLA_HEADS):
            sl = slice(hh * LANES, (hh + 1) * LANES)
            r = _lane_rinv(q[:, sl], float(MLA_QK))
            mq_o[j, hh] = ((q[:, sl] * r) * gc_q + (qr[:, sl] * r) * sinm).astype(BF16)
            ks = kn[:, sl] + kr
            r = _lane_rinv(ks, float(MLA_QK))
            mk_o[j, hh] = ((ks * r) * gc_k + (krr * r) * sinm).astype(BF16)
        mv_o[j] = vv.T.astype(BF16)

        def gqa(c0, g_q, g_k, q_o, k_o, v_o, v_transposed):
            zq, zqr = z[:, c0:c0 + 256], z[:, c0 + 256:c0 + 512]
            r = _head_rinv(zq, gmat)
            gc = _gain(vec_ref, g_q) * cosh
            for i in range(2):
                sl = slice(i * LANES, (i + 1) * LANES)
                q_o[j, i] = ((zq[:, sl] * r[:, sl]) * gc + (zqr[:, sl] * r[:, sl]) * sinh).astype(BF16)
            zk, zkr = z[:, c0 + 512:c0 + 640], z[:, c0 + 640:c0 + 768]
            r = _head_rinv(zk, gmat[:LANES, :LANES])
            ks = (zk * r) * (_gain(vec_ref, g_k) * cosh) + (zkr * r) * sinh
            pair_store(j, k_o, ks)
            zv = z[:, c0 + 768:c0 + 896]
            if v_transposed:
                v_o[j] = zv.T.astype(BF16)
            else:
                pair_store(j, v_o, zv)

        gqa(640, GAIN_SQ, GAIN_SK, sq_o, sk_o, sv_o, False)
        gqa(1536, GAIN_AQ, GAIN_AK, aq_o, ak_o, av_o, True)
        lru_o[j] = z[:, 2432:2688]

    for j in range(n_sub):
        heads(j, project_in(j))


def _project(x, ctx, mod_all, w, tabs, layer):
    split_ctx = ctx is not None
    b, d = x.shape[0], x.shape[2]
    s = x.shape[1] + (ctx.shape[1] if split_ctx else 0)
    nt = s // TILE
    ctx_row = b

    assert b % SUB_BATCH == 0 and ctx_row % SUB_BATCH == 0
    nbb = b // SUB_BATCH

    def tile_spec(width):
        return pl.BlockSpec((SUB_BATCH, TILE, width), lambda t, i: (i, t, 0))

    def head_spec(nh):
        return pl.BlockSpec((SUB_BATCH, nh, TILE, LANES), lambda t, i: (i, 0, t, 0))

    def head_spec_t(nh):
        return pl.BlockSpec((SUB_BATCH, nh * HEAD_DIM, TILE), lambda t, i: (i, 0, t))

    if split_ctx:
        streams = [x, ctx]
        stream_specs = [pl.BlockSpec((SUB_BATCH, TILE, d), lambda t, i: (i, jnp.maximum(t - 1, 0), 0)),
                        pl.BlockSpec((SUB_BATCH, TILE, d), lambda t, i: (jnp.where(t == 0, i, nbb - 1), 0, 0))]
    else:
        streams, stream_specs = [x], [tile_spec(d)]
    tab_spec = pl.BlockSpec((TILE, LANES), lambda t, i: (t, 0))
    consts = [w["vecs"], w["wp"], w["wuq"], w["wuqr"], w["wuk"], w["wuv"]]
    in_specs = (stream_specs
                + [pl.BlockSpec((1, SUB_BATCH, 3, d),
                                lambda t, i: (layer, jnp.where(t == 0, ctx_row // SUB_BATCH, i), 0, 0))]
                + [_layer_spec(c, layer) for c in consts]
                + [pl.BlockSpec(w["gmat"].shape, lambda t, i: (0, 0))] + [tab_spec] * 4)
    n_slabs = [MLA_HEADS, MLA_HEADS, MLA_HEADS, 2, 4, 4, 2, 4, AXA_KV_HEADS]
    transposed = [False, False, True, False, False, False, False, False, True]
    out_specs = ([head_spec_t(nh) if tr else head_spec(nh) for nh, tr in zip(n_slabs, transposed)]
                 + [tile_spec(4 * HEAD_DIM)])
    out_shape = ([jax.ShapeDtypeStruct((b, nh * HEAD_DIM, s) if tr else (b, nh, s, LANES), BF16)
                  for nh, tr in zip(n_slabs, transposed)]
                 + [jax.ShapeDtypeStruct((b, s, 4 * HEAD_DIM), F32)])
    return pl.pallas_call(
        functools.partial(_proj_kernel, split_ctx=split_ctx),
        grid=(nt, nbb),
        in_specs=in_specs,
        out_specs=out_specs,
        out_shape=out_shape,
        compiler_params=_cparams(("arbitrary", "arbitrary")),
        name="project",
    )(*streams, mod_all, *consts, w["gmat"], *tabs)


def _dense_attn_kernel(q_ref, k_ref, v_ref, o_ref, *, q_share, ctx_tile_first):
    s_len = k_ref.shape[2]
    n_heads = k_ref.shape[1]
    v_share = n_heads * HEAD_DIM // v_ref.shape[1]

    def attend(n_keys):
        n_groups = n_keys // LANES
        n_chunks = min(KEY_CHUNKS, n_groups)
        edges = [(c * n_groups // n_chunks) * LANES for c in range(n_chunks + 1)]
        chunks = [slice(a, b) for a, b in zip(edges[:-1], edges[1:])]

        def scores(unit):
            j, hh = unit
            q = q_ref[j, hh // q_share]
            return [_dot_nt(k_ref[j, hh, sl, :], q) for sl in chunks]

        def head_out(unit, st):
            j, hh = unit
            m = functools.reduce(jnp.maximum, [jnp.max(sc, axis=0, keepdims=True) for sc in st])
            pt = [jnp.exp2(sc - m) for sc in st]
            l = sum(jnp.sum(pc, axis=0, keepdims=True) for pc in pt)
            band = slice(hh // v_share * HEAD_DIM, (hh // v_share + 1) * HEAD_DIM)
            ot = sum(_dot(v_ref[j, band, sl], pc.astype(BF16)) for sl, pc in zip(chunks, pt))
            return ot / l

        units = [(j, hh) for j in range(q_ref.shape[0]) for hh in range(n_heads)]
        st_next = scores(units[0])
        outs = []
        for u, unit in enumerate(units):
            st = st_next
            if u + 1 < len(units):
                st_next = scores(units[u + 1])
            outs.append(head_out(unit, st))
        for j in range(q_ref.shape[0]):
            o_ref[j] = jnp.concatenate(outs[j * n_heads:(j + 1) * n_heads], axis=0).T

    if ctx_tile_first:
        pl.when(pl.program_id(1) == 0)(lambda: attend(TILE))
        pl.when(pl.program_id(1) > 0)(lambda: attend(s_len))
    else:
        attend(s_len)


def _dense_attention(q, k, v, with_ctx_queries):
    b, nq_slabs, s, _ = q.shape
    nh = k.shape[1]
    nt = s // TILE
    off = 0 if with_ctx_queries else 1
    nq = nt - off
    assert b % SUB_BATCH == 0
    kern = functools.partial(_dense_attn_kernel, q_share=nh // nq_slabs,
                             ctx_tile_first=with_ctx_queries)
    return pl.pallas_call(
        kern,
        grid=(b // SUB_BATCH, nq),
        in_specs=[pl.BlockSpec((SUB_BATCH, nq_slabs, TILE, LANES), lambda i, t: (i, 0, t + off, 0)),
                  pl.BlockSpec((SUB_BATCH, nh, s, LANES), lambda i, t: (i, 0, 0, 0)),
                  pl.BlockSpec((SUB_BATCH,) + v.shape[1:], lambda i, t: (i, 0, 0))],
        out_specs=pl.BlockSpec((SUB_BATCH, TILE, 2 * LANES), lambda i, t: (i, t, 0)),
        out_shape=jax.ShapeDtypeStruct((b, nq * TILE, 2 * LANES), F32),
        compiler_params=_cparams(("arbitrary", "arbitrary")),
        name="dense_attention",
    )(q, k, v)


def _window_attn_kernel(sink_ref, q_ref, k_ref, v_ref, o_ref, *, tile_off, layer):
    s_len = k_ref.shape[2]
    n_tiles = s_len // TILE
    ti = pl.program_id(1) + tile_off
    is_lat = ti > 0
    t0 = ti * TILE
    half = TILE // 2
    left0 = pl.multiple_of(jnp.maximum(t0 - half, 0), half)
    cen0 = pl.multiple_of(t0, TILE)
    right0 = pl.multiple_of(jnp.minimum(t0 + TILE, s_len - half), half)
    r_c = lax.broadcasted_iota(jnp.int32, (TILE, TILE), 0)
    c_c = lax.broadcasted_iota(jnp.int32, (TILE, TILE), 1)
    r_s = lax.broadcasted_iota(jnp.int32, (TILE, half), 0)
    c_s = lax.broadcasted_iota(jnp.int32, (TILE, half), 1)
    ok_cen = (jnp.abs(r_c - c_c) <= WINDOW) & is_lat
    ok_left = ((r_s - c_s + half) <= WINDOW) & (ti > 1)
    ok_right = ((c_s - r_s + TILE) <= WINDOW) & is_lat & (ti < n_tiles - 1)
    def lane_groups(a):
        return [a[:, g * LANES:(g + 1) * LANES] for g in range(a.shape[1] // LANES)]

    def scores(unit):
        j, hh = unit
        q = q_ref[j, hh // 2]
        k = lambda start, size: k_ref[j, hh, pl.ds(start, size), :]
        return (_dot_nt(q, k(0, TILE)),
                jnp.where(ok_left, _dot_nt(q, k(left0, half)), NEG_INF),
                jnp.where(ok_cen, _dot_nt(q, k(cen0, TILE)), NEG_INF),
                jnp.where(ok_right, _dot_nt(q, k(right0, half)), NEG_INF))

    def head_out(unit, sc):
        j, hh = unit
        sink = sink_ref[layer, hh] * LOG2E
        m = functools.reduce(jnp.maximum, [g for piece in sc for g in lane_groups(piece)])
        m = jnp.maximum(jnp.max(m, -1, keepdims=True), sink)
        ps = [jnp.exp2(piece - m) for piece in sc]
        den = sum(g for piece in ps for g in lane_groups(piece))
        den = jnp.sum(den, -1, keepdims=True) + jnp.exp2(sink - m)
        v = lambda start, size: v_ref[j, hh, pl.ds(start, size), :]
        vs = (v(0, TILE), v(left0, half), v(cen0, TILE), v(right0, half))
        num = sum(_dot(p.astype(BF16), vv) for p, vv in zip(ps, vs))
        return num / den

    n_heads = k_ref.shape[1]
    units = [(j, hh) for j in range(q_ref.shape[0]) for hh in range(n_heads)]
    sc_next = scores(units[0])
    outs = []
    for u, unit in enumerate(units):
        sc = sc_next
        if u + 1 < len(units):
            sc_next = scores(units[u + 1])
        outs.append(head_out(unit, sc))
    for j in range(q_ref.shape[0]):
        o = outs[j * n_heads:(j + 1) * n_heads]
        o_ref[j] = jnp.concatenate([o[0] + o[1], o[2] + o[3]], axis=-1)


def _window_attention(sink, q, k, v, with_ctx_queries, layer):
    b, nq_slabs, s, _ = q.shape
    nh = k.shape[1]
    nt = s // TILE
    off = 0 if with_ctx_queries else 1
    nq = nt - off
    assert b % SUB_BATCH == 0
    kern = functools.partial(_window_attn_kernel, tile_off=off, layer=layer)
    return pl.pallas_call(
        kern,
        grid=(b // SUB_BATCH, nq),
        in_specs=[pl.BlockSpec(memory_space=pltpu.SMEM),
                  pl.BlockSpec((SUB_BATCH, nq_slabs, TILE, LANES), lambda i, t: (i, 0, t + off, 0)),
                  pl.BlockSpec((SUB_BATCH, nh, s, LANES), lambda i, t: (i, 0, 0, 0)),
                  pl.BlockSpec((SUB_BATCH, nh, s, LANES), lambda i, t: (i, 0, 0, 0))],
        out_specs=pl.BlockSpec((SUB_BATCH, TILE, 2 * LANES), lambda i, t: (i, t, 0)),
        out_shape=jax.ShapeDtypeStruct((b, nq * TILE, 2 * LANES), F32),
        compiler_params=_cparams(("arbitrary", "arbitrary")),
        name="window_attention",
    )(sink, q, k, v)


def _lru_conv(j, prev_ref, cur_ref, next_ref, prev_ok, next_ok, vec_ref):
    t = cur_ref.shape[1]
    prev = jnp.where(prev_ok, prev_ref[j], 0.0)
    nxt = jnp.where(next_ok, next_ref[j], 0.0)
    ext = jnp.concatenate([prev, cur_ref[j], nxt], axis=0)
    n = t + 2 * SUBLANES
    c = cur_ref.shape[2]

    def tap(j):
        return vec_ref[0, VEC_CONVW + j:VEC_CONVW + j + 1, 0:c]

    u = vec_ref[0, VEC_CONVB:VEC_CONVB + 1, 0:c] + cur_ref[j] * tap(CONV_LEFT)
    for j in range(CONV_W):
        off = j - CONV_LEFT
        if off == 0:
            continue
        sh = pltpu.roll(ext, (-off) % n, 0)[SUBLANES:SUBLANES + t]
        u = u + sh * tap(j)
    return u


def _lru_scan_tile(u, d, wri_ref, vec_ref, h0, reverse):
    t, c = u.shape

    def vec(row):
        return vec_ref[0, row + d:row + d + 1, 0:c]

    gates = _dot(u.astype(BF16), wri_ref[0, d])
    r = _sigmoid(gates[:, :c] + vec(VEC_BR))
    i = _sigmoid(gates[:, c:] + vec(VEC_BI))
    lam = vec(VEC_LAM)
    softplus_neg = jnp.maximum(-lam, 0.0) + jnp.log1p(jnp.exp(-jnp.abs(lam)))
    a = jnp.exp2(r * (softplus_neg * (-LRU_C * LOG2E)))
    bx = jnp.sqrt(1.0 - a * a) * (i * u)
    row = lax.broadcasted_iota(jnp.int32, u.shape, 0)
    step = 1
    while step < t:
        if reverse:
            ok = row < t - step
            a_sh = jnp.where(ok, pltpu.roll(a, t - step, 0), 1.0)
            b_sh = jnp.where(ok, pltpu.roll(bx, t - step, 0), 0.0)
        else:
            ok = row >= step
            a_sh = jnp.where(ok, pltpu.roll(a, step, 0), 1.0)
            b_sh = jnp.where(ok, pltpu.roll(bx, step, 0), 0.0)
        bx = a * b_sh + bx
        a = a * a_sh
        step *= 2
    h = a * h0 + bx
    h_end = h[0:1, :] if reverse else h[t - 1:t, :]
    return h, h_end


def _lru_kernel(fp_ref, fc_ref, fn_ref, rp_ref, rc_ref, rn_ref, vec_ref, wri_ref,
                yf_ref, yb_ref, hf_s, hb_s):
    j = pl.program_id(1)
    nt = pl.num_programs(1)

    @pl.when(j == 0)
    def _():
        hf_s[...] = jnp.zeros_like(hf_s)
        hb_s[...] = jnp.zeros_like(hb_s)

    def bounds(ti):
        return ti >= 2, (ti >= 1) & (ti < nt - 1)

    tf = j
    tr = jnp.where(j == 0, 0, nt - j)
    for e in range(fc_ref.shape[0]):
        p_ok, n_ok = bounds(tf)
        uf = _lru_conv(e, fp_ref, fc_ref, fn_ref, p_ok, n_ok, vec_ref)
        h, h_end = _lru_scan_tile(uf, 0, wri_ref, vec_ref, hf_s[e], False)
        yf_ref[e] = h
        hf_s[e] = h_end
        p_ok, n_ok = bounds(tr)
        ur = _lru_conv(e, rp_ref, rc_ref, rn_ref, p_ok, n_ok, vec_ref)
        h, h_end = _lru_scan_tile(ur, 1, wri_ref, vec_ref, hb_s[e], True)
        yb_ref[e] = h
        hb_s[e] = h_end


def _rglru(z, w, layer):
    b, s, c = z.shape
    nt = s // TILE
    r8 = TILE // SUBLANES
    n8 = s // SUBLANES

    def rev(j):
        return jnp.where(j == 0, 0, nt - j)

    assert b % SUB_BATCH == 0

    def cur(f):
        return pl.BlockSpec((SUB_BATCH, TILE, c), lambda i, j: (i, f(j), 0))

    def prev(f):
        return pl.BlockSpec((SUB_BATCH, SUBLANES, c), lambda i, j: (i, jnp.maximum(f(j) * r8 - 1, 0), 0))

    def nxt(f):
        return pl.BlockSpec((SUB_BATCH, SUBLANES, c), lambda i, j: (i, jnp.minimum((f(j) + 1) * r8, n8 - 1), 0))

    ident = lambda j: j
    consts = [w["vecs"], w["wri"]]
    return pl.pallas_call(
        _lru_kernel,
        grid=(b // SUB_BATCH, nt),
        in_specs=[prev(ident), cur(ident), nxt(ident), prev(rev), cur(rev), nxt(rev)]
                 + [_layer_spec(x, layer) for x in consts],
        out_specs=[cur(ident), cur(rev)],
        out_shape=[jax.ShapeDtypeStruct((b, s, c), F32)] * 2,
        scratch_shapes=[pltpu.VMEM((SUB_BATCH, 1, c), F32), pltpu.VMEM((SUB_BATCH, 1, c), F32)],
        compiler_params=_cparams(("arbitrary", "arbitrary")),
        name="rglru",
    )(z, z, z, z, z, z, *consts)


def _merge_kernel(*refs, split_ctx, ctx_tile_first):
    x_ref, ctx_ref = (refs[0], refs[1]) if split_ctx else (refs[0], None)
    (mod_ref, vec_ref, wg_ref, wm_ref, ya_ref, yb_ref, yc_ref, yf_ref, yr_ref,
     wb_ref, wo_ref, o_ref) = refs[2 if split_ctx else 1:]
    is_ctx = (pl.program_id(1) == 0) if ctx_tile_first else False
    d = x_ref.shape[2]
    bw = ya_ref.shape[2]
    for j in range(x_ref.shape[0]):
        x = _stream_tile(x_ref, ctx_ref, is_ctx, j)
        mod = mod_ref[0, pl.ds(jnp.where(is_ctx, 0, j), 1)][0]
        hb = _modulated_norm(x, mod, vec_ref).astype(BF16)
        zg = _dot(hb, wg_ref[0])
        branches = (ya_ref[j], yb_ref[j], yc_ref[j], yf_ref[j] + yr_ref[j])
        mix = jnp.zeros_like(x)
        for k in range(N_BRANCH):
            g = zg[:, k * bw:(k + 1) * bw]
            y = branches[k] * (g * _sigmoid(g))
            proj = _dot(y.astype(BF16), wb_ref[0, k])
            zm = _dot(hb, wm_ref[0, :, k * d:(k + 1) * d])
            mix = mix + _sigmoid(zm) * proj
        o_ref[j] = x + mod[2:3, :] * _dot(mix.astype(BF16), wo_ref[0])


def _merge(x, ctx, mod_all, w, ya, yb, yc, yf, yr, with_ctx, layer):
    split_ctx = ctx is not None
    assert with_ctx or not split_ctx
    b, d = x.shape[0], x.shape[2]
    s = x.shape[1] + (ctx.shape[1] if split_ctx else 0)
    nt = s // TILE
    off = 0 if with_ctx else 1
    nq = nt - off
    ctx_row = b
    bw = ya.shape[2]
    assert b % SUB_BATCH == 0 and ctx_row % SUB_BATCH == 0
    nbb = b // SUB_BATCH

    def stream(width):
        return pl.BlockSpec((SUB_BATCH, TILE, width), lambda i, t: (i, t + off, 0))

    def local(width):
        return pl.BlockSpec((SUB_BATCH, TILE, width), lambda i, t: (i, t, 0))

    if split_ctx:
        streams = [x, ctx]
        stream_specs = [pl.BlockSpec((SUB_BATCH, TILE, d), lambda i, t: (i, jnp.maximum(t - 1, 0), 0)),
                        pl.BlockSpec((SUB_BATCH, TILE, d), lambda i, t: (i, 0, 0))]
    else:
        streams, stream_specs = [x], [stream(d)]
    consts_a = [w["vecs"], w["wg"], w["wm"]]
    consts_b = [w["wb"], w["wo"]]
    return pl.pallas_call(
        functools.partial(_merge_kernel, split_ctx=split_ctx, ctx_tile_first=with_ctx),
        grid=(nbb, nq),
        in_specs=stream_specs
                 + [pl.BlockSpec((1, SUB_BATCH, 3, d),
                                 lambda i, t: (layer, jnp.where(t + off == 0, ctx_row // SUB_BATCH, i), 0, 0))]
                 + [_layer_spec(c, layer) for c in consts_a]
                 + [local(bw), local(bw), local(bw), stream(bw), stream(bw)]
                 + [_layer_spec(c, layer) for c in consts_b],
        out_specs=local(d),
        out_shape=jax.ShapeDtypeStruct((b, nq * TILE, d), F32),
        compiler_params=_cparams(("arbitrary", "arbitrary")),
        name="merge",
    )(*streams, mod_all, *consts_a, ya, yb, yc, yf, yr, *consts_b)


def _rope_tables(n, n_ctx):
    def axial(rot_dim):
        n_rows = n // GRID_W
        rows = np.repeat(np.arange(n_rows, dtype=np.float64), GRID_W)
        cols = np.tile(np.arange(GRID_W, dtype=np.float64), n_rows)
        quarter = rot_dim // 4
        freqs = ROPE_THETA ** (-np.arange(quarter, dtype=np.float64) / quarter)
        ang = np.concatenate([rows[:, None] * freqs, cols[:, None] * freqs], axis=-1)
        return np.cos(ang), np.sin(ang)

    def with_ctx(cos, sin):
        return (jnp.asarray(np.concatenate([np.ones((n_ctx, LANES)), cos], axis=0), F32),
                jnp.asarray(np.concatenate([np.zeros((n_ctx, LANES)), sin], axis=0), F32))

    cm, sm = axial(MLA_ROPE)
    one, zero = np.ones((n, MLA_NOPE)), np.zeros((n, MLA_NOPE))
    pad1, pad0 = np.ones((n, LANES - MLA_QK)), np.zeros((n, LANES - MLA_QK))
    cosm, sinm = with_ctx(np.concatenate([one, cm, cm, pad1], axis=-1),
                          np.concatenate([zero, sm, sm, pad0], axis=-1))
    ch, sh = axial(HEAD_DIM)
    cosh, sinh = with_ctx(np.tile(ch, (1, 4)), np.tile(sh, (1, 4)))
    return cosm, sinm, cosh, sinh


def _place(w, mat, gain=None):
    wg = w if gain is None else w * gain[:, None, :]
    return jnp.einsum("lkc,cp->lkp", wg, jnp.asarray(mat, F32), precision=lax.Precision.HIGHEST)


def _copy_matrix(n_heads, in_width, in_start, count, out_width, out_start):
    m = np.zeros((n_heads * in_width, n_heads * out_width), np.float32)
    for h in range(n_heads):
        for j in range(count):
            m[h * in_width + in_start + j, h * out_width + out_start + j] = 1.0
    return m


def _partner_matrix(n_heads, in_width, in_start, half, out_width, out_start):
    m = np.zeros((n_heads * in_width, n_heads * out_width), np.float32)
    for h in range(n_heads):
        for j in range(half):
            x1, x2 = h * in_width + in_start + j, h * in_width + in_start + half + j
            o1, o2 = h * out_width + out_start + j, h * out_width + out_start + half + j
            m[x2, o1] = -1.0
            m[x1, o2] = 1.0
    return m


IN_SIZES = (256, 128, 32, 256, 128, 128, 256, 128, 128, 256, 1024)
PREP_ROWS = 256


def _prep_kernel(wt_ref, vec_ref, pc_ref, pm_ref, ph_ref, wp_o, wg_o, wm_o):
    offs = [0]
    for sz in IN_SIZES:
        offs.append(offs[-1] + sz)

    def rows(a, b):
        return wt_ref[0, a:b, :].T

    def piece(i):
        return rows(offs[i], offs[i + 1])

    def put(start, val):
        wp_o[0, :, start:start + val.shape[1]] = val.astype(BF16)

    def partner(val, gain, mat):
        return _dot((val * gain).astype(BF16), mat)

    put(0, piece(0))
    put(256, piece(1))
    z = rows(offs[2], offs[2] + LANES)
    hi = z.astype(BF16)
    r1 = z - hi.astype(F32)
    mid = r1.astype(BF16)
    lo = (r1 - mid.astype(F32)).astype(BF16)
    pc = pc_ref[...]
    kr_slab = _dot(hi, pc) + _dot(mid, pc) + _dot(lo, pc)
    put(384, kr_slab)
    put(512, partner(kr_slab, _gain(vec_ref, GAIN_MK), pm_ref[...]))
    ph = ph_ref[...]
    col = 640
    for first, q_slot, k_slot in ((3, GAIN_SQ, GAIN_SK), (6, GAIN_AQ, GAIN_AK)):
        q, k, v = piece(first), piece(first + 1), piece(first + 2)
        g_q = _gain(vec_ref, q_slot)
        put(col, q)
        put(col + 256, partner(q, jnp.concatenate([g_q, g_q], axis=1), ph))
        put(col + 512, k)
        put(col + 640, partner(k, _gain(vec_ref, k_slot), ph[:LANES, :LANES]))
        put(col + 768, v)
        col += 896
    put(col, piece(9))
    wg_o[0] = piece(10).astype(BF16)
    for c in range(offs[11], wt_ref.shape[1], 4 * LANES):
        wm_o[0, :, c - offs[11]:c - offs[11] + 4 * LANES] = rows(c, c + 4 * LANES).astype(BF16)


def _prep_in_weights(w_in, vecs):
    depth, d, cols = w_in.shape
    n_merge = cols - sum(IN_SIZES)
    half = HEAD_DIM // 2
    mats = [_copy_matrix(1, LANES, 0, MLA_ROPE, LANES, MLA_NOPE),
            _partner_matrix(1, LANES, MLA_NOPE, MLA_ROPE // 2, LANES, MLA_NOPE),
            _partner_matrix(4, HEAD_DIM, 0, half, HEAD_DIM, 0)]
    mats = [jnp.asarray(m, BF16) for m in mats]
    wp_cols = 2688

    def out_rows(width):
        return pl.BlockSpec((1, PREP_ROWS, width), lambda l, r: (l, r, 0))

    return pl.pallas_call(
        _prep_kernel,
        grid=(depth, d // PREP_ROWS),
        in_specs=[pl.BlockSpec((1, cols, PREP_ROWS), lambda l, r: (l, 0, r)),
                  pl.BlockSpec((1,) + vecs.shape[1:], lambda l, r: (l, 0, 0))]
                 + [pl.BlockSpec(m.shape, lambda l, r: (0, 0)) for m in mats],
        out_specs=[out_rows(wp_cols), out_rows(IN_SIZES[10]), out_rows(n_merge)],
        out_shape=[jax.ShapeDtypeStruct((depth, d, wp_cols), BF16),
                   jax.ShapeDtypeStruct((depth, d, IN_SIZES[10]), BF16),
                   jax.ShapeDtypeStruct((depth, d, n_merge), BF16)],
        compiler_params=_cparams(("arbitrary", "arbitrary")),
        name="prep_weights",
    )(jnp.transpose(w_in, (0, 2, 1)), vecs, *mats)


def _prepare_weights(p):
    depth, d, _ = p["w_in"].shape
    q_scale = HEAD_DIM ** -0.5 * LOG2E
    mq_scale = MLA_QK ** -0.5 * LOG2E
    g_sq = jnp.tile(p["swa_q_norm"] * q_scale, (1, 2))
    g_sk = jnp.tile(p["swa_k_norm"], (1, 2))
    g_aq = jnp.tile(p["axa_q_norm"] * q_scale, (1, 2))
    g_ak = jnp.tile(p["axa_k_norm"], (1, 2))
    g_mq = p["mla_q_norm"] * mq_scale
    g_mk = p["mla_k_norm"]

    w_uq, w_ukv = p["mla_w_uq"], p["mla_w_ukv"]
    kv_width = MLA_NOPE + MLA_V
    wuq = _place(w_uq, _copy_matrix(MLA_HEADS, MLA_QK, 0, MLA_QK, LANES, 0)).astype(BF16)
    wuqr = _place(w_uq, _partner_matrix(MLA_HEADS, MLA_QK, MLA_NOPE, MLA_ROPE // 2, LANES, MLA_NOPE),
                  jnp.tile(g_mq, (1, MLA_HEADS))).astype(BF16)
    wuk = _place(w_ukv, _copy_matrix(MLA_HEADS, kv_width, 0, MLA_NOPE, LANES, 0)).astype(BF16)
    wuv = _place(w_ukv, _copy_matrix(MLA_HEADS, kv_width, MLA_NOPE, MLA_V, MLA_V, 0)).astype(BF16)

    def row(v):
        v = v if v.ndim == 3 else v[:, None, :]
        return jnp.pad(v, ((0, 0), (0, 0), (0, d - v.shape[2])))

    slab = lambda g: jnp.pad(g, ((0, 0), (0, LANES - g.shape[1])))
    gains = jnp.concatenate([p["mla_ckv_norm"], slab(g_mq), slab(g_mk), g_sq, g_sk, g_aq, g_ak], axis=1)
    vec_rows = [row(p["norm_w"]), row(p["mla_cq_norm"]), row(gains), row(p["lru_conv_w"]),
                row(p["lru_conv_b"]), row(p["lru_b_r"]), row(p["lru_b_i"]), row(p["lru_lambda"])]
    n_rows = sum(r.shape[1] for r in vec_rows)
    vecs = jnp.concatenate(vec_rows + [jnp.zeros((depth, VEC_ROWS - n_rows, d), F32)], axis=1)
    wp, wg, wm = _prep_in_weights(p["w_in"], vecs)

    idx = np.arange(4 * HEAD_DIM) // HEAD_DIM
    gmat = jnp.asarray(idx[:, None] == idx[None, :], BF16)

    def block_diag(wb):
        l, two, k, m, _ = wb.shape
        return jnp.einsum("ldkij,kn->ldkinj", wb, jnp.eye(k, dtype=wb.dtype)).reshape(l, two, k * m, k * m)

    wri = jnp.concatenate([block_diag(p["lru_w_r"]), block_diag(p["lru_w_i"])], axis=-1).astype(BF16)
    return {
        "vecs": vecs, "gmat": gmat,
        "wp": wp, "wuq": wuq, "wuqr": wuqr, "wuk": wuk, "wuv": wuv,
        "wg": wg, "wm": wm,
        "wb": p["w_branch"].astype(BF16), "wo": p["w_out"].astype(BF16),
        "wri": wri,
    }


def kernel(x, c, ctx, c_ctx, w_mod, b_mod, norm_w, w_in, mla_cq_norm, mla_ckv_norm, mla_w_uq, mla_w_ukv, mla_q_norm, mla_k_norm, swa_q_norm, swa_k_norm, swa_sink, axa_q_norm, axa_k_norm, lru_conv_w, lru_conv_b, lru_w_r, lru_b_r, lru_w_i, lru_b_i, lru_lambda, w_branch, w_out):
    p = dict(norm_w=norm_w, w_in=w_in, mla_cq_norm=mla_cq_norm, mla_ckv_norm=mla_ckv_norm,
             mla_w_uq=mla_w_uq, mla_w_ukv=mla_w_ukv, mla_q_norm=mla_q_norm, mla_k_norm=mla_k_norm,
             swa_q_norm=swa_q_norm, swa_k_norm=swa_k_norm,
             axa_q_norm=axa_q_norm, axa_k_norm=axa_k_norm, lru_conv_w=lru_conv_w,
             lru_conv_b=lru_conv_b, lru_w_r=lru_w_r, lru_b_r=lru_b_r, lru_w_i=lru_w_i,
             lru_b_i=lru_b_i, lru_lambda=lru_lambda, w_branch=w_branch, w_out=w_out)
    b, n, d = x.shape
    n_ctx = ctx.shape[1]
    depth = w_mod.shape[0]
    assert n_ctx == TILE and n % TILE == 0 and n % GRID_W == 0 and b + 1 <= SUBLANES

    rows = jnp.concatenate([c, c_ctx[None, :], jnp.zeros((SUBLANES - b - 1, d), F32)], axis=0)
    mod_all = _modulation(rows, w_mod, b_mod).reshape(depth, SUBLANES, 3, d)

    w = _prepare_weights(p)
    tabs = _rope_tables(n, n_ctx)
    xs, cs = x, ctx
    for layer in range(depth):
        upd = layer < depth - 1
        mq, mk, mv, sq, sk, sv, aq, ak, av, zl = _project(xs, cs, mod_all, w, tabs, layer)
        ya = _dense_attention(mq, mk, mv, upd)
        yb = _window_attention(swa_sink, sq, sk, sv, upd, layer)
        yc = _dense_attention(aq, ak, av, upd)
        yf, yr = _rglru(zl, w, layer)
        if cs is not None and not upd:
            xs, cs = jnp.concatenate([cs, xs], axis=1), None
        xs, cs = _merge(xs, cs, mod_all, w, ya, yb, yc, yf, yr, upd, layer), None
    return xs
```

```python
import functools

import numpy as np
import jax
import jax.numpy as jnp
from jax import lax
from jax.experimental import pallas as pl
from jax.experimental.pallas import tpu as pltpu

GRID_W = 64
N_BRANCH = 4
HEAD_DIM = 64
MLA_HEADS = 4
MLA_NOPE = 64
MLA_ROPE = 32
MLA_V = 64
MLA_QK = MLA_NOPE + MLA_ROPE
SWA_Q_HEADS = 4
SWA_KV_HEADS = 2
WINDOW = 128
AXA_Q_HEADS = 4
AXA_KV_HEADS = 2
LRU_BLOCKS = 4
LRU_C = 8.0
CONV_W = 4
CONV_LEFT = 2
ROPE_THETA = 10000.0
RMS_EPS = 1e-6
NEG_INF = -1e30
LOG2E = 1.4426950408889634

LANES = 128
SUBLANES = 8
TILE = 256
SUB_BATCH = 2
KEY_CHUNKS = 4
VMEM_LIMIT = 56 * 1024 * 1024

F32 = jnp.float32
BF16 = jnp.bfloat16

VEC_NW, VEC_GCQ, VEC_GAINS, VEC_CONVW, VEC_CONVB, VEC_BR, VEC_BI, VEC_LAM, VEC_ROWS = 0, 1, 2, 3, 7, 8, 10, 12, 16
GAIN_CKV, GAIN_MQ, GAIN_MK, GAIN_SQ, GAIN_SK, GAIN_AQ, GAIN_AK = range(7)


def _dot(a, b):
    return jnp.dot(a, b, preferred_element_type=F32)


def _dot_nt(a, b):
    return lax.dot_general(a, b, (((1,), (1,)), ((), ())), preferred_element_type=F32)


def _split(a):
    hi = a.astype(BF16)
    lo = (a - hi.astype(F32)).astype(BF16)
    return hi, lo


def _dot3(a, w_hi, w_lo):
    a_hi, a_lo = _split(a)
    return _dot(a_hi, w_hi) + _dot(a_lo, w_hi) + _dot(a_hi, w_lo)


def _sigmoid(v):
    return 1.0 / (1.0 + jnp.exp2(v * -LOG2E))


def _cparams(sem):
    return pltpu.CompilerParams(dimension_semantics=sem, vmem_limit_bytes=VMEM_LIMIT)


def _layer_spec(arr, layer):
    nd = arr.ndim - 1
    return pl.BlockSpec((1,) + arr.shape[1:], lambda *_: (layer,) + (0,) * nd)


def _gain(vec_ref, slot):
    return vec_ref[0, VEC_GAINS:VEC_GAINS + 1, slot * LANES:(slot + 1) * LANES]


def _mod_kernel(c_ref, w_ref, b_ref, o_ref):
    v = c_ref[...]
    a = v * _sigmoid(v)
    w_hi, w_lo = _split(w_ref[0])
    o_ref[0] = _dot3(a, w_hi, w_lo) + b_ref[0]


def _modulation(cvec, w_mod, b_mod):
    depth, d, d3 = w_mod.shape
    r = cvec.shape[0]
    bn = 512
    return pl.pallas_call(
        _mod_kernel,
        grid=(depth, d3 // bn),
        in_specs=[pl.BlockSpec((r, d), lambda l, j: (0, 0)),
                  pl.BlockSpec((1, d, bn), lambda l, j: (l, 0, j)),
                  pl.BlockSpec((1, 1, bn), lambda l, j: (l, 0, j))],
        out_specs=pl.BlockSpec((1, r, bn), lambda l, j: (l, 0, j)),
        out_shape=jax.ShapeDtypeStruct((depth, r, d3), F32),
        compiler_params=_cparams(("arbitrary", "arbitrary")),
        name="modulation",
    )(cvec, w_mod, b_mod.reshape(depth, 1, d3))


def _modulated_norm(x, mod, vec_ref):
    ms = jnp.mean(x * x, axis=-1, keepdims=True)
    y = x * lax.rsqrt(ms + RMS_EPS) * vec_ref[0, VEC_NW:VEC_NW + 1, :]
    return y * (1.0 + mod[1:2, :]) + mod[0:1, :]


def _stream_tile(x_ref, ctx_ref, is_ctx, j=0):
    if ctx_ref is None:
        return x_ref[j]
    return jnp.where(is_ctx, ctx_ref[j], x_ref[j])


def _lane_rinv(v, n):
    return lax.rsqrt(jnp.sum(v * v, axis=-1, keepdims=True) * (1.0 / n) + RMS_EPS)


def _head_rinv(v, gmat):
    hi, lo = _split(v * v)
    ss = _dot(hi, gmat) + _dot(lo, gmat)
    return lax.rsqrt(ss * (1.0 / HEAD_DIM) + RMS_EPS)


def _proj_kernel(*refs, split_ctx):
    x_ref, ctx_ref = (refs[0], refs[1]) if split_ctx else (refs[0], None)
    (mod_ref, vec_ref, wp_ref, wuq_ref, wuqr_ref, wuk_ref, wuv_ref, gmat_ref,
     cosm_ref, sinm_ref, cosh_ref, sinh_ref,
     mq_o, mk_o, mv_o, sq_o, sk_o, sv_o, aq_o, ak_o, av_o, lru_o) = refs[2 if split_ctx else 1:]
    n_sub = x_ref.shape[0]
    is_ctx = pl.program_id(0) == 0
    def project_in(j):
        x = _stream_tile(x_ref, ctx_ref, is_ctx, j)
        mod = mod_ref[0, pl.ds(jnp.where(is_ctx, 0, j), 1)][0]
        hb = _modulated_norm(x, mod, vec_ref).astype(BF16)
        return _dot(hb, wp_ref[0])

    t = x_ref.shape[1]
    lo = lax.broadcasted_iota(jnp.int32, (t, LANES), 1) < HEAD_DIM
    cosm, sinm = cosm_ref[...], sinm_ref[...]
    cosh, sinh = cosh_ref[...], sinh_ref[...]
    gmat = gmat_ref[...]

    def pair_store(j, out, slab):
        swapped = pltpu.roll(slab, HEAD_DIM, 1)
        out[j, 0] = jnp.where(lo, slab, 0.0).astype(BF16)
        out[j, 1] = jnp.where(lo, 0.0, swapped).astype(BF16)
        out[j, 2] = jnp.where(lo, swapped, 0.0).astype(BF16)
        out[j, 3] = jnp.where(lo, 0.0, slab).astype(BF16)

    def heads(j, z):
        g_cq = vec_ref[0, VEC_GCQ:VEC_GCQ + 1, 0:256]
        cq = (z[:, 0:256] * _lane_rinv(z[:, 0:256], 256.0) * g_cq).astype(BF16)
        q = _dot(cq, wuq_ref[0])
        qr = _dot(cq, wuqr_ref[0])
        ckv = (z[:, 256:384] * _lane_rinv(z[:, 256:384], 128.0) * _gain(vec_ref, GAIN_CKV)).astype(BF16)
        kn = _dot(ckv, wuk_ref[0])
        vv = _dot(ckv, wuv_ref[0])
        kr = z[:, 384:512]
        krr = z[:, 512:640]
        gc_q = _gain(vec_ref, GAIN_MQ) * cosm
        gc_k = _gain(vec_ref, GAIN_MK) * cosm
        for hh in range(MLA_HEADS):
            sl = slice(hh * LANES, (hh + 1) * LANES)
            r = _lane_rinv(q[:, sl], float(MLA_QK))
            mq_o[j, hh] = ((q[:, sl] * r) * gc_q + (qr[:, sl] * r) * sinm).astype(BF16)
            ks = kn[:, sl] + kr
            r = _lane_rinv(ks, float(MLA_QK))
            mk_o[j, hh] = ((ks * r) * gc_k + (krr * r) * sinm).astype(BF16)
        mv_o[j] = vv.T.astype(BF16)

        def gqa(c0, g_q, g_k, q_o, k_o, v_o, v_transposed):
            zq, zqr = z[:, c0:c0 + 256], z[:, c0 + 256:c0 + 512]
            r = _head_rinv(zq, gmat)
            gc = _gain(vec_ref, g_q) * cosh
            for i in range(2):
                sl = slice(i * LANES, (i + 1) * LANES)
                qs = ((zq[:, sl] * r[:, sl]) * gc + (zqr[:, sl] * r[:, sl]) * sinh).astype(BF16)
                if v_transposed:
                    q_o[j, 2 * i] = qs
                    q_o[j, 2 * i + 1] = qs
                else:
                    q_o[j, i] = qs
            zk, zkr = z[:, c0 + 512:c0 + 640], z[:, c0 + 640:c0 + 768]
            r = _head_rinv(zk, gmat[:LANES, :LANES])
            ks = (zk * r) * (_gain(vec_ref, g_k) * cosh) + (zkr * r) * sinh
            pair_store(j, k_o, ks)
            zv = z[:, c0 + 768:c0 + 896]
            if v_transposed:
                zvt = zv.T.astype(BF16)
                v_o[j] = jnp.concatenate([zvt[:HEAD_DIM], zvt[:HEAD_DIM], zvt[HEAD_DIM:], zvt[HEAD_DIM:]], axis=0)
            else:
                pair_store(j, v_o, zv)

        gqa(640, GAIN_SQ, GAIN_SK, sq_o, sk_o, sv_o, False)
        gqa(1536, GAIN_AQ, GAIN_AK, aq_o, ak_o, av_o, True)
        lru_o[j] = z[:, 2432:2688]

    for j in range(n_sub):
        heads(j, project_in(j))


def _project(x, ctx, mod_all, w, tabs, layer):
    split_ctx = ctx is not None
    b, d = x.shape[0], x.shape[2]
    s = x.shape[1] + (ctx.shape[1] if split_ctx else 0)
    nt = s // TILE
    ctx_row = b

    assert b % SUB_BATCH == 0 and ctx_row % SUB_BATCH == 0
    nbb = b // SUB_BATCH

    def tile_spec(width):
        return pl.BlockSpec((SUB_BATCH, TILE, width), lambda t, i: (i, t, 0))

    def head_spec(nh):
        return pl.BlockSpec((SUB_BATCH, nh, TILE, LANES), lambda t, i: (i, 0, t, 0))

    def head_spec_t(nh):
        return pl.BlockSpec((SUB_BATCH, nh * HEAD_DIM, TILE), lambda t, i: (i, 0, t))

    if split_ctx:
        streams = [x, ctx]
        stream_specs = [pl.BlockSpec((SUB_BATCH, TILE, d), lambda t, i: (i, jnp.maximum(t - 1, 0), 0)),
                        pl.BlockSpec((SUB_BATCH, TILE, d), lambda t, i: (jnp.where(t == 0, i, nbb - 1), 0, 0))]
    else:
        streams, stream_specs = [x], [tile_spec(d)]
    tab_spec = pl.BlockSpec((TILE, LANES), lambda t, i: (t, 0))
    consts = [w["vecs"], w["wp"], w["wuq"], w["wuqr"], w["wuk"], w["wuv"]]
    in_specs = (stream_specs
                + [pl.BlockSpec((1, SUB_BATCH, 3, d),
                                lambda t, i: (layer, jnp.where(t == 0, ctx_row // SUB_BATCH, i), 0, 0))]
                + [_layer_spec(c, layer) for c in consts]
                + [pl.BlockSpec(w["gmat"].shape, lambda t, i: (0, 0))] + [tab_spec] * 4)
    n_slabs = [MLA_HEADS, MLA_HEADS, MLA_HEADS, 2, 4, 4, 4, 4, 4]
    transposed = [False, False, True, False, False, False, False, False, True]
    out_specs = ([head_spec_t(nh) if tr else head_spec(nh) for nh, tr in zip(n_slabs, transposed)]
                 + [tile_spec(4 * HEAD_DIM)])
    out_shape = ([jax.ShapeDtypeStruct((b, nh * HEAD_DIM, s) if tr else (b, nh, s, LANES), BF16)
                  for nh, tr in zip(n_slabs, transposed)]
                 + [jax.ShapeDtypeStruct((b, s, 4 * HEAD_DIM), F32)])
    return pl.pallas_call(
        functools.partial(_proj_kernel, split_ctx=split_ctx),
        grid=(nt, nbb),
        in_specs=in_specs,
        out_specs=out_specs,
        out_shape=out_shape,
        compiler_params=_cparams(("arbitrary", "arbitrary")),
        name="project",
    )(*streams, mod_all, *consts, w["gmat"], *tabs)


def _dense_attn_kernel(q_ref, k_ref, v_ref, o_ref, *, q_share, ctx_tile_first):
    s_len = k_ref.shape[2]
    n_heads = k_ref.shape[1]
    v_share = n_heads * HEAD_DIM // v_ref.shape[1]

    def attend(n_keys):
        n_groups = n_keys // LANES
        n_chunks = min(KEY_CHUNKS, n_groups)
        edges = [(c * n_groups // n_chunks) * LANES for c in range(n_chunks + 1)]
        chunks = [slice(a, b) for a, b in zip(edges[:-1], edges[1:])]

        def scores(unit):
            j, hh = unit
            q = q_ref[j, hh // q_share]
            return [_dot_nt(k_ref[j, hh, sl, :], q) for sl in chunks]

        def head_out(unit, st):
            j, hh = unit
            m = functools.reduce(jnp.maximum, [jnp.max(sc, axis=0, keepdims=True) for sc in st])
            pt = [jnp.exp2(sc - m) for sc in st]
            l = sum(jnp.sum(pc, axis=0, keepdims=True) for pc in pt)
            band = slice(hh // v_share * HEAD_DIM, (hh // v_share + 1) * HEAD_DIM)
            ot = sum(_dot(v_ref[j, band, sl], pc.astype(BF16)) for sl, pc in zip(chunks, pt))
            return ot / l

        units = [(j, hh) for j in range(q_ref.shape[0]) for hh in range(n_heads)]
        st_next = scores(units[0])
        outs = []
        for u, unit in enumerate(units):
            st = st_next
            if u + 1 < len(units):
                st_next = scores(units[u + 1])
            outs.append(head_out(unit, st))
        for j in range(q_ref.shape[0]):
            o_ref[j] = jnp.concatenate(outs[j * n_heads:(j + 1) * n_heads], axis=0).T

    if ctx_tile_first:
        pl.when(pl.program_id(1) == 0)(lambda: attend(TILE))
        pl.when(pl.program_id(1) > 0)(lambda: attend(s_len))
    else:
        attend(s_len)


def _dense_attention(q, k, v, with_ctx_queries):
    b, nq_slabs, s, _ = q.shape
    nh = k.shape[1]
    nt = s // TILE
    off = 0 if with_ctx_queries else 1
    nq = nt - off
    assert b % SUB_BATCH == 0
    kern = functools.partial(_dense_attn_kernel, q_share=nh // nq_slabs,
                             ctx_tile_first=with_ctx_queries)
    return pl.pallas_call(
        kern,
        grid=(b // SUB_BATCH, nq),
        in_specs=[pl.BlockSpec((SUB_BATCH, nq_slabs, TILE, LANES), lambda i, t: (i, 0, t + off, 0)),
                  pl.BlockSpec((SUB_BATCH, nh, s, LANES), lambda i, t: (i, 0, 0, 0)),
                  pl.BlockSpec((SUB_BATCH,) + v.shape[1:], lambda i, t: (i, 0, 0))],
        out_specs=pl.BlockSpec((SUB_BATCH, TILE, 2 * LANES), lambda i, t: (i, t, 0)),
        out_shape=jax.ShapeDtypeStruct((b, nq * TILE, 2 * LANES), F32),
        compiler_params=_cparams(("arbitrary", "arbitrary")),
        name="dense_attention",
    )(q, k, v)


def _window_attn_kernel(sink_ref, q_ref, k_ref, v_ref, o_ref, *, tile_off, layer):
    s_len = k_ref.shape[2]
    n_tiles = s_len // TILE
    ti = pl.program_id(1) + tile_off
    is_lat = ti > 0
    t0 = ti * TILE
    half = TILE // 2
    left0 = pl.multiple_of(jnp.maximum(t0 - half, 0), half)
    cen0 = pl.multiple_of(t0, TILE)
    right0 = pl.multiple_of(jnp.minimum(t0 + TILE, s_len - half), half)
    r_c = lax.broadcasted_iota(jnp.int32, (TILE, TILE), 0)
    c_c = lax.broadcasted_iota(jnp.int32, (TILE, TILE), 1)
    r_s = lax.broadcasted_iota(jnp.int32, (TILE, half), 0)
    c_s = lax.broadcasted_iota(jnp.int32, (TILE, half), 1)
    ok_cen = (jnp.abs(r_c - c_c) <= WINDOW) & is_lat
    ok_left = ((r_s - c_s + half) <= WINDOW) & (ti > 1)
    ok_right = ((c_s - r_s + TILE) <= WINDOW) & is_lat & (ti < n_tiles - 1)
    def lane_groups(a):
        return [a[:, g * LANES:(g + 1) * LANES] for g in range(a.shape[1] // LANES)]

    def scores(unit):
        j, hh = unit
        q = q_ref[j, hh // 2]
        k = lambda start, size: k_ref[j, hh, pl.ds(start, size), :]
        return (_dot_nt(q, k(0, TILE)),
                jnp.where(ok_left, _dot_nt(q, k(left0, half)), NEG_INF),
                jnp.where(ok_cen, _dot_nt(q, k(cen0, TILE)), NEG_INF),
                jnp.where(ok_right, _dot_nt(q, k(right0, half)), NEG_INF))

    def head_out(unit, sc):
        j, hh = unit
        sink = sink_ref[layer, hh] * LOG2E
        m = functools.reduce(jnp.maximum, [g for piece in sc for g in lane_groups(piece)])
        m = jnp.maximum(jnp.max(m, -1, keepdims=True), sink)
        ps = [jnp.exp2(piece - m) for piece in sc]
        den = sum(g for piece in ps for g in lane_groups(piece))
        den = jnp.sum(den, -1, keepdims=True) + jnp.exp2(sink - m)
        v = lambda start, size: v_ref[j, hh, pl.ds(start, size), :]
        vs = (v(0, TILE), v(left0, half), v(cen0, TILE), v(right0, half))
        num = sum(_dot(p.astype(BF16), vv) for p, vv in zip(ps, vs))
        return num / den

    n_heads = k_ref.shape[1]
    units = [(j, hh) for j in range(q_ref.shape[0]) for hh in range(n_heads)]
    sc_next = scores(units[0])
    outs = []
    for u, unit in enumerate(units):
        sc = sc_next
        if u + 1 < len(units):
            sc_next = scores(units[u + 1])
        outs.append(head_out(unit, sc))
    for j in range(q_ref.shape[0]):
        o = outs[j * n_heads:(j + 1) * n_heads]
        o_ref[j] = jnp.concatenate([o[0] + o[1], o[2] + o[3]], axis=-1)


def _window_attention(sink, q, k, v, with_ctx_queries, layer):
    b, nq_slabs, s, _ = q.shape
    nh = k.shape[1]
    nt = s // TILE
    off = 0 if with_ctx_queries else 1
    nq = nt - off
    assert b % SUB_BATCH == 0
    kern = functools.partial(_window_attn_kernel, tile_off=off, layer=layer)
    return pl.pallas_call(
        kern,
        grid=(b // SUB_BATCH, nq),
        in_specs=[pl.BlockSpec(memory_space=pltpu.SMEM),
                  pl.BlockSpec((SUB_BATCH, nq_slabs, TILE, LANES), lambda i, t: (i, 0, t + off, 0)),
                  pl.BlockSpec((SUB_BATCH, nh, s, LANES), lambda i, t: (i, 0, 0, 0)),
                  pl.BlockSpec((SUB_BATCH, nh, s, LANES), lambda i, t: (i, 0, 0, 0))],
        out_specs=pl.BlockSpec((SUB_BATCH, TILE, 2 * LANES), lambda i, t: (i, t, 0)),
        out_shape=jax.ShapeDtypeStruct((b, nq * TILE, 2 * LANES), F32),
        compiler_params=_cparams(("arbitrary", "arbitrary")),
        name="window_attention",
    )(sink, q, k, v)


def _lru_conv(j, prev_ref, cur_ref, next_ref, prev_ok, next_ok, vec_ref):
    t = cur_ref.shape[1]
    prev = jnp.where(prev_ok, prev_ref[j], 0.0)
    nxt = jnp.where(next_ok, next_ref[j], 0.0)
    ext = jnp.concatenate([prev, cur_ref[j], nxt], axis=0)
    n = t + 2 * SUBLANES
    c = cur_ref.shape[2]

    def tap(j):
        return vec_ref[0, VEC_CONVW + j:VEC_CONVW + j + 1, 0:c]

    u = vec_ref[0, VEC_CONVB:VEC_CONVB + 1, 0:c] + cur_ref[j] * tap(CONV_LEFT)
    for j in range(CONV_W):
        off = j - CONV_LEFT
        if off == 0:
            continue
        sh = pltpu.roll(ext, (-off) % n, 0)[SUBLANES:SUBLANES + t]
        u = u + sh * tap(j)
    return u


def _lru_scan_tile(u, d, wri_ref, vec_ref, h0, reverse):
    t, c = u.shape

    def vec(row):
        return vec_ref[0, row + d:row + d + 1, 0:c]

    gates = _dot(u.astype(BF16), wri_ref[0, d])
    r = _sigmoid(gates[:, :c] + vec(VEC_BR))
    i = _sigmoid(gates[:, c:] + vec(VEC_BI))
    lam = vec(VEC_LAM)
    softplus_neg = jnp.maximum(-lam, 0.0) + jnp.log1p(jnp.exp(-jnp.abs(lam)))
    a = jnp.exp2(r * (softplus_neg * (-LRU_C * LOG2E)))
    bx = jnp.sqrt(1.0 - a * a) * (i * u)
    row = lax.broadcasted_iota(jnp.int32, u.shape, 0)
    step = 1
    while step < t:
        if reverse:
            ok = row < t - step
            a_sh = jnp.where(ok, pltpu.roll(a, t - step, 0), 1.0)
            b_sh = jnp.where(ok, pltpu.roll(bx, t - step, 0), 0.0)
        else:
            ok = row >= step
            a_sh = jnp.where(ok, pltpu.roll(a, step, 0), 1.0)
            b_sh = jnp.where(ok, pltpu.roll(bx, step, 0), 0.0)
        bx = a * b_sh + bx
        a = a * a_sh
        step *= 2
    h = a * h0 + bx
    h_end = h[0:1, :] if reverse else h[t - 1:t, :]
    return h, h_end


def _lru_kernel(fp_ref, fc_ref, fn_ref, rp_ref, rc_ref, rn_ref, vec_ref, wri_ref,
                yf_ref, yb_ref, hf_s, hb_s):
    j = pl.program_id(1)
    nt = pl.num_programs(1)

    @pl.when(j == 0)
    def _():
        hf_s[...] = jnp.zeros_like(hf_s)
        hb_s[...] = jnp.zeros_like(hb_s)

    def bounds(ti):
        return ti >= 2, (ti >= 1) & (ti < nt - 1)

    tf = j
    tr = jnp.where(j == 0, 0, nt - j)
    for e in range(fc_ref.shape[0]):
        p_ok, n_ok = bounds(tf)
        uf = _lru_conv(e, fp_ref, fc_ref, fn_ref, p_ok, n_ok, vec_ref)
        h, h_end = _lru_scan_tile(uf, 0, wri_ref, vec_ref, hf_s[e], False)
        yf_ref[e] = h
        hf_s[e] = h_end
        p_ok, n_ok = bounds(tr)
        ur = _lru_conv(e, rp_ref, rc_ref, rn_ref, p_ok, n_ok, vec_ref)
        h, h_end = _lru_scan_tile(ur, 1, wri_ref, vec_ref, hb_s[e], True)
        yb_ref[e] = h
        hb_s[e] = h_end


def _rglru(z, w, layer):
    b, s, c = z.shape
    nt = s // TILE
    r8 = TILE // SUBLANES
    n8 = s // SUBLANES

    def rev(j):
        return jnp.where(j == 0, 0, nt - j)

    assert b % SUB_BATCH == 0

    def cur(f):
        return pl.BlockSpec((SUB_BATCH, TILE, c), lambda i, j: (i, f(j), 0))

    def prev(f):
        return pl.BlockSpec((SUB_BATCH, SUBLANES, c), lambda i, j: (i, jnp.maximum(f(j) * r8 - 1, 0), 0))

    def nxt(f):
        return pl.BlockSpec((SUB_BATCH, SUBLANES, c), lambda i, j: (i, jnp.minimum((f(j) + 1) * r8, n8 - 1), 0))

    ident = lambda j: j
    consts = [w["vecs"], w["wri"]]
    return pl.pallas_call(
        _lru_kernel,
        grid=(b // SUB_BATCH, nt),
        in_specs=[prev(ident), cur(ident), nxt(ident), prev(rev), cur(rev), nxt(rev)]
                 + [_layer_spec(x, layer) for x in consts],
        out_specs=[cur(ident), cur(rev)],
        out_shape=[jax.ShapeDtypeStruct((b, s, c), F32)] * 2,
        scratch_shapes=[pltpu.VMEM((SUB_BATCH, 1, c), F32), pltpu.VMEM((SUB_BATCH, 1, c), F32)],
        compiler_params=_cparams(("arbitrary", "arbitrary")),
        name="rglru",
    )(z, z, z, z, z, z, *consts)


def _merge_kernel(*refs, split_ctx, ctx_tile_first):
    x_ref, ctx_ref = (refs[0], refs[1]) if split_ctx else (refs[0], None)
    (mod_ref, vec_ref, wg_ref, wm_ref, ya_ref, yb_ref, yc_ref, yf_ref, yr_ref,
     wb_ref, wo_ref, o_ref) = refs[2 if split_ctx else 1:]
    is_ctx = (pl.program_id(1) == 0) if ctx_tile_first else False
    d = x_ref.shape[2]
    bw = ya_ref.shape[2]
    for j in range(x_ref.shape[0]):
        x = _stream_tile(x_ref, ctx_ref, is_ctx, j)
        mod = mod_ref[0, pl.ds(jnp.where(is_ctx, 0, j), 1)][0]
        hb = _modulated_norm(x, mod, vec_ref).astype(BF16)
        zg = _dot(hb, wg_ref[0])
        branches = (ya_ref[j], yb_ref[j], yc_ref[j], yf_ref[j] + yr_ref[j])
        mix = jnp.zeros_like(x)
        for k in range(N_BRANCH):
            g = zg[:, k * bw:(k + 1) * bw]
            y = branches[k] * (g * _sigmoid(g))
            proj = _dot(y.astype(BF16), wb_ref[0, k])
            zm = _dot(hb, wm_ref[0, :, k * d:(k + 1) * d])
            mix = mix + _sigmoid(zm) * proj
        o_ref[j] = x + mod[2:3, :] * _dot(mix.astype(BF16), wo_ref[0])


def _merge(x, ctx, mod_all, w, ya, yb, yc, yf, yr, with_ctx, layer):
    split_ctx = ctx is not None
    assert with_ctx or not split_ctx
    b, d = x.shape[0], x.shape[2]
    s = x.shape[1] + (ctx.shape[1] if split_ctx else 0)
    nt = s // TILE
    off = 0 if with_ctx else 1
    nq = nt - off
    ctx_row = b
    bw = ya.shape[2]
    assert b % SUB_BATCH == 0 and ctx_row % SUB_BATCH == 0
    nbb = b // SUB_BATCH

    def stream(width):
        return pl.BlockSpec((SUB_BATCH, TILE, width), lambda i, t: (i, t + off, 0))

    def local(width):
        return pl.BlockSpec((SUB_BATCH, TILE, width), lambda i, t: (i, t, 0))

    if split_ctx:
        streams = [x, ctx]
        stream_specs = [pl.BlockSpec((SUB_BATCH, TILE, d), lambda i, t: (i, jnp.maximum(t - 1, 0), 0)),
                        pl.BlockSpec((SUB_BATCH, TILE, d), lambda i, t: (i, 0, 0))]
    else:
        streams, stream_specs = [x], [stream(d)]
    consts_a = [w["vecs"], w["wg"], w["wm"]]
    consts_b = [w["wb"], w["wo"]]
    return pl.pallas_call(
        functools.partial(_merge_kernel, split_ctx=split_ctx, ctx_tile_first=with_ctx),
        grid=(nbb, nq),
        in_specs=stream_specs
                 + [pl.BlockSpec((1, SUB_BATCH, 3, d),
                                 lambda i, t: (layer, jnp.where(t + off == 0, ctx_row // SUB_BATCH, i), 0, 0))]
                 + [_layer_spec(c, layer) for c in consts_a]
                 + [local(bw), local(bw), local(bw), stream(bw), stream(bw)]
                 + [_layer_spec(c, layer) for c in consts_b],
        out_specs=local(d),
        out_shape=jax.ShapeDtypeStruct((b, nq * TILE, d), F32),
        compiler_params=_cparams(("arbitrary", "arbitrary")),
        name="merge",
    )(*streams, mod_all, *consts_a, ya, yb, yc, yf, yr, *consts_b)


def _rope_tables(n, n_ctx):
    def axial(rot_dim):
        n_rows = n // GRID_W
        rows = np.repeat(np.arange(n_rows, dtype=np.float64), GRID_W)
        cols = np.tile(np.arange(GRID_W, dtype=np.float64), n_rows)
        quarter = rot_dim // 4
        freqs = ROPE_THETA ** (-np.arange(quarter, dtype=np.float64) / quarter)
        ang = np.concatenate([rows[:, None] * freqs, cols[:, None] * freqs], axis=-1)
        return np.cos(ang), np.sin(ang)

    def with_ctx(cos, sin):
        return (jnp.asarray(np.concatenate([np.ones((n_ctx, LANES)), cos], axis=0), F32),
                jnp.asarray(np.concatenate([np.zeros((n_ctx, LANES)), sin], axis=0), F32))

    cm, sm = axial(MLA_ROPE)
    one, zero = np.ones((n, MLA_NOPE)), np.zeros((n, MLA_NOPE))
    pad1, pad0 = np.ones((n, LANES - MLA_QK)), np.zeros((n, LANES - MLA_QK))
    cosm, sinm = with_ctx(np.concatenate([one, cm, cm, pad1], axis=-1),
                          np.concatenate([zero, sm, sm, pad0], axis=-1))
    ch, sh = axial(HEAD_DIM)
    cosh, sinh = with_ctx(np.tile(ch, (1, 4)), np.tile(sh, (1, 4)))
    return cosm, sinm, cosh, sinh


def _place(w, mat, gain=None):
    wg = w if gain is None else w * gain[:, None, :]
    return jnp.einsum("lkc,cp->lkp", wg, jnp.asarray(mat, F32), precision=lax.Precision.HIGHEST)


def _copy_matrix(n_heads, in_width, in_start, count, out_width, out_start):
    m = np.zeros((n_heads * in_width, n_heads * out_width), np.float32)
    for h in range(n_heads):
        for j in range(count):
            m[h * in_width + in_start + j, h * out_width + out_start + j] = 1.0
    return m


def _partner_matrix(n_heads, in_width, in_start, half, out_width, out_start):
    m = np.zeros((n_heads * in_width, n_heads * out_width), np.float32)
    for h in range(n_heads):
        for j in range(half):
            x1, x2 = h * in_width + in_start + j, h * in_width + in_start + half + j
            o1, o2 = h * out_width + out_start + j, h * out_width + out_start + half + j
            m[x2, o1] = -1.0
            m[x1, o2] = 1.0
    return m


IN_SIZES = (256, 128, 32, 256, 128, 128, 256, 128, 128, 256, 1024)
PREP_ROWS = 256


def _prep_kernel(wt_ref, vec_ref, pc_ref, pm_ref, ph_ref, wp_o, wg_o, wm_o):
    offs = [0]
    for sz in IN_SIZES:
        offs.append(offs[-1] + sz)

    def rows(a, b):
        return wt_ref[0, a:b, :].T

    def piece(i):
        return rows(offs[i], offs[i + 1])

    def put(start, val):
        wp_o[0, :, start:start + val.shape[1]] = val.astype(BF16)

    def partner(val, gain, mat):
        return _dot((val * gain).astype(BF16), mat)

    put(0, piece(0))
    put(256, piece(1))
    z = rows(offs[2], offs[2] + LANES)
    hi = z.astype(BF16)
    r1 = z - hi.astype(F32)
    mid = r1.astype(BF16)
    lo = (r1 - mid.astype(F32)).astype(BF16)
    pc = pc_ref[...]
    kr_slab = _dot(hi, pc) + _dot(mid, pc) + _dot(lo, pc)
    put(384, kr_slab)
    put(512, partner(kr_slab, _gain(vec_ref, GAIN_MK), pm_ref[...]))
    ph = ph_ref[...]
    col = 640
    for first, q_slot, k_slot in ((3, GAIN_SQ, GAIN_SK), (6, GAIN_AQ, GAIN_AK)):
        q, k, v = piece(first), piece(first + 1), piece(first + 2)
        g_q = _gain(vec_ref, q_slot)
        put(col, q)
        put(col + 256, partner(q, jnp.concatenate([g_q, g_q], axis=1), ph))
        put(col + 512, k)
        put(col + 640, partner(k, _gain(vec_ref, k_slot), ph[:LANES, :LANES]))
        put(col + 768, v)
        col += 896
    put(col, piece(9))
    wg_o[0] = piece(10).astype(BF16)
    for c in range(offs[11], wt_ref.shape[1], 4 * LANES):
        wm_o[0, :, c - offs[11]:c - offs[11] + 4 * LANES] = rows(c, c + 4 * LANES).astype(BF16)


def _prep_in_weights(w_in, vecs):
    depth, d, cols = w_in.shape
    n_merge = cols - sum(IN_SIZES)
    half = HEAD_DIM // 2
    mats = [_copy_matrix(1, LANES, 0, MLA_ROPE, LANES, MLA_NOPE),
            _partner_matrix(1, LANES, MLA_NOPE, MLA_ROPE // 2, LANES, MLA_NOPE),
            _partner_matrix(4, HEAD_DIM, 0, half, HEAD_DIM, 0)]
    mats = [jnp.asarray(m, BF16) for m in mats]
    wp_cols = 2688

    def out_rows(width):
        return pl.BlockSpec((1, PREP_ROWS, width), lambda l, r: (l, r, 0))

    return pl.pallas_call(
        _prep_kernel,
        grid=(depth, d // PREP_ROWS),
        in_specs=[pl.BlockSpec((1, cols, PREP_ROWS), lambda l, r: (l, 0, r)),
                  pl.BlockSpec((1,) + vecs.shape[1:], lambda l, r: (l, 0, 0))]
                 + [pl.BlockSpec(m.shape, lambda l, r: (0, 0)) for m in mats],
        out_specs=[out_rows(wp_cols), out_rows(IN_SIZES[10]), out_rows(n_merge)],
        out_shape=[jax.ShapeDtypeStruct((depth, d, wp_cols), BF16),
                   jax.ShapeDtypeStruct((depth, d, IN_SIZES[10]), BF16),
                   jax.ShapeDtypeStruct((depth, d, n_merge), BF16)],
        compiler_params=_cparams(("arbitrary", "arbitrary")),
        name="prep_weights",
    )(jnp.transpose(w_in, (0, 2, 1)), vecs, *mats)


def _prepare_weights(p):
    depth, d, _ = p["w_in"].shape
    q_scale = HEAD_DIM ** -0.5 * LOG2E
    mq_scale = MLA_QK ** -0.5 * LOG2E
    g_sq = jnp.tile(p["swa_q_norm"] * q_scale, (1, 2))
    g_sk = jnp.tile(p["swa_k_norm"], (1, 2))
    g_aq = jnp.tile(p["axa_q_norm"] * q_scale, (1, 2))
    g_ak = jnp.tile(p["axa_k_norm"], (1, 2))
    g_mq = p["mla_q_norm"] * mq_scale
    g_mk = p["mla_k_norm"]

    w_uq, w_ukv = p["mla_w_uq"], p["mla_w_ukv"]
    kv_width = MLA_NOPE + MLA_V
    wuq = _place(w_uq, _copy_matrix(MLA_HEADS, MLA_QK, 0, MLA_QK, LANES, 0)).astype(BF16)
    wuqr = _place(w_uq, _partner_matrix(MLA_HEADS, MLA_QK, MLA_NOPE, MLA_ROPE // 2, LANES, MLA_NOPE),
                  jnp.tile(g_mq, (1, MLA_HEADS))).astype(BF16)
    wuk = _place(w_ukv, _copy_matrix(MLA_HEADS, kv_width, 0, MLA_NOPE, LANES, 0)).astype(BF16)
    wuv = _place(w_ukv, _copy_matrix(MLA_HEADS, kv_width, MLA_NOPE, MLA_V, MLA_V, 0)).astype(BF16)

    def row(v):
        v = v if v.ndim == 3 else v[:, None, :]
        return jnp.pad(v, ((0, 0), (0, 0), (0, d - v.shape[2])))

    slab = lambda g: jnp.pad(g, ((0, 0), (0, LANES - g.shape[1])))
    gains = jnp.concatenate([p["mla_ckv_norm"], slab(g_mq), slab(g_mk), g_sq, g_sk, g_aq, g_ak], axis=1)
    vec_rows = [row(p["norm_w"]), row(p["mla_cq_norm"]), row(gains), row(p["lru_conv_w"]),
                row(p["lru_conv_b"]), row(p["lru_b_r"]), row(p["lru_b_i"]), row(p["lru_lambda"])]
    n_rows = sum(r.shape[1] for r in vec_rows)
    vecs = jnp.concatenate(vec_rows + [jnp.zeros((depth, VEC_ROWS - n_rows, d), F32)], axis=1)
    wp, wg, wm = _prep_in_weights(p["w_in"], vecs)

    idx = np.arange(4 * HEAD_DIM) // HEAD_DIM
    gmat = jnp.asarray(idx[:, None] == idx[None, :], BF16)

    def block_diag(wb):
        l, two, k, m, _ = wb.shape
        return jnp.einsum("ldkij,kn->ldkinj", wb, jnp.eye(k, dtype=wb.dtype)).reshape(l, two, k * m, k * m)

    wri = jnp.concatenate([block_diag(p["lru_w_r"]), block_diag(p["lru_w_i"])], axis=-1).astype(BF16)
    return {
        "vecs": vecs, "gmat": gmat,
        "wp": wp, "wuq": wuq, "wuqr": wuqr, "wuk": wuk, "wuv": wuv,
        "wg": wg, "wm": wm,
        "wb": p["w_branch"].astype(BF16), "wo": p["w_out"].astype(BF16),
        "wri": wri,
    }


def kernel(x, c, ctx, c_ctx, w_mod, b_mod, norm_w, w_in, mla_cq_norm, mla_ckv_norm, mla_w_uq, mla_w_ukv, mla_q_norm, mla_k_norm, swa_q_norm, swa_k_norm, swa_sink, axa_q_norm, axa_k_norm, lru_conv_w, lru_conv_b, lru_w_r, lru_b_r, lru_w_i, lru_b_i, lru_lambda, w_branch, w_out):
    p = dict(norm_w=norm_w, w_in=w_in, mla_cq_norm=mla_cq_norm, mla_ckv_norm=mla_ckv_norm,
             mla_w_uq=mla_w_uq, mla_w_ukv=mla_w_ukv, mla_q_norm=mla_q_norm, mla_k_norm=mla_k_norm,
             swa_q_norm=swa_q_norm, swa_k_norm=swa_k_norm,
             axa_q_norm=axa_q_norm, axa_k_norm=axa_k_norm, lru_conv_w=lru_conv_w,
             lru_conv_b=lru_conv_b, lru_w_r=lru_w_r, lru_b_r=lru_b_r, lru_w_i=lru_w_i,
             lru_b_i=lru_b_i, lru_lambda=lru_lambda, w_branch=w_branch, w_out=w_out)
    b, n, d = x.shape
    n_ctx = ctx.shape[1]
    depth = w_mod.shape[0]
    assert n_ctx == TILE and n % TILE == 0 and n % GRID_W == 0 and b + 1 <= SUBLANES

    rows = jnp.concatenate([c, c_ctx[None, :], jnp.zeros((SUBLANES - b - 1, d), F32)], axis=0)
    mod_all = _modulation(rows, w_mod, b_mod).reshape(depth, SUBLANES, 3, d)

    w = _prepare_weights(p)
    tabs = _rope_tables(n, n_ctx)
    xs, cs = x, ctx
    for layer in range(depth):
        upd = layer < depth - 1
        mq, mk, mv, sq, sk, sv, aq, ak, av, zl = _project(xs, cs, mod_all, w, tabs, layer)
        ya = _dense_attention(mq, mk, mv, upd)
        yb = _window_attention(swa_sink, sq, sk, sv, upd, layer)
        yc = _dense_attention(aq, ak, av, upd)
        yf, yr = _rglru(zl, w, layer)
        if cs is not None and not upd:
            xs, cs = jnp.concatenate([cs, xs], axis=1), None
        xs, cs = _merge(xs, cs, mod_all, w, ya, yb, yc, yf, yr, upd, layer), None
    return xs
```

```python
import functools

import numpy as np
import jax
import jax.numpy as jnp
from jax import lax
from jax.experimental import pallas as pl
from jax.experimental.pallas import tpu as pltpu

GRID_W = 64
N_BRANCH = 4
HEAD_DIM = 64
MLA_HEADS = 4
MLA_NOPE = 64
MLA_ROPE = 32
MLA_V = 64
MLA_QK = MLA_NOPE + MLA_ROPE
SWA_Q_HEADS = 4
SWA_KV_HEADS = 2
WINDOW = 128
AXA_Q_HEADS = 4
AXA_KV_HEADS = 2
LRU_BLOCKS = 4
LRU_C = 8.0
CONV_W = 4
CONV_LEFT = 2
ROPE_THETA = 10000.0
RMS_EPS = 1e-6
NEG_INF = -1e30
LOG2E = 1.4426950408889634

LANES = 128
SUBLANES = 8
TILE = 256
SUB_BATCH = 2
KEY_CHUNKS = 4
VMEM_LIMIT = 56 * 1024 * 1024

F32 = jnp.float32
BF16 = jnp.bfloat16

VEC_NW, VEC_GCQ, VEC_GAINS, VEC_CONVW, VEC_CONVB, VEC_BR, VEC_BI, VEC_LAM, VEC_ROWS = 0, 1, 2, 3, 7, 8, 10, 12, 16
GAIN_CKV, GAIN_MQ, GAIN_MK, GAIN_SQ, GAIN_SK, GAIN_AQ, GAIN_AK = range(7)


def _dot(a, b):
    return jnp.dot(a, b, preferred_element_type=F32)


def _dot_nt(a, b):
    return lax.dot_general(a, b, (((1,), (1,)), ((), ())), preferred_element_type=F32)


def _split(a):
    hi = a.astype(BF16)
    lo = (a - hi.astype(F32)).astype(BF16)
    return hi, lo


def _dot3(a, w_hi, w_lo):
    a_hi, a_lo = _split(a)
    return _dot(a_hi, w_hi) + _dot(a_lo, w_hi) + _dot(a_hi, w_lo)


def _sigmoid(v):
    return 1.0 / (1.0 + jnp.exp2(v * -LOG2E))


def _cparams(sem):
    return pltpu.CompilerParams(dimension_semantics=sem, vmem_limit_bytes=VMEM_LIMIT)


def _layer_spec(arr, layer):
    nd = arr.ndim - 1
    return pl.BlockSpec((1,) + arr.shape[1:], lambda *_: (layer,) + (0,) * nd)


def _gain(vec_ref, slot):
    return vec_ref[0, VEC_GAINS:VEC_GAINS + 1, slot * LANES:(slot + 1) * LANES]


def _mod_kernel(c_ref, w_ref, b_ref, o_ref):
    v = c_ref[...]
    a = v * _sigmoid(v)
    w_hi, w_lo = _split(w_ref[0])
    o_ref[0] = _dot3(a, w_hi, w_lo) + b_ref[0]


def _modulation(cvec, w_mod, b_mod):
    depth, d, d3 = w_mod.shape
    r = cvec.shape[0]
    bn = 512
    return pl.pallas_call(
        _mod_kernel,
        grid=(depth, d3 // bn),
        in_specs=[pl.BlockSpec((r, d), lambda l, j: (0, 0)),
                  pl.BlockSpec((1, d, bn), lambda l, j: (l, 0, j)),
                  pl.BlockSpec((1, 1, bn), lambda l, j: (l, 0, j))],
        out_specs=pl.BlockSpec((1, r, bn), lambda l, j: (l, 0, j)),
        out_shape=jax.ShapeDtypeStruct((depth, r, d3), F32),
        compiler_params=_cparams(("arbitrary", "arbitrary")),
        name="modulation",
    )(cvec, w_mod, b_mod.reshape(depth, 1, d3))


def _modulated_norm(x, mod, vec_ref):
    ms = jnp.mean(x * x, axis=-1, keepdims=True)
    y = x * lax.rsqrt(ms + RMS_EPS) * vec_ref[0, VEC_NW:VEC_NW + 1, :]
    return y * (1.0 + mod[1:2, :]) + mod[0:1, :]


def _stream_tile(x_ref, ctx_ref, is_ctx, j=0):
    if ctx_ref is None:
        return x_ref[j]
    return jnp.where(is_ctx, ctx_ref[j], x_ref[j])


def _lane_rinv(v, n):
    return lax.rsqrt(jnp.sum(v * v, axis=-1, keepdims=True) * (1.0 / n) + RMS_EPS)


def _head_rinv(v, gmat):
    hi, lo = _split(v * v)
    ss = _dot(hi, gmat) + _dot(lo, gmat)
    return lax.rsqrt(ss * (1.0 / HEAD_DIM) + RMS_EPS)


def _proj_kernel(*refs, split_ctx):
    x_ref, ctx_ref = (refs[0], refs[1]) if split_ctx else (refs[0], None)
    (mod_ref, vec_ref, wp_ref, wuq_ref, wuqr_ref, wuk_ref, wuv_ref, gmat_ref,
     cosm_ref, sinm_ref, cosh_ref, sinh_ref,
     mq_o, mk_o, mv_o, sq_o, sk_o, sv_o, aq_o, ak_o, av_o, lru_o) = refs[2 if split_ctx else 1:]
    n_sub = x_ref.shape[0]
    is_ctx = pl.program_id(0) == 0
    def project_in(j):
        x = _stream_tile(x_ref, ctx_ref, is_ctx, j)
        mod = mod_ref[0, pl.ds(jnp.where(is_ctx, 0, j), 1)][0]
        hb = _modulated_norm(x, mod, vec_ref).astype(BF16)
        return _dot(hb, wp_ref[0])

    t = x_ref.shape[1]
    lo = lax.broadcasted_iota(jnp.int32, (t, LANES), 1) < HEAD_DIM
    cosm, sinm = cosm_ref[...], sinm_ref[...]
    gmat = gmat_ref[...]
    first_half = (lax.broadcasted_iota(jnp.int32, (t, LANES), 1) & (HEAD_DIM - 1)) < HEAD_DIM // 2
    cosh = cosh_ref[...]
    sinh_signed = jnp.where(first_half, -sinh_ref[...], sinh_ref[...])

    def rope_heads(y):
        partner = jnp.where(first_half, pltpu.roll(y, LANES - HEAD_DIM // 2, 1), pltpu.roll(y, HEAD_DIM // 2, 1))
        return y * cosh + partner * sinh_signed

    def pair_store(j, out, slab):
        swapped = pltpu.roll(slab, HEAD_DIM, 1)
        out[j, 0] = jnp.where(lo, slab, 0.0).astype(BF16)
        out[j, 1] = jnp.where(lo, 0.0, swapped).astype(BF16)
        out[j, 2] = jnp.where(lo, swapped, 0.0).astype(BF16)
        out[j, 3] = jnp.where(lo, 0.0, slab).astype(BF16)

    def heads(j, z):
        g_cq = vec_ref[0, VEC_GCQ:VEC_GCQ + 1, 0:256]
        cq = (z[:, 0:256] * _lane_rinv(z[:, 0:256], 256.0) * g_cq).astype(BF16)
        q = _dot(cq, wuq_ref[0])
        qr = _dot(cq, wuqr_ref[0])
        ckv = (z[:, 256:384] * _lane_rinv(z[:, 256:384], 128.0) * _gain(vec_ref, GAIN_CKV)).astype(BF16)
        kn = _dot(ckv, wuk_ref[0])
        vv = _dot(ckv, wuv_ref[0])
        kr = z[:, 384:512]
        krr = z[:, 512:640]
        gc_q = _gain(vec_ref, GAIN_MQ) * cosm
        gc_k = _gain(vec_ref, GAIN_MK) * cosm
        for hh in range(MLA_HEADS):
            sl = slice(hh * LANES, (hh + 1) * LANES)
            r = _lane_rinv(q[:, sl], float(MLA_QK))
            mq_o[j, hh] = ((q[:, sl] * r) * gc_q + (qr[:, sl] * r) * sinm).astype(BF16)
            ks = kn[:, sl] + kr
            r = _lane_rinv(ks, float(MLA_QK))
            mk_o[j, hh] = ((ks * r) * gc_k + (krr * r) * sinm).astype(BF16)
        mv_o[j] = vv.T.astype(BF16)

        def gqa(c0, g_q, g_k, q_o, k_o, v_o, v_transposed):
            zq = z[:, c0:c0 + 256]
            r = _head_rinv(zq, gmat)
            for i in range(2):
                sl = slice(i * LANES, (i + 1) * LANES)
                qs = rope_heads((zq[:, sl] * r[:, sl]) * _gain(vec_ref, g_q)).astype(BF16)
                q_o[j, 2 * i] = qs
                q_o[j, 2 * i + 1] = qs
            zk = z[:, c0 + 256:c0 + 384]
            r = _head_rinv(zk, gmat[:LANES, :LANES])
            pair_store(j, k_o, rope_heads((zk * r) * _gain(vec_ref, g_k)))
            zv = z[:, c0 + 384:c0 + 512]
            if v_transposed:
                zvt = zv.T.astype(BF16)
                v_o[j] = jnp.concatenate([zvt[:HEAD_DIM], zvt[:HEAD_DIM], zvt[HEAD_DIM:], zvt[HEAD_DIM:]], axis=0)
            else:
                pair_store(j, v_o, zv)

        gqa(640, GAIN_SQ, GAIN_SK, sq_o, sk_o, sv_o, False)
        gqa(1152, GAIN_AQ, GAIN_AK, aq_o, ak_o, av_o, True)
        lru_o[j] = z[:, 1664:1920]

    for j in range(n_sub):
        heads(j, project_in(j))


def _project(x, ctx, mod_all, w, tabs, layer):
    split_ctx = ctx is not None
    b, d = x.shape[0], x.shape[2]
    s = x.shape[1] + (ctx.shape[1] if split_ctx else 0)
    nt = s // TILE
    ctx_row = b

    assert b % SUB_BATCH == 0 and ctx_row % SUB_BATCH == 0
    nbb = b // SUB_BATCH

    def tile_spec(width):
        return pl.BlockSpec((SUB_BATCH, TILE, width), lambda t, i: (i, t, 0))

    def head_spec(nh):
        return pl.BlockSpec((SUB_BATCH, nh, TILE, LANES), lambda t, i: (i, 0, t, 0))

    def head_spec_t(nh):
        return pl.BlockSpec((SUB_BATCH, nh * HEAD_DIM, TILE), lambda t, i: (i, 0, t))

    if split_ctx:
        streams = [x, ctx]
        stream_specs = [pl.BlockSpec((SUB_BATCH, TILE, d), lambda t, i: (i, jnp.maximum(t - 1, 0), 0)),
                        pl.BlockSpec((SUB_BATCH, TILE, d), lambda t, i: (jnp.where(t == 0, i, nbb - 1), 0, 0))]
    else:
        streams, stream_specs = [x], [tile_spec(d)]
    tab_spec = pl.BlockSpec((TILE, LANES), lambda t, i: (t, 0))
    consts = [w["vecs"], w["wp"], w["wuq"], w["wuqr"], w["wuk"], w["wuv"]]
    in_specs = (stream_specs
                + [pl.BlockSpec((1, SUB_BATCH, 3, d),
                                lambda t, i: (layer, jnp.where(t == 0, ctx_row // SUB_BATCH, i), 0, 0))]
                + [_layer_spec(c, layer) for c in consts]
                + [pl.BlockSpec(w["gmat"].shape, lambda t, i: (0, 0))] + [tab_spec] * 4)
    n_slabs = [MLA_HEADS, MLA_HEADS, MLA_HEADS, 4, 4, 4, 4, 4, 4]
    transposed = [False, False, True, False, False, False, False, False, True]
    out_specs = ([head_spec_t(nh) if tr else head_spec(nh) for nh, tr in zip(n_slabs, transposed)]
                 + [tile_spec(4 * HEAD_DIM)])
    out_shape = ([jax.ShapeDtypeStruct((b, nh * HEAD_DIM, s) if tr else (b, nh, s, LANES), BF16)
                  for nh, tr in zip(n_slabs, transposed)]
                 + [jax.ShapeDtypeStruct((b, s, 4 * HEAD_DIM), F32)])
    return pl.pallas_call(
        functools.partial(_proj_kernel, split_ctx=split_ctx),
        grid=(nt, nbb),
        in_specs=in_specs,
        out_specs=out_specs,
        out_shape=out_shape,
        compiler_params=_cparams(("arbitrary", "arbitrary")),
        name="project",
    )(*streams, mod_all, *consts, w["gmat"], *tabs)


def _dense_attn_kernel(q_ref, k_ref, v_ref, o_ref, *, q_share, ctx_tile_first):
    s_len = k_ref.shape[2]
    n_heads = k_ref.shape[1]
    v_share = n_heads * HEAD_DIM // v_ref.shape[1]

    def attend(n_keys):
        n_groups = n_keys // LANES
        n_chunks = min(KEY_CHUNKS, n_groups)
        edges = [(c * n_groups // n_chunks) * LANES for c in range(n_chunks + 1)]
        chunks = [slice(a, b) for a, b in zip(edges[:-1], edges[1:])]

        def scores(unit):
            j, hh = unit
            q = q_ref[j, hh // q_share]
            return [_dot_nt(k_ref[j, hh, sl, :], q) for sl in chunks]

        def head_out(unit, st):
            j, hh = unit
            m = functools.reduce(jnp.maximum, [jnp.max(sc, axis=0, keepdims=True) for sc in st])
            pt = [jnp.exp2(sc - m) for sc in st]
            l = sum(jnp.sum(pc, axis=0, keepdims=True) for pc in pt)
            band = slice(hh // v_share * HEAD_DIM, (hh // v_share + 1) * HEAD_DIM)
            ot = sum(_dot(v_ref[j, band, sl], pc.astype(BF16)) for sl, pc in zip(chunks, pt))
            return ot / l

        units = [(j, hh) for j in range(q_ref.shape[0]) for hh in range(n_heads)]
        st_next = scores(units[0])
        outs = []
        for u, unit in enumerate(units):
            st = st_next
            if u + 1 < len(units):
                st_next = scores(units[u + 1])
            outs.append(head_out(unit, st))
        for j in range(q_ref.shape[0]):
            o_ref[j] = jnp.concatenate(outs[j * n_heads:(j + 1) * n_heads], axis=0).T

    if ctx_tile_first:
        pl.when(pl.program_id(1) == 0)(lambda: attend(TILE))
        pl.when(pl.program_id(1) > 0)(lambda: attend(s_len))
    else:
        attend(s_len)


def _dense_attention(q, k, v, with_ctx_queries):
    b, nq_slabs, s, _ = q.shape
    nh = k.shape[1]
    nt = s // TILE
    off = 0 if with_ctx_queries else 1
    nq = nt - off
    assert b % SUB_BATCH == 0
    kern = functools.partial(_dense_attn_kernel, q_share=nh // nq_slabs,
                             ctx_tile_first=with_ctx_queries)
    return pl.pallas_call(
        kern,
        grid=(b // SUB_BATCH, nq),
        in_specs=[pl.BlockSpec((SUB_BATCH, nq_slabs, TILE, LANES), lambda i, t: (i, 0, t + off, 0)),
                  pl.BlockSpec((SUB_BATCH, nh, s, LANES), lambda i, t: (i, 0, 0, 0)),
                  pl.BlockSpec((SUB_BATCH,) + v.shape[1:], lambda i, t: (i, 0, 0))],
        out_specs=pl.BlockSpec((SUB_BATCH, TILE, 2 * LANES), lambda i, t: (i, t, 0)),
        out_shape=jax.ShapeDtypeStruct((b, nq * TILE, 2 * LANES), F32),
        compiler_params=_cparams(("arbitrary", "arbitrary")),
        name="dense_attention",
    )(q, k, v)


def _window_attn_kernel(sink_ref, q_ref, k_ref, v_ref, o_ref, *, tile_off, layer):
    s_len = k_ref.shape[2]
    n_tiles = s_len // TILE
    ti = pl.program_id(1) + tile_off
    is_lat = ti > 0
    t0 = ti * TILE
    half = TILE // 2
    left0 = pl.multiple_of(jnp.maximum(t0 - half, 0), half)
    cen0 = pl.multiple_of(t0, TILE)
    right0 = pl.multiple_of(jnp.minimum(t0 + TILE, s_len - half), half)
    r_c = lax.broadcasted_iota(jnp.int32, (TILE, TILE), 0)
    c_c = lax.broadcasted_iota(jnp.int32, (TILE, TILE), 1)
    r_s = lax.broadcasted_iota(jnp.int32, (TILE, half), 0)
    c_s = lax.broadcasted_iota(jnp.int32, (TILE, half), 1)
    ok_cen = (jnp.abs(r_c - c_c) <= WINDOW) & is_lat
    ok_left = ((r_s - c_s + half) <= WINDOW) & (ti > 1)
    ok_right = ((c_s - r_s + TILE) <= WINDOW) & is_lat & (ti < n_tiles - 1)
    def lane_groups(a):
        return [a[:, g * LANES:(g + 1) * LANES] for g in range(a.shape[1] // LANES)]

    def scores(unit):
        j, hh = unit
        q = q_ref[j, hh]
        k = lambda start, size: k_ref[j, hh, pl.ds(start, size), :]
        return (_dot_nt(q, k(0, TILE)),
                jnp.where(ok_left, _dot_nt(q, k(left0, half)), NEG_INF),
                jnp.where(ok_cen, _dot_nt(q, k(cen0, TILE)), NEG_INF),
                jnp.where(ok_right, _dot_nt(q, k(right0, half)), NEG_INF))

    def head_out(unit, sc):
        j, hh = unit
        sink = sink_ref[layer, hh] * LOG2E
        m = functools.reduce(jnp.maximum, [g for piece in sc for g in lane_groups(piece)])
        m = jnp.maximum(jnp.max(m, -1, keepdims=True), sink)
        ps = [jnp.exp2(piece - m) for piece in sc]
        den = sum(g for piece in ps for g in lane_groups(piece))
        den = jnp.sum(den, -1, keepdims=True) + jnp.exp2(sink - m)
        v = lambda start, size: v_ref[j, hh, pl.ds(start, size), :]
        vs = (v(0, TILE), v(left0, half), v(cen0, TILE), v(right0, half))
        num = sum(_dot(p.astype(BF16), vv) for p, vv in zip(ps, vs))
        return num / den

    n_heads = k_ref.shape[1]
    units = [(j, hh) for j in range(q_ref.shape[0]) for hh in range(n_heads)]
    sc_next = scores(units[0])
    outs = []
    for u, unit in enumerate(units):
        sc = sc_next
        if u + 1 < len(units):
            sc_next = scores(units[u + 1])
        outs.append(head_out(unit, sc))
    for j in range(q_ref.shape[0]):
        o = outs[j * n_heads:(j + 1) * n_heads]
        o_ref[j] = jnp.concatenate([o[0] + o[1], o[2] + o[3]], axis=-1)


def _window_attention(sink, q, k, v, with_ctx_queries, layer):
    b, nq_slabs, s, _ = q.shape
    nh = k.shape[1]
    nt = s // TILE
    off = 0 if with_ctx_queries else 1
    nq = nt - off
    assert b % SUB_BATCH == 0
    kern = functools.partial(_window_attn_kernel, tile_off=off, layer=layer)
    return pl.pallas_call(
        kern,
        grid=(b // SUB_BATCH, nq),
        in_specs=[pl.BlockSpec(memory_space=pltpu.SMEM),
                  pl.BlockSpec((SUB_BATCH, nq_slabs, TILE, LANES), lambda i, t: (i, 0, t + off, 0)),
                  pl.BlockSpec((SUB_BATCH, nh, s, LANES), lambda i, t: (i, 0, 0, 0)),
                  pl.BlockSpec((SUB_BATCH, nh, s, LANES), lambda i, t: (i, 0, 0, 0))],
        out_specs=pl.BlockSpec((SUB_BATCH, TILE, 2 * LANES), lambda i, t: (i, t, 0)),
        out_shape=jax.ShapeDtypeStruct((b, nq * TILE, 2 * LANES), F32),
        compiler_params=_cparams(("arbitrary", "arbitrary")),
        name="window_attention",
    )(sink, q, k, v)


def _lru_conv(j, prev_ref, cur_ref, next_ref, prev_ok, next_ok, vec_ref):
    t = cur_ref.shape[1]
    prev = jnp.where(prev_ok, prev_ref[j], 0.0)
    nxt = jnp.where(next_ok, next_ref[j], 0.0)
    ext = jnp.concatenate([prev, cur_ref[j], nxt], axis=0)
    n = t + 2 * SUBLANES
    c = cur_ref.shape[2]

    def tap(j):
        return vec_ref[0, VEC_CONVW + j:VEC_CONVW + j + 1, 0:c]

    u = vec_ref[0, VEC_CONVB:VEC_CONVB + 1, 0:c] + cur_ref[j] * tap(CONV_LEFT)
    for j in range(CONV_W):
        off = j - CONV_LEFT
        if off == 0:
            continue
        sh = pltpu.roll(ext, (-off) % n, 0)[SUBLANES:SUBLANES + t]
        u = u + sh * tap(j)
    return u


def _lru_scan_tile(u, d, wri_ref, vec_ref, h0, reverse):
    t, c = u.shape

    def vec(row):
        return vec_ref[0, row + d:row + d + 1, 0:c]

    gates = _dot(u.astype(BF16), wri_ref[0, d])
    r = _sigmoid(gates[:, :c] + vec(VEC_BR))
    i = _sigmoid(gates[:, c:] + vec(VEC_BI))
    lam = vec(VEC_LAM)
    softplus_neg = jnp.maximum(-lam, 0.0) + jnp.log1p(jnp.exp(-jnp.abs(lam)))
    a = jnp.exp2(r * (softplus_neg * (-LRU_C * LOG2E)))
    bx = jnp.sqrt(1.0 - a * a) * (i * u)
    row = lax.broadcasted_iota(jnp.int32, u.shape, 0)
    step = 1
    while step < t:
        if reverse:
            ok = row < t - step
            a_sh = jnp.where(ok, pltpu.roll(a, t - step, 0), 1.0)
            b_sh = jnp.where(ok, pltpu.roll(bx, t - step, 0), 0.0)
        else:
            ok = row >= step
            a_sh = jnp.where(ok, pltpu.roll(a, step, 0), 1.0)
            b_sh = jnp.where(ok, pltpu.roll(bx, step, 0), 0.0)
        bx = a * b_sh + bx
        a = a * a_sh
        step *= 2
    h = a * h0 + bx
    h_end = h[0:1, :] if reverse else h[t - 1:t, :]
    return h, h_end


def _lru_kernel(fp_ref, fc_ref, fn_ref, rp_ref, rc_ref, rn_ref, vec_ref, wri_ref,
                yf_ref, yb_ref, hf_s, hb_s):
    j = pl.program_id(1)
    nt = pl.num_programs(1)

    @pl.when(j == 0)
    def _():
        hf_s[...] = jnp.zeros_like(hf_s)
        hb_s[...] = jnp.zeros_like(hb_s)

    def bounds(ti):
        return ti >= 2, (ti >= 1) & (ti < nt - 1)

    tf = j
    tr = jnp.where(j == 0, 0, nt - j)
    for e in range(fc_ref.shape[0]):
        p_ok, n_ok = bounds(tf)
        uf = _lru_conv(e, fp_ref, fc_ref, fn_ref, p_ok, n_ok, vec_ref)
        h, h_end = _lru_scan_tile(uf, 0, wri_ref, vec_ref, hf_s[e], False)
        yf_ref[e] = h
        hf_s[e] = h_end
        p_ok, n_ok = bounds(tr)
        ur = _lru_conv(e, rp_ref, rc_ref, rn_ref, p_ok, n_ok, vec_ref)
        h, h_end = _lru_scan_tile(ur, 1, wri_ref, vec_ref, hb_s[e], True)
        yb_ref[e] = h
        hb_s[e] = h_end


def _rglru(z, w, layer):
    b, s, c = z.shape
    nt = s // TILE
    r8 = TILE // SUBLANES
    n8 = s // SUBLANES

    def rev(j):
        return jnp.where(j == 0, 0, nt - j)

    assert b % SUB_BATCH == 0

    def cur(f):
        return pl.BlockSpec((SUB_BATCH, TILE, c), lambda i, j: (i, f(j), 0))

    def prev(f):
        return pl.BlockSpec((SUB_BATCH, SUBLANES, c), lambda i, j: (i, jnp.maximum(f(j) * r8 - 1, 0), 0))

    def nxt(f):
        return pl.BlockSpec((SUB_BATCH, SUBLANES, c), lambda i, j: (i, jnp.minimum((f(j) + 1) * r8, n8 - 1), 0))

    ident = lambda j: j
    consts = [w["vecs"], w["wri"]]
    return pl.pallas_call(
        _lru_kernel,
        grid=(b // SUB_BATCH, nt),
        in_specs=[prev(ident), cur(ident), nxt(ident), prev(rev), cur(rev), nxt(rev)]
                 + [_layer_spec(x, layer) for x in consts],
        out_specs=[cur(ident), cur(rev)],
        out_shape=[jax.ShapeDtypeStruct((b, s, c), F32)] * 2,
        scratch_shapes=[pltpu.VMEM((SUB_BATCH, 1, c), F32), pltpu.VMEM((SUB_BATCH, 1, c), F32)],
        compiler_params=_cparams(("arbitrary", "arbitrary")),
        name="rglru",
    )(z, z, z, z, z, z, *consts)


def _merge_kernel(*refs, split_ctx, ctx_tile_first):
    x_ref, ctx_ref = (refs[0], refs[1]) if split_ctx else (refs[0], None)
    (mod_ref, vec_ref, wg_ref, wm_ref, ya_ref, yb_ref, yc_ref, yf_ref, yr_ref,
     wb_ref, wo_ref, o_ref) = refs[2 if split_ctx else 1:]
    is_ctx = (pl.program_id(1) == 0) if ctx_tile_first else False
    d = x_ref.shape[2]
    bw = ya_ref.shape[2]
    for j in range(x_ref.shape[0]):
        x = _stream_tile(x_ref, ctx_ref, is_ctx, j)
        mod = mod_ref[0, pl.ds(jnp.where(is_ctx, 0, j), 1)][0]
        hb = _modulated_norm(x, mod, vec_ref).astype(BF16)
        zg = _dot(hb, wg_ref[0])
        branches = (ya_ref[j], yb_ref[j], yc_ref[j], yf_ref[j] + yr_ref[j])
        mix = jnp.zeros_like(x)
        for k in range(N_BRANCH):
            g = zg[:, k * bw:(k + 1) * bw]
            y = branches[k] * (g * _sigmoid(g))
            proj = _dot(y.astype(BF16), wb_ref[0, k])
            zm = _dot(hb, wm_ref[0, :, k * d:(k + 1) * d])
            mix = mix + _sigmoid(zm) * proj
        o_ref[j] = x + mod[2:3, :] * _dot(mix.astype(BF16), wo_ref[0])


def _merge(x, ctx, mod_all, w, ya, yb, yc, yf, yr, with_ctx, layer):
    split_ctx = ctx is not None
    assert with_ctx or not split_ctx
    b, d = x.shape[0], x.shape[2]
    s = x.shape[1] + (ctx.shape[1] if split_ctx else 0)
    nt = s // TILE
    off = 0 if with_ctx else 1
    nq = nt - off
    ctx_row = b
    bw = ya.shape[2]
    assert b % SUB_BATCH == 0 and ctx_row % SUB_BATCH == 0
    nbb = b // SUB_BATCH

    def stream(width):
        return pl.BlockSpec((SUB_BATCH, TILE, width), lambda i, t: (i, t + off, 0))

    def local(width):
        return pl.BlockSpec((SUB_BATCH, TILE, width), lambda i, t: (i, t, 0))

    if split_ctx:
        streams = [x, ctx]
        stream_specs = [pl.BlockSpec((SUB_BATCH, TILE, d), lambda i, t: (i, jnp.maximum(t - 1, 0), 0)),
                        pl.BlockSpec((SUB_BATCH, TILE, d), lambda i, t: (i, 0, 0))]
    else:
        streams, stream_specs = [x], [stream(d)]
    consts_a = [w["vecs"], w["wg"], w["wm"]]
    consts_b = [w["wb"], w["wo"]]
    return pl.pallas_call(
        functools.partial(_merge_kernel, split_ctx=split_ctx, ctx_tile_first=with_ctx),
        grid=(nbb, nq),
        in_specs=stream_specs
                 + [pl.BlockSpec((1, SUB_BATCH, 3, d),
                                 lambda i, t: (layer, jnp.where(t + off == 0, ctx_row // SUB_BATCH, i), 0, 0))]
                 + [_layer_spec(c, layer) for c in consts_a]
                 + [local(bw), local(bw), local(bw), stream(bw), stream(bw)]
                 + [_layer_spec(c, layer) for c in consts_b],
        out_specs=local(d),
        out_shape=jax.ShapeDtypeStruct((b, nq * TILE, d), F32),
        compiler_params=_cparams(("arbitrary", "arbitrary")),
        name="merge",
    )(*streams, mod_all, *consts_a, ya, yb, yc, yf, yr, *consts_b)


def _rope_tables(n, n_ctx):
    def axial(rot_dim):
        n_rows = n // GRID_W
        rows = np.repeat(np.arange(n_rows, dtype=np.float64), GRID_W)
        cols = np.tile(np.arange(GRID_W, dtype=np.float64), n_rows)
        quarter = rot_dim // 4
        freqs = ROPE_THETA ** (-np.arange(quarter, dtype=np.float64) / quarter)
        ang = np.concatenate([rows[:, None] * freqs, cols[:, None] * freqs], axis=-1)
        return np.cos(ang), np.sin(ang)

    def with_ctx(cos, sin):
        return (jnp.asarray(np.concatenate([np.ones((n_ctx, LANES)), cos], axis=0), F32),
                jnp.asarray(np.concatenate([np.zeros((n_ctx, LANES)), sin], axis=0), F32))

    cm, sm = axial(MLA_ROPE)
    one, zero = np.ones((n, MLA_NOPE)), np.zeros((n, MLA_NOPE))
    pad1, pad0 = np.ones((n, LANES - MLA_QK)), np.zeros((n, LANES - MLA_QK))
    cosm, sinm = with_ctx(np.concatenate([one, cm, cm, pad1], axis=-1),
                          np.concatenate([zero, sm, sm, pad0], axis=-1))
    ch, sh = axial(HEAD_DIM)
    cosh, sinh = with_ctx(np.tile(ch, (1, 4)), np.tile(sh, (1, 4)))
    return cosm, sinm, cosh, sinh


def _place(w, mat, gain=None):
    wg = w if gain is None else w * gain[:, None, :]
    return jnp.einsum("lkc,cp->lkp", wg, jnp.asarray(mat, F32), precision=lax.Precision.HIGHEST)


def _copy_matrix(n_heads, in_width, in_start, count, out_width, out_start):
    m = np.zeros((n_heads * in_width, n_heads * out_width), np.float32)
    for h in range(n_heads):
        for j in range(count):
            m[h * in_width + in_start + j, h * out_width + out_start + j] = 1.0
    return m


def _partner_matrix(n_heads, in_width, in_start, half, out_width, out_start):
    m = np.zeros((n_heads * in_width, n_heads * out_width), np.float32)
    for h in range(n_heads):
        for j in range(half):
            x1, x2 = h * in_width + in_start + j, h * in_width + in_start + half + j
            o1, o2 = h * out_width + out_start + j, h * out_width + out_start + half + j
            m[x2, o1] = -1.0
            m[x1, o2] = 1.0
    return m


IN_SIZES = (256, 128, 32, 256, 128, 128, 256, 128, 128, 256, 1024)
PREP_ROWS = 256


def _prep_kernel(wt_ref, vec_ref, pc_ref, pm_ref, wp_o, wg_o, wm_o):
    offs = [0]
    for sz in IN_SIZES:
        offs.append(offs[-1] + sz)

    def rows(a, b):
        return wt_ref[0, a:b, :].T

    def piece(i):
        return rows(offs[i], offs[i + 1])

    def put(start, val):
        wp_o[0, :, start:start + val.shape[1]] = val.astype(BF16)

    def partner(val, gain, mat):
        return _dot((val * gain).astype(BF16), mat)

    put(0, piece(0))
    put(256, piece(1))
    z = rows(offs[2], offs[2] + LANES)
    hi = z.astype(BF16)
    r1 = z - hi.astype(F32)
    mid = r1.astype(BF16)
    lo = (r1 - mid.astype(F32)).astype(BF16)
    pc = pc_ref[...]
    kr_slab = _dot(hi, pc) + _dot(mid, pc) + _dot(lo, pc)
    put(384, kr_slab)
    put(512, partner(kr_slab, _gain(vec_ref, GAIN_MK), pm_ref[...]))
    col = 640
    for first in (3, 6):
        put(col, piece(first))
        put(col + 256, piece(first + 1))
        put(col + 384, piece(first + 2))
        col += 512
    put(col, piece(9))
    wg_o[0] = piece(10).astype(BF16)
    for c in range(offs[11], wt_ref.shape[1], 4 * LANES):
        wm_o[0, :, c - offs[11]:c - offs[11] + 4 * LANES] = rows(c, c + 4 * LANES).astype(BF16)


def _prep_in_weights(w_in, vecs):
    depth, d, cols = w_in.shape
    n_merge = cols - sum(IN_SIZES)
    mats = [_copy_matrix(1, LANES, 0, MLA_ROPE, LANES, MLA_NOPE),
            _partner_matrix(1, LANES, MLA_NOPE, MLA_ROPE // 2, LANES, MLA_NOPE)]
    mats = [jnp.asarray(m, BF16) for m in mats]
    wp_cols = 1920

    def out_rows(width):
        return pl.BlockSpec((1, PREP_ROWS, width), lambda l, r: (l, r, 0))

    return pl.pallas_call(
        _prep_kernel,
        grid=(depth, d // PREP_ROWS),
        in_specs=[pl.BlockSpec((1, cols, PREP_ROWS), lambda l, r: (l, 0, r)),
                  pl.BlockSpec((1,) + vecs.shape[1:], lambda l, r: (l, 0, 0))]
                 + [pl.BlockSpec(m.shape, lambda l, r: (0, 0)) for m in mats],
        out_specs=[out_rows(wp_cols), out_rows(IN_SIZES[10]), out_rows(n_merge)],
        out_shape=[jax.ShapeDtypeStruct((depth, d, wp_cols), BF16),
                   jax.ShapeDtypeStruct((depth, d, IN_SIZES[10]), BF16),
                   jax.ShapeDtypeStruct((depth, d, n_merge), BF16)],
        compiler_params=_cparams(("arbitrary", "arbitrary")),
        name="prep_weights",
    )(jnp.transpose(w_in, (0, 2, 1)), vecs, *mats)


def _prepare_weights(p):
    depth, d, _ = p["w_in"].shape
    q_scale = HEAD_DIM ** -0.5 * LOG2E
    mq_scale = MLA_QK ** -0.5 * LOG2E
    g_sq = jnp.tile(p["swa_q_norm"] * q_scale, (1, 2))
    g_sk = jnp.tile(p["swa_k_norm"], (1, 2))
    g_aq = jnp.tile(p["axa_q_norm"] * q_scale, (1, 2))
    g_ak = jnp.tile(p["axa_k_norm"], (1, 2))
    g_mq = p["mla_q_norm"] * mq_scale
    g_mk = p["mla_k_norm"]

    w_uq, w_ukv = p["mla_w_uq"], p["mla_w_ukv"]
    kv_width = MLA_NOPE + MLA_V
    wuq = _place(w_uq, _copy_matrix(MLA_HEADS, MLA_QK, 0, MLA_QK, LANES, 0)).astype(BF16)
    wuqr = _place(w_uq, _partner_matrix(MLA_HEADS, MLA_QK, MLA_NOPE, MLA_ROPE // 2, LANES, MLA_NOPE),
                  jnp.tile(g_mq, (1, MLA_HEADS))).astype(BF16)
    wuk = _place(w_ukv, _copy_matrix(MLA_HEADS, kv_width, 0, MLA_NOPE, LANES, 0)).astype(BF16)
    wuv = _place(w_ukv, _copy_matrix(MLA_HEADS, kv_width, MLA_NOPE, MLA_V, MLA_V, 0)).astype(BF16)

    def row(v):
        v = v if v.ndim == 3 else v[:, None, :]
        return jnp.pad(v, ((0, 0), (0, 0), (0, d - v.shape[2])))

    slab = lambda g: jnp.pad(g, ((0, 0), (0, LANES - g.shape[1])))
    gains = jnp.concatenate([p["mla_ckv_norm"], slab(g_mq), slab(g_mk), g_sq, g_sk, g_aq, g_ak], axis=1)
    vec_rows = [row(p["norm_w"]), row(p["mla_cq_norm"]), row(gains), row(p["lru_conv_w"]),
                row(p["lru_conv_b"]), row(p["lru_b_r"]), row(p["lru_b_i"]), row(p["lru_lambda"])]
    n_rows = sum(r.shape[1] for r in vec_rows)
    vecs = jnp.concatenate(vec_rows + [jnp.zeros((depth, VEC_ROWS - n_rows, d), F32)], axis=1)
    wp, wg, wm = _prep_in_weights(p["w_in"], vecs)

    idx = np.arange(4 * HEAD_DIM) // HEAD_DIM
    gmat = jnp.asarray(idx[:, None] == idx[None, :], BF16)

    def block_diag(wb):
        l, two, k, m, _ = wb.shape
        return jnp.einsum("ldkij,kn->ldkinj", wb, jnp.eye(k, dtype=wb.dtype)).reshape(l, two, k * m, k * m)

    wri = jnp.concatenate([block_diag(p["lru_w_r"]), block_diag(p["lru_w_i"])], axis=-1).astype(BF16)
    return {
        "vecs": vecs, "gmat": gmat,
        "wp": wp, "wuq": wuq, "wuqr": wuqr, "wuk": wuk, "wuv": wuv,
        "wg": wg, "wm": wm,
        "wb": p["w_branch"].astype(BF16), "wo": p["w_out"].astype(BF16),
        "wri": wri,
    }


def kernel(x, c, ctx, c_ctx, w_mod, b_mod, norm_w, w_in, mla_cq_norm, mla_ckv_norm, mla_w_uq, mla_w_ukv, mla_q_norm, mla_k_norm, swa_q_norm, swa_k_norm, swa_sink, axa_q_norm, axa_k_norm, lru_conv_w, lru_conv_b, lru_w_r, lru_b_r, lru_w_i, lru_b_i, lru_lambda, w_branch, w_out):
    p = dict(norm_w=norm_w, w_in=w_in, mla_cq_norm=mla_cq_norm, mla_ckv_norm=mla_ckv_norm,
             mla_w_uq=mla_w_uq, mla_w_ukv=mla_w_ukv, mla_q_norm=mla_q_norm, mla_k_norm=mla_k_norm,
             swa_q_norm=swa_q_norm, swa_k_norm=swa_k_norm,
             axa_q_norm=axa_q_norm, axa_k_norm=axa_k_norm, lru_conv_w=lru_conv_w,
             lru_conv_b=lru_conv_b, lru_w_r=lru_w_r, lru_b_r=lru_b_r, lru_w_i=lru_w_i,
             lru_b_i=lru_b_i, lru_lambda=lru_lambda, w_branch=w_branch, w_out=w_out)
    b, n, d = x.shape
    n_ctx = ctx.shape[1]
    depth = w_mod.shape[0]
    assert n_ctx == TILE and n % TILE == 0 and n % GRID_W == 0 and b + 1 <= SUBLANES

    rows = jnp.concatenate([c, c_ctx[None, :], jnp.zeros((SUBLANES - b - 1, d), F32)], axis=0)
    mod_all = _modulation(rows, w_mod, b_mod).reshape(depth, SUBLANES, 3, d)

    w = _prepare_weights(p)
    tabs = _rope_tables(n, n_ctx)
    xs, cs = x, ctx
    for layer in range(depth):
        upd = layer < depth - 1
        mq, mk, mv, sq, sk, sv, aq, ak, av, zl = _project(xs, cs, mod_all, w, tabs, layer)
        ya = _dense_attention(mq, mk, mv, upd)
        yb = _window_attention(swa_sink, sq, sk, sv, upd, layer)
        yc = _dense_attention(aq, ak, av, upd)
        yf, yr = _rglru(zl, w, layer)
        if cs is not None and not upd:
            xs, cs = jnp.concatenate([cs, xs], axis=1), None
        xs, cs = _merge(xs, cs, mod_all, w, ya, yb, yc, yf, yr, upd, layer), None
    return xs
```

```python
import functools

import numpy as np
import jax
import jax.numpy as jnp
from jax import lax
from jax.experimental import pallas as pl
from jax.experimental.pallas import tpu as pltpu

GRID_W = 64
N_BRANCH = 4
HEAD_DIM = 64
MLA_HEADS = 4
MLA_NOPE = 64
MLA_ROPE = 32
MLA_V = 64
MLA_QK = MLA_NOPE + MLA_ROPE
SWA_Q_HEADS = 4
SWA_KV_HEADS = 2
WINDOW = 128
AXA_Q_HEADS = 4
AXA_KV_HEADS = 2
LRU_BLOCKS = 4
LRU_C = 8.0
CONV_W = 4
CONV_LEFT = 2
ROPE_THETA = 10000.0
RMS_EPS = 1e-6
NEG_INF = -1e30
LOG2E = 1.4426950408889634

LANES = 128
SUBLANES = 8
TILE = 256
SUB_BATCH = 2
KEY_CHUNKS = 4
VMEM_LIMIT = 56 * 1024 * 1024

F32 = jnp.float32
BF16 = jnp.bfloat16

VEC_NW, VEC_GCQ, VEC_GAINS, VEC_CONVW, VEC_CONVB, VEC_BR, VEC_BI, VEC_LAM, VEC_ROWS = 0, 1, 2, 3, 7, 8, 10, 12, 16
GAIN_CKV, GAIN_MQ, GAIN_MK, GAIN_SQ, GAIN_SK, GAIN_AQ, GAIN_AK = range(7)


def _dot(a, b):
    return jnp.dot(a, b, preferred_element_type=F32)


def _dot_nt(a, b):
    return lax.dot_general(a, b, (((1,), (1,)), ((), ())), preferred_element_type=F32)


def _split(a):
    hi = a.astype(BF16)
    lo = (a - hi.astype(F32)).astype(BF16)
    return hi, lo


def _dot3(a, w_hi, w_lo):
    a_hi, a_lo = _split(a)
    return _dot(a_hi, w_hi) + _dot(a_lo, w_hi) + _dot(a_hi, w_lo)


def _sigmoid(v):
    return 1.0 / (1.0 + jnp.exp2(v * -LOG2E))


def _cparams(sem):
    return pltpu.CompilerParams(dimension_semantics=sem, vmem_limit_bytes=VMEM_LIMIT)


def _layer_spec(arr, layer):
    nd = arr.ndim - 1
    return pl.BlockSpec((1,) + arr.shape[1:], lambda *_: (layer,) + (0,) * nd)


def _gain(vec_ref, slot):
    return vec_ref[0, VEC_GAINS:VEC_GAINS + 1, slot * LANES:(slot + 1) * LANES]


def _mod_kernel(c_ref, w_ref, b_ref, o_ref):
    v = c_ref[...]
    a = v * _sigmoid(v)
    w_hi, w_lo = _split(w_ref[0])
    o_ref[0] = _dot3(a, w_hi, w_lo) + b_ref[0]


def _modulation(cvec, w_mod, b_mod):
    depth, d, d3 = w_mod.shape
    r = cvec.shape[0]
    bn = 512
    return pl.pallas_call(
        _mod_kernel,
        grid=(depth, d3 // bn),
        in_specs=[pl.BlockSpec((r, d), lambda l, j: (0, 0)),
                  pl.BlockSpec((1, d, bn), lambda l, j: (l, 0, j)),
                  pl.BlockSpec((1, 1, bn), lambda l, j: (l, 0, j))],
        out_specs=pl.BlockSpec((1, r, bn), lambda l, j: (l, 0, j)),
        out_shape=jax.ShapeDtypeStruct((depth, r, d3), F32),
        compiler_params=_cparams(("arbitrary", "arbitrary")),
        name="modulation",
    )(cvec, w_mod, b_mod.reshape(depth, 1, d3))


def _modulated_norm(x, mod, vec_ref):
    ms = jnp.mean(x * x, axis=-1, keepdims=True)
    y = x * lax.rsqrt(ms + RMS_EPS) * vec_ref[0, VEC_NW:VEC_NW + 1, :]
    return y * (1.0 + mod[1:2, :]) + mod[0:1, :]


def _stream_tile(x_ref, ctx_ref, is_ctx, j=0):
    if ctx_ref is None:
        return x_ref[j]
    return jnp.where(is_ctx, ctx_ref[j], x_ref[j])


def _lane_rinv(v, n):
    return lax.rsqrt(jnp.sum(v * v, axis=-1, keepdims=True) * (1.0 / n) + RMS_EPS)


def _head_rinv(v, gmat):
    hi, lo = _split(v * v)
    ss = _dot(hi, gmat) + _dot(lo, gmat)
    return lax.rsqrt(ss * (1.0 / HEAD_DIM) + RMS_EPS)


def _proj_kernel(*refs, split_ctx):
    x_ref, ctx_ref = (refs[0], refs[1]) if split_ctx else (refs[0], None)
    (mod_ref, vec_ref, wp_ref, wuq_ref, wuqr_ref, wuk_ref, wuv_ref, gmat_ref,
     cosm_ref, sinm_ref, cosh_ref, sinh_ref,
     mq_o, mk_o, mv_o, sq_o, sk_o, sv_o, aq_o, ak_o, av_o, lru_o) = refs[2 if split_ctx else 1:]
    n_sub = x_ref.shape[0]
    is_ctx = pl.program_id(0) == 0
    def project_in(j):
        x = _stream_tile(x_ref, ctx_ref, is_ctx, j)
        mod = mod_ref[0, pl.ds(jnp.where(is_ctx, 0, j), 1)][0]
        hb = _modulated_norm(x, mod, vec_ref).astype(BF16)
        return _dot(hb, wp_ref[0])

    t = x_ref.shape[1]
    lo = lax.broadcasted_iota(jnp.int32, (t, LANES), 1) < HEAD_DIM
    cosm, sinm = cosm_ref[...], sinm_ref[...]
    gmat = gmat_ref[...]
    first_half = (lax.broadcasted_iota(jnp.int32, (t, LANES), 1) & (HEAD_DIM - 1)) < HEAD_DIM // 2
    cosh = cosh_ref[...]
    sinh_signed = jnp.where(first_half, -sinh_ref[...], sinh_ref[...])

    def rope_heads(y):
        partner = jnp.where(first_half, pltpu.roll(y, LANES - HEAD_DIM // 2, 1), pltpu.roll(y, HEAD_DIM // 2, 1))
        return y * cosh + partner * sinh_signed

    def pair_store(j, out, slab):
        swapped = pltpu.roll(slab, HEAD_DIM, 1)
        out[j, 0] = jnp.where(lo, slab, 0.0).astype(BF16)
        out[j, 1] = jnp.where(lo, 0.0, swapped).astype(BF16)
        out[j, 2] = jnp.where(lo, swapped, 0.0).astype(BF16)
        out[j, 3] = jnp.where(lo, 0.0, slab).astype(BF16)

    def heads(j, z):
        g_cq = vec_ref[0, VEC_GCQ:VEC_GCQ + 1, 0:256]
        cq = (z[:, 0:256] * _lane_rinv(z[:, 0:256], 256.0) * g_cq).astype(BF16)
        q = _dot(cq, wuq_ref[0])
        qr = _dot(cq, wuqr_ref[0])
        ckv = (z[:, 256:384] * _lane_rinv(z[:, 256:384], 128.0) * _gain(vec_ref, GAIN_CKV)).astype(BF16)
        kn = _dot(ckv, wuk_ref[0])
        vv = _dot(ckv, wuv_ref[0])
        kr = z[:, 384:512]
        krr = z[:, 512:640]
        gc_q = _gain(vec_ref, GAIN_MQ) * cosm
        gc_k = _gain(vec_ref, GAIN_MK) * cosm
        for hh in range(MLA_HEADS):
            sl = slice(hh * LANES, (hh + 1) * LANES)
            r = _lane_rinv(q[:, sl], float(MLA_QK))
            mq_o[j, hh] = ((q[:, sl] * r) * gc_q + (qr[:, sl] * r) * sinm).astype(BF16)
            ks = kn[:, sl] + kr
            r = _lane_rinv(ks, float(MLA_QK))
            mk_o[j, hh] = ((ks * r) * gc_k + (krr * r) * sinm).astype(BF16)
        mv_o[j] = vv.T.astype(BF16)

        def gqa(c0, g_q, g_k, q_o, k_o, v_o, v_transposed):
            zq = z[:, c0:c0 + 256]
            r = _head_rinv(zq, gmat)
            for i in range(2):
                sl = slice(i * LANES, (i + 1) * LANES)
                qs = rope_heads((zq[:, sl] * r[:, sl]) * _gain(vec_ref, g_q)).astype(BF16)
                q_o[j, 2 * i] = qs
                q_o[j, 2 * i + 1] = qs
            zk = z[:, c0 + 256:c0 + 384]
            r = _head_rinv(zk, gmat[:LANES, :LANES])
            pair_store(j, k_o, rope_heads((zk * r) * _gain(vec_ref, g_k)))
            zv = z[:, c0 + 384:c0 + 512]
            if v_transposed:
                zvt = zv.T.astype(BF16)
                v_o[j] = jnp.concatenate([zvt[:HEAD_DIM], zvt[:HEAD_DIM], zvt[HEAD_DIM:], zvt[HEAD_DIM:]], axis=0)
            else:
                pair_store(j, v_o, zv)

        gqa(640, GAIN_SQ, GAIN_SK, sq_o, sk_o, sv_o, False)
        gqa(1152, GAIN_AQ, GAIN_AK, aq_o, ak_o, av_o, True)
        lru_o[j] = z[:, 1664:1920]

    for j in range(n_sub):
        heads(j, project_in(j))


def _project(x, ctx, mod_all, w, tabs, layer):
    split_ctx = ctx is not None
    b, d = x.shape[0], x.shape[2]
    s = x.shape[1] + (ctx.shape[1] if split_ctx else 0)
    nt = s // TILE
    ctx_row = b

    assert b % SUB_BATCH == 0 and ctx_row % SUB_BATCH == 0
    nbb = b // SUB_BATCH

    def tile_spec(width):
        return pl.BlockSpec((SUB_BATCH, TILE, width), lambda t, i: (i, t, 0))

    def head_spec(nh):
        return pl.BlockSpec((SUB_BATCH, nh, TILE, LANES), lambda t, i: (i, 0, t, 0))

    def head_spec_t(nh):
        return pl.BlockSpec((SUB_BATCH, nh * HEAD_DIM, TILE), lambda t, i: (i, 0, t))

    if split_ctx:
        streams = [x, ctx]
        stream_specs = [pl.BlockSpec((SUB_BATCH, TILE, d), lambda t, i: (i, jnp.maximum(t - 1, 0), 0)),
                        pl.BlockSpec((SUB_BATCH, TILE, d), lambda t, i: (jnp.where(t == 0, i, nbb - 1), 0, 0))]
    else:
        streams, stream_specs = [x], [tile_spec(d)]
    tab_spec = pl.BlockSpec((TILE, LANES), lambda t, i: (t, 0))
    consts = [w["vecs"], w["wp"], w["wuq"], w["wuqr"], w["wuk"], w["wuv"]]
    in_specs = (stream_specs
                + [pl.BlockSpec((1, SUB_BATCH, 3, d),
                                lambda t, i: (layer, jnp.where(t == 0, ctx_row // SUB_BATCH, i), 0, 0))]
                + [_layer_spec(c, layer) for c in consts]
                + [pl.BlockSpec(w["gmat"].shape, lambda t, i: (0, 0))] + [tab_spec] * 4)
    n_slabs = [MLA_HEADS, MLA_HEADS, MLA_HEADS, 4, 4, 4, 4, 4, 4]
    transposed = [False, False, True, False, False, False, False, False, True]
    out_specs = ([head_spec_t(nh) if tr else head_spec(nh) for nh, tr in zip(n_slabs, transposed)]
                 + [tile_spec(4 * HEAD_DIM)])
    out_shape = ([jax.ShapeDtypeStruct((b, nh * HEAD_DIM, s) if tr else (b, nh, s, LANES), BF16)
                  for nh, tr in zip(n_slabs, transposed)]
                 + [jax.ShapeDtypeStruct((b, s, 4 * HEAD_DIM), F32)])
    return pl.pallas_call(
        functools.partial(_proj_kernel, split_ctx=split_ctx),
        grid=(nt, nbb),
        in_specs=in_specs,
        out_specs=out_specs,
        out_shape=out_shape,
        compiler_params=_cparams(("arbitrary", "arbitrary")),
        name="project",
    )(*streams, mod_all, *consts, w["gmat"], *tabs)


def _dense_attn_kernel(q_ref, k_ref, v_ref, o_ref, *, q_share, ctx_tile_first):
    s_len = k_ref.shape[2]
    n_heads = k_ref.shape[1]
    v_share = n_heads * HEAD_DIM // v_ref.shape[1]

    def attend(n_keys):
        n_groups = n_keys // LANES
        n_chunks = min(KEY_CHUNKS, n_groups)
        edges = [(c * n_groups // n_chunks) * LANES for c in range(n_chunks + 1)]
        chunks = [slice(a, b) for a, b in zip(edges[:-1], edges[1:])]

        def scores(unit):
            j, hh = unit
            q = q_ref[j, hh // q_share]
            return [_dot_nt(k_ref[j, hh, sl, :], q) for sl in chunks]

        def head_out(unit, st):
            j, hh = unit
            m = functools.reduce(jnp.maximum, [jnp.max(sc, axis=0, keepdims=True) for sc in st])
            pt = [jnp.exp2(sc - m) for sc in st]
            l = sum(jnp.sum(pc, axis=0, keepdims=True) for pc in pt)
            band = slice(hh // v_share * HEAD_DIM, (hh // v_share + 1) * HEAD_DIM)
            ot = sum(_dot(v_ref[j, band, sl], pc.astype(BF16)) for sl, pc in zip(chunks, pt))
            return ot / l

        units = [(j, hh) for j in range(q_ref.shape[0]) for hh in range(n_heads)]
        st_next = scores(units[0])
        outs = []
        for u, unit in enumerate(units):
            st = st_next
            if u + 1 < len(units):
                st_next = scores(units[u + 1])
            outs.append(head_out(unit, st))
        for j in range(q_ref.shape[0]):
            o_ref[j] = jnp.concatenate(outs[j * n_heads:(j + 1) * n_heads], axis=0).T

    if ctx_tile_first:
        pl.when(pl.program_id(1) == 0)(lambda: attend(TILE))
        pl.when(pl.program_id(1) > 0)(lambda: attend(s_len))
    else:
        attend(s_len)


def _dense_attention(q, k, v, with_ctx_queries):
    b, nq_slabs, s, _ = q.shape
    nh = k.shape[1]
    nt = s // TILE
    off = 0 if with_ctx_queries else 1
    nq = nt - off
    assert b % SUB_BATCH == 0
    kern = functools.partial(_dense_attn_kernel, q_share=nh // nq_slabs,
                             ctx_tile_first=with_ctx_queries)
    return pl.pallas_call(
        kern,
        grid=(b // SUB_BATCH, nq),
        in_specs=[pl.BlockSpec((SUB_BATCH, nq_slabs, TILE, LANES), lambda i, t: (i, 0, t + off, 0)),
                  pl.BlockSpec((SUB_BATCH, nh, s, LANES), lambda i, t: (i, 0, 0, 0)),
                  pl.BlockSpec((SUB_BATCH,) + v.shape[1:], lambda i, t: (i, 0, 0))],
        out_specs=pl.BlockSpec((SUB_BATCH, TILE, 2 * LANES), lambda i, t: (i, t, 0)),
        out_shape=jax.ShapeDtypeStruct((b, nq * TILE, 2 * LANES), F32),
        compiler_params=_cparams(("arbitrary", "arbitrary")),
        name="dense_attention",
    )(q, k, v)


def _window_attn_kernel(sink_ref, q_ref, k_ref, v_ref, o_ref, *, tile_off, layer):
    s_len = k_ref.shape[2]
    n_tiles = s_len // TILE
    ti = pl.program_id(1) + tile_off
    is_lat = ti > 0
    t0 = ti * TILE
    half = TILE // 2
    left0 = pl.multiple_of(jnp.maximum(t0 - half, 0), half)
    cen0 = pl.multiple_of(t0, TILE)
    right0 = pl.multiple_of(jnp.minimum(t0 + TILE, s_len - half), half)
    r_c = lax.broadcasted_iota(jnp.int32, (TILE, TILE), 0)
    c_c = lax.broadcasted_iota(jnp.int32, (TILE, TILE), 1)
    r_s = lax.broadcasted_iota(jnp.int32, (TILE, half), 0)
    c_s = lax.broadcasted_iota(jnp.int32, (TILE, half), 1)
    ok_cen = (jnp.abs(r_c - c_c) <= WINDOW) & is_lat
    ok_left = ((r_s - c_s + half) <= WINDOW) & (ti > 1)
    ok_right = ((c_s - r_s + TILE) <= WINDOW) & is_lat & (ti < n_tiles - 1)
    def lane_groups(a):
        return [a[:, g * LANES:(g + 1) * LANES] for g in range(a.shape[1] // LANES)]

    def scores(unit):
        j, hh = unit
        q = q_ref[j, hh]
        k = lambda start, size: k_ref[j, hh, pl.ds(start, size), :]
        return (_dot_nt(q, k(0, TILE)),
                jnp.where(ok_left, _dot_nt(q, k(left0, half)), NEG_INF),
                jnp.where(ok_cen, _dot_nt(q, k(cen0, TILE)), NEG_INF),
                jnp.where(ok_right, _dot_nt(q, k(right0, half)), NEG_INF))

    def head_out(unit, sc):
        j, hh = unit
        sink = sink_ref[layer, hh] * LOG2E
        m = functools.reduce(jnp.maximum, [g for piece in sc for g in lane_groups(piece)])
        m = jnp.maximum(jnp.max(m, -1, keepdims=True), sink)
        ps = [jnp.exp2(piece - m) for piece in sc]
        den = sum(g for piece in ps for g in lane_groups(piece))
        den = jnp.sum(den, -1, keepdims=True) + jnp.exp2(sink - m)
        v = lambda start, size: v_ref[j, hh, pl.ds(start, size), :]
        vs = (v(0, TILE), v(left0, half), v(cen0, TILE), v(right0, half))
        num = sum(_dot(p.astype(BF16), vv) for p, vv in zip(ps, vs))
        return num / den

    n_heads = k_ref.shape[1]
    units = [(j, hh) for j in range(q_ref.shape[0]) for hh in range(n_heads)]
    sc_next = scores(units[0])
    outs = []
    for u, unit in enumerate(units):
        sc = sc_next
        if u + 1 < len(units):
            sc_next = scores(units[u + 1])
        outs.append(head_out(unit, sc))
    for j in range(q_ref.shape[0]):
        o = outs[j * n_heads:(j + 1) * n_heads]
        o_ref[j] = jnp.concatenate([o[0] + o[1], o[2] + o[3]], axis=-1)


def _window_attention(sink, q, k, v, with_ctx_queries, layer):
    b, nq_slabs, s, _ = q.shape
    nh = k.shape[1]
    nt = s // TILE
    off = 0 if with_ctx_queries else 1
    nq = nt - off
    assert b % SUB_BATCH == 0
    kern = functools.partial(_window_attn_kernel, tile_off=off, layer=layer)
    return pl.pallas_call(
        kern,
        grid=(b // SUB_BATCH, nq),
        in_specs=[pl.BlockSpec(memory_space=pltpu.SMEM),
                  pl.BlockSpec((SUB_BATCH, nq_slabs, TILE, LANES), lambda i, t: (i, 0, t + off, 0)),
                  pl.BlockSpec((SUB_BATCH, nh, s, LANES), lambda i, t: (i, 0, 0, 0)),
                  pl.BlockSpec((SUB_BATCH, nh, s, LANES), lambda i, t: (i, 0, 0, 0))],
        out_specs=pl.BlockSpec((SUB_BATCH, TILE, 2 * LANES), lambda i, t: (i, t, 0)),
        out_shape=jax.ShapeDtypeStruct((b, nq * TILE, 2 * LANES), F32),
        compiler_params=_cparams(("arbitrary", "arbitrary")),
        name="window_attention",
    )(sink, q, k, v)


def _lru_conv(j, prev_ref, cur_ref, next_ref, prev_ok, next_ok, vec_ref):
    t = cur_ref.shape[1]
    prev = jnp.where(prev_ok, prev_ref[j], 0.0)
    nxt = jnp.where(next_ok, next_ref[j], 0.0)
    ext = jnp.concatenate([prev, cur_ref[j], nxt], axis=0)
    n = t + 2 * SUBLANES
    c = cur_ref.shape[2]

    def tap(j):
        return vec_ref[0, VEC_CONVW + j:VEC_CONVW + j + 1, 0:c]

    u = vec_ref[0, VEC_CONVB:VEC_CONVB + 1, 0:c] + cur_ref[j] * tap(CONV_LEFT)
    for j in range(CONV_W):
        off = j - CONV_LEFT
        if off == 0:
            continue
        sh = pltpu.roll(ext, (-off) % n, 0)[SUBLANES:SUBLANES + t]
        u = u + sh * tap(j)
    return u


def _lru_scan_tile(u, d, wri_ref, vec_ref, h0, reverse):
    t, c = u.shape

    def vec(row):
        return vec_ref[0, row + d:row + d + 1, 0:c]

    gates = _dot(u.astype(BF16), wri_ref[0, d])
    r = _sigmoid(gates[:, :c] + vec(VEC_BR))
    i = _sigmoid(gates[:, c:] + vec(VEC_BI))
    lam = vec(VEC_LAM)
    softplus_neg = jnp.maximum(-lam, 0.0) + jnp.log1p(jnp.exp(-jnp.abs(lam)))
    a = jnp.exp2(r * (softplus_neg * (-LRU_C * LOG2E)))
    bx = jnp.sqrt(1.0 - a * a) * (i * u)
    row8 = lax.broadcasted_iota(jnp.int32, u.shape, 0) & (SUBLANES - 1)
    step = 1
    while step < SUBLANES:
        if reverse:
            ok = row8 < SUBLANES - step
            a_sh = jnp.where(ok, pltpu.roll(a, t - step, 0), 1.0)
            b_sh = jnp.where(ok, pltpu.roll(bx, t - step, 0), 0.0)
        else:
            ok = row8 >= step
            a_sh = jnp.where(ok, pltpu.roll(a, step, 0), 1.0)
            b_sh = jnp.where(ok, pltpu.roll(bx, step, 0), 0.0)
        bx = a * b_sh + bx
        a = a * a_sh
        step *= 2
    n_groups = t // SUBLANES
    hs = [None] * n_groups
    carry = h0
    for g in (reversed(range(n_groups)) if reverse else range(n_groups)):
        sl = slice(g * SUBLANES, (g + 1) * SUBLANES)
        hg = a[sl] * carry + bx[sl]
        hs[g] = hg
        carry = hg[0:1] if reverse else hg[SUBLANES - 1:SUBLANES]
    return jnp.concatenate(hs, axis=0), carry


def _lru_kernel(fp_ref, fc_ref, fn_ref, rp_ref, rc_ref, rn_ref, vec_ref, wri_ref,
                yf_ref, yb_ref, hf_s, hb_s):
    j = pl.program_id(1)
    nt = pl.num_programs(1)

    @pl.when(j == 0)
    def _():
        hf_s[...] = jnp.zeros_like(hf_s)
        hb_s[...] = jnp.zeros_like(hb_s)

    def bounds(ti):
        return ti >= 2, (ti >= 1) & (ti < nt - 1)

    tf = j
    tr = jnp.where(j == 0, 0, nt - j)
    for e in range(fc_ref.shape[0]):
        p_ok, n_ok = bounds(tf)
        uf = _lru_conv(e, fp_ref, fc_ref, fn_ref, p_ok, n_ok, vec_ref)
        h, h_end = _lru_scan_tile(uf, 0, wri_ref, vec_ref, hf_s[e], False)
        yf_ref[e] = h
        hf_s[e] = h_end
        p_ok, n_ok = bounds(tr)
        ur = _lru_conv(e, rp_ref, rc_ref, rn_ref, p_ok, n_ok, vec_ref)
        h, h_end = _lru_scan_tile(ur, 1, wri_ref, vec_ref, hb_s[e], True)
        yb_ref[e] = h
        hb_s[e] = h_end


def _rglru(z, w, layer):
    b, s, c = z.shape
    nt = s // TILE
    r8 = TILE // SUBLANES
    n8 = s // SUBLANES

    def rev(j):
        return jnp.where(j == 0, 0, nt - j)

    assert b % SUB_BATCH == 0

    def cur(f):
        return pl.BlockSpec((SUB_BATCH, TILE, c), lambda i, j: (i, f(j), 0))

    def prev(f):
        return pl.BlockSpec((SUB_BATCH, SUBLANES, c), lambda i, j: (i, jnp.maximum(f(j) * r8 - 1, 0), 0))

    def nxt(f):
        return pl.BlockSpec((SUB_BATCH, SUBLANES, c), lambda i, j: (i, jnp.minimum((f(j) + 1) * r8, n8 - 1), 0))

    ident = lambda j: j
    consts = [w["vecs"], w["wri"]]
    return pl.pallas_call(
        _lru_kernel,
        grid=(b // SUB_BATCH, nt),
        in_specs=[prev(ident), cur(ident), nxt(ident), prev(rev), cur(rev), nxt(rev)]
                 + [_layer_spec(x, layer) for x in consts],
        out_specs=[cur(ident), cur(rev)],
        out_shape=[jax.ShapeDtypeStruct((b, s, c), F32)] * 2,
        scratch_shapes=[pltpu.VMEM((SUB_BATCH, 1, c), F32), pltpu.VMEM((SUB_BATCH, 1, c), F32)],
        compiler_params=_cparams(("arbitrary", "arbitrary")),
        name="rglru",
    )(z, z, z, z, z, z, *consts)


def _merge_kernel(*refs, split_ctx, ctx_tile_first):
    x_ref, ctx_ref = (refs[0], refs[1]) if split_ctx else (refs[0], None)
    (mod_ref, vec_ref, wg_ref, wm_ref, ya_ref, yb_ref, yc_ref, yf_ref, yr_ref,
     wb_ref, wo_ref, o_ref) = refs[2 if split_ctx else 1:]
    is_ctx = (pl.program_id(1) == 0) if ctx_tile_first else False
    d = x_ref.shape[2]
    bw = ya_ref.shape[2]
    for j in range(x_ref.shape[0]):
        x = _stream_tile(x_ref, ctx_ref, is_ctx, j)
        mod = mod_ref[0, pl.ds(jnp.where(is_ctx, 0, j), 1)][0]
        hb = _modulated_norm(x, mod, vec_ref).astype(BF16)
        zg = _dot(hb, wg_ref[0])
        branches = (ya_ref[j], yb_ref[j], yc_ref[j], yf_ref[j] + yr_ref[j])
        mix = jnp.zeros_like(x)
        for k in range(N_BRANCH):
            g = zg[:, k * bw:(k + 1) * bw]
            y = branches[k] * (g * _sigmoid(g))
            proj = _dot(y.astype(BF16), wb_ref[0, k])
            zm = _dot(hb, wm_ref[0, :, k * d:(k + 1) * d])
            mix = mix + _sigmoid(zm) * proj
        o_ref[j] = x + mod[2:3, :] * _dot(mix.astype(BF16), wo_ref[0])


def _merge(x, ctx, mod_all, w, ya, yb, yc, yf, yr, with_ctx, layer):
    split_ctx = ctx is not None
    assert with_ctx or not split_ctx
    b, d = x.shape[0], x.shape[2]
    s = x.shape[1] + (ctx.shape[1] if split_ctx else 0)
    nt = s // TILE
    off = 0 if with_ctx else 1
    nq = nt - off
    ctx_row = b
    bw = ya.shape[2]
    assert b % SUB_BATCH == 0 and ctx_row % SUB_BATCH == 0
    nbb = b // SUB_BATCH

    def stream(width):
        return pl.BlockSpec((SUB_BATCH, TILE, width), lambda i, t: (i, t + off, 0))

    def local(width):
        return pl.BlockSpec((SUB_BATCH, TILE, width), lambda i, t: (i, t, 0))

    if split_ctx:
        streams = [x, ctx]
        stream_specs = [pl.BlockSpec((SUB_BATCH, TILE, d), lambda i, t: (i, jnp.maximum(t - 1, 0), 0)),
                        pl.BlockSpec((SUB_BATCH, TILE, d), lambda i, t: (i, 0, 0))]
    else:
        streams, stream_specs = [x], [stream(d)]
    consts_a = [w["vecs"], w["wg"], w["wm"]]
    consts_b = [w["wb"], w["wo"]]
    return pl.pallas_call(
        functools.partial(_merge_kernel, split_ctx=split_ctx, ctx_tile_first=with_ctx),
        grid=(nbb, nq),
        in_specs=stream_specs
                 + [pl.BlockSpec((1, SUB_BATCH, 3, d),
                                 lambda i, t: (layer, jnp.where(t + off == 0, ctx_row // SUB_BATCH, i), 0, 0))]
                 + [_layer_spec(c, layer) for c in consts_a]
                 + [local(bw), local(bw), local(bw), stream(bw), stream(bw)]
                 + [_layer_spec(c, layer) for c in consts_b],
        out_specs=local(d),
        out_shape=jax.ShapeDtypeStruct((b, nq * TILE, d), F32),
        compiler_params=_cparams(("arbitrary", "arbitrary")),
        name="merge",
    )(*streams, mod_all, *consts_a, ya, yb, yc, yf, yr, *consts_b)


def _rope_tables(n, n_ctx):
    def axial(rot_dim):
        n_rows = n // GRID_W
        rows = np.repeat(np.arange(n_rows, dtype=np.float64), GRID_W)
        cols = np.tile(np.arange(GRID_W, dtype=np.float64), n_rows)
        quarter = rot_dim // 4
        freqs = ROPE_THETA ** (-np.arange(quarter, dtype=np.float64) / quarter)
        ang = np.concatenate([rows[:, None] * freqs, cols[:, None] * freqs], axis=-1)
        return np.cos(ang), np.sin(ang)

    def with_ctx(cos, sin):
        return (jnp.asarray(np.concatenate([np.ones((n_ctx, LANES)), cos], axis=0), F32),
                jnp.asarray(np.concatenate([np.zeros((n_ctx, LANES)), sin], axis=0), F32))

    cm, sm = axial(MLA_ROPE)
    one, zero = np.ones((n, MLA_NOPE)), np.zeros((n, MLA_NOPE))
    pad1, pad0 = np.ones((n, LANES - MLA_QK)), np.zeros((n, LANES - MLA_QK))
    cosm, sinm = with_ctx(np.concatenate([one, cm, cm, pad1], axis=-1),
                          np.concatenate([zero, sm, sm, pad0], axis=-1))
    ch, sh = axial(HEAD_DIM)
    cosh, sinh = with_ctx(np.tile(ch, (1, 4)), np.tile(sh, (1, 4)))
    return cosm, sinm, cosh, sinh


def _place(w, mat, gain=None):
    wg = w if gain is None else w * gain[:, None, :]
    return jnp.einsum("lkc,cp->lkp", wg, jnp.asarray(mat, F32), precision=lax.Precision.HIGHEST)


def _copy_matrix(n_heads, in_width, in_start, count, out_width, out_start):
    m = np.zeros((n_heads * in_width, n_heads * out_width), np.float32)
    for h in range(n_heads):
        for j in range(count):
            m[h * in_width + in_start + j, h * out_width + out_start + j] = 1.0
    return m


def _partner_matrix(n_heads, in_width, in_start, half, out_width, out_start):
    m = np.zeros((n_heads * in_width, n_heads * out_width), np.float32)
    for h in range(n_heads):
        for j in range(half):
            x1, x2 = h * in_width + in_start + j, h * in_width + in_start + half + j
            o1, o2 = h * out_width + out_start + j, h * out_width + out_start + half + j
            m[x2, o1] = -1.0
            m[x1, o2] = 1.0
    return m


IN_SIZES = (256, 128, 32, 256, 128, 128, 256, 128, 128, 256, 1024)
PREP_ROWS = 256


def _prep_kernel(wt_ref, vec_ref, pc_ref, pm_ref, wp_o, wg_o, wm_o):
    offs = [0]
    for sz in IN_SIZES:
        offs.append(offs[-1] + sz)

    def rows(a, b):
        return wt_ref[0, a:b, :].T

    def piece(i):
        return rows(offs[i], offs[i + 1])

    def put(start, val):
        wp_o[0, :, start:start + val.shape[1]] = val.astype(BF16)

    def partner(val, gain, mat):
        return _dot((val * gain).astype(BF16), mat)

    put(0, piece(0))
    put(256, piece(1))
    z = rows(offs[2], offs[2] + LANES)
    hi = z.astype(BF16)
    r1 = z - hi.astype(F32)
    mid = r1.astype(BF16)
    lo = (r1 - mid.astype(F32)).astype(BF16)
    pc = pc_ref[...]
    kr_slab = _dot(hi, pc) + _dot(mid, pc) + _dot(lo, pc)
    put(384, kr_slab)
    put(512, partner(kr_slab, _gain(vec_ref, GAIN_MK), pm_ref[...]))
    col = 640
    for first in (3, 6):
        put(col, piece(first))
        put(col + 256, piece(first + 1))
        put(col + 384, piece(first + 2))
        col += 512
    put(col, piece(9))
    wg_o[0] = piece(10).astype(BF16)
    for c in range(offs[11], wt_ref.shape[1], 4 * LANES):
        wm_o[0, :, c - offs[11]:c - offs[11] + 4 * LANES] = rows(c, c + 4 * LANES).astype(BF16)


def _prep_in_weights(w_in, vecs):
    depth, d, cols = w_in.shape
    n_merge = cols - sum(IN_SIZES)
    mats = [_copy_matrix(1, LANES, 0, MLA_ROPE, LANES, MLA_NOPE),
            _partner_matrix(1, LANES, MLA_NOPE, MLA_ROPE // 2, LANES, MLA_NOPE)]
    mats = [jnp.asarray(m, BF16) for m in mats]
    wp_cols = 1920

    def out_rows(width):
        return pl.BlockSpec((1, PREP_ROWS, width), lambda l, r: (l, r, 0))

    return pl.pallas_call(
        _prep_kernel,
        grid=(depth, d // PREP_ROWS),
        in_specs=[pl.BlockSpec((1, cols, PREP_ROWS), lambda l, r: (l, 0, r)),
                  pl.BlockSpec((1,) + vecs.shape[1:], lambda l, r: (l, 0, 0))]
                 + [pl.BlockSpec(m.shape, lambda l, r: (0, 0)) for m in mats],
        out_specs=[out_rows(wp_cols), out_rows(IN_SIZES[10]), out_rows(n_merge)],
        out_shape=[jax.ShapeDtypeStruct((depth, d, wp_cols), BF16),
                   jax.ShapeDtypeStruct((depth, d, IN_SIZES[10]), BF16),
                   jax.ShapeDtypeStruct((depth, d, n_merge), BF16)],
        compiler_params=_cparams(("arbitrary", "arbitrary")),
        name="prep_weights",
    )(jnp.transpose(w_in, (0, 2, 1)), vecs, *mats)


def _prepare_weights(p):
    depth, d, _ = p["w_in"].shape
    q_scale = HEAD_DIM ** -0.5 * LOG2E
    mq_scale = MLA_QK ** -0.5 * LOG2E
    g_sq = jnp.tile(p["swa_q_norm"] * q_scale, (1, 2))
    g_sk = jnp.tile(p["swa_k_norm"], (1, 2))
    g_aq = jnp.tile(p["axa_q_norm"] * q_scale, (1, 2))
    g_ak = jnp.tile(p["axa_k_norm"], (1, 2))
    g_mq = p["mla_q_norm"] * mq_scale
    g_mk = p["mla_k_norm"]

    w_uq, w_ukv = p["mla_w_uq"], p["mla_w_ukv"]
    kv_width = MLA_NOPE + MLA_V
    wuq = _place(w_uq, _copy_matrix(MLA_HEADS, MLA_QK, 0, MLA_QK, LANES, 0)).astype(BF16)
    wuqr = _place(w_uq, _partner_matrix(MLA_HEADS, MLA_QK, MLA_NOPE, MLA_ROPE // 2, LANES, MLA_NOPE),
                  jnp.tile(g_mq, (1, MLA_HEADS))).astype(BF16)
    wuk = _place(w_ukv, _copy_matrix(MLA_HEADS, kv_width, 0, MLA_NOPE, LANES, 0)).astype(BF16)
    wuv = _place(w_ukv, _copy_matrix(MLA_HEADS, kv_width, MLA_NOPE, MLA_V, MLA_V, 0)).astype(BF16)

    def row(v):
        v = v if v.ndim == 3 else v[:, None, :]
        return jnp.pad(v, ((0, 0), (0, 0), (0, d - v.shape[2])))

    slab = lambda g: jnp.pad(g, ((0, 0), (0, LANES - g.shape[1])))
    gains = jnp.concatenate([p["mla_ckv_norm"], slab(g_mq), slab(g_mk), g_sq, g_sk, g_aq, g_ak], axis=1)
    vec_rows = [row(p["norm_w"]), row(p["mla_cq_norm"]), row(gains), row(p["lru_conv_w"]),
                row(p["lru_conv_b"]), row(p["lru_b_r"]), row(p["lru_b_i"]), row(p["lru_lambda"])]
    n_rows = sum(r.shape[1] for r in vec_rows)
    vecs = jnp.concatenate(vec_rows + [jnp.zeros((depth, VEC_ROWS - n_rows, d), F32)], axis=1)
    wp, wg, wm = _prep_in_weights(p["w_in"], vecs)

    idx = np.arange(4 * HEAD_DIM) // HEAD_DIM
    gmat = jnp.asarray(idx[:, None] == idx[None, :], BF16)

    def block_diag(wb):
        l, two, k, m, _ = wb.shape
        return jnp.einsum("ldkij,kn->ldkinj", wb, jnp.eye(k, dtype=wb.dtype)).reshape(l, two, k * m, k * m)

    wri = jnp.concatenate([block_diag(p["lru_w_r"]), block_diag(p["lru_w_i"])], axis=-1).astype(BF16)
    return {
        "vecs": vecs, "gmat": gmat,
        "wp": wp, "wuq": wuq, "wuqr": wuqr, "wuk": wuk, "wuv": wuv,
        "wg": wg, "wm": wm,
        "wb": p["w_branch"].astype(BF16), "wo": p["w_out"].astype(BF16),
        "wri": wri,
    }


def kernel(x, c, ctx, c_ctx, w_mod, b_mod, norm_w, w_in, mla_cq_norm, mla_ckv_norm, mla_w_uq, mla_w_ukv, mla_q_norm, mla_k_norm, swa_q_norm, swa_k_norm, swa_sink, axa_q_norm, axa_k_norm, lru_conv_w, lru_conv_b, lru_w_r, lru_b_r, lru_w_i, lru_b_i, lru_lambda, w_branch, w_out):
    p = dict(norm_w=norm_w, w_in=w_in, mla_cq_norm=mla_cq_norm, mla_ckv_norm=mla_ckv_norm,
             mla_w_uq=mla_w_uq, mla_w_ukv=mla_w_ukv, mla_q_norm=mla_q_norm, mla_k_norm=mla_k_norm,
             swa_q_norm=swa_q_norm, swa_k_norm=swa_k_norm,
             axa_q_norm=axa_q_norm, axa_k_norm=axa_k_norm, lru_conv_w=lru_conv_w,
             lru_conv_b=lru_conv_b, lru_w_r=lru_w_r, lru_b_r=lru_b_r, lru_w_i=lru_w_i,
             lru_b_i=lru_b_i, lru_lambda=lru_lambda, w_branch=w_branch, w_out=w_out)
    b, n, d = x.shape
    n_ctx = ctx.shape[1]
    depth = w_mod.shape[0]
    assert n_ctx == TILE and n % TILE == 0 and n % GRID_W == 0 and b + 1 <= SUBLANES

    rows = jnp.concatenate([c, c_ctx[None, :], jnp.zeros((SUBLANES - b - 1, d), F32)], axis=0)
    mod_all = _modulation(rows, w_mod, b_mod).reshape(depth, SUBLANES, 3, d)

    w = _prepare_weights(p)
    tabs = _rope_tables(n, n_ctx)
    xs, cs = x, ctx
    for layer in range(depth):
        upd = layer < depth - 1
        mq, mk, mv, sq, sk, sv, aq, ak, av, zl = _project(xs, cs, mod_all, w, tabs, layer)
        ya = _dense_attention(mq, mk, mv, upd)
        yb = _window_attention(swa_sink, sq, sk, sv, upd, layer)
        yc = _dense_attention(aq, ak, av, upd)
        yf, yr = _rglru(zl, w, layer)
        if cs is not None and not upd:
            xs, cs = jnp.concatenate([cs, xs], axis=1), None
        xs, cs = _merge(xs, cs, mod_all, w, ya, yb, yc, yf, yr, upd, layer), None
    return xs
```
